```python
import jax, jax.numpy as jnp
from jax import lax
import numpy as np

D_MODEL = 1024
BATCH = 8
SEQ = 4096
DEPTH = 4

POOL_WINDOWS = (2, 4, 8, 16)
POOL_WIDTH = D_MODEL // 4
POOL_GROUP = POOL_WIDTH // len(POOL_WINDOWS)
HGRN_HEADS = 4
HGRN_KDIM = D_MODEL // 16
HGRN_VDIM = D_MODEL // 16
HGRN_WIDTH = HGRN_HEADS * HGRN_VDIM
HGRN_CHUNK = 64
ATT_HEADS = 8
ATT_KV_HEADS = 2
HEAD_DIM = 64
ATT_WIDTH = ATT_HEADS * HEAD_DIM
WINDOW = 128
MIX_WIDTH = POOL_WIDTH + HGRN_WIDTH + ATT_WIDTH
SPLIT_SIZES = (POOL_WIDTH,
               HGRN_HEADS * HGRN_KDIM, HGRN_HEADS * HGRN_KDIM,
               HGRN_HEADS * HGRN_VDIM, HGRN_HEADS * HGRN_VDIM,
               ATT_HEADS * HEAD_DIM, ATT_KV_HEADS * HEAD_DIM, ATT_KV_HEADS * HEAD_DIM)
IN_WIDTH = sum(SPLIT_SIZES)
N_EXPERTS = 16
N_GROUPS = 4
EXPERTS_PER_GROUP = N_EXPERTS // N_GROUPS
TOP_K = 2
D_EXPERT = D_MODEL // 2
EPS = 1e-6
MAX_ONE_MINUS_F = 1.0 - 1e-6

kernel_name = "hybrid_pool_hgrn2_swa_groupmoe_block"


def rms_norm(x, w):
    xf = x.astype(jnp.float32)
    y = xf * lax.rsqrt(jnp.mean(xf * xf, axis=-1, keepdims=True) + EPS)
    return (y * w.astype(jnp.float32)).astype(x.dtype)


def alibi_slopes(n):
    return jnp.exp2(-8.0 * jnp.arange(1, n + 1, dtype=jnp.float32) / n)


def pool_mixer(a, pool_w, pool_scale):
    B, S, _ = a.shape
    af = a.astype(jnp.float32)
    pos = jnp.arange(S, dtype=jnp.float32)[:, None]
    outs = []
    for g, w in enumerate(POOL_WINDOWS):
        ag = af[..., g * POOL_GROUP:(g + 1) * POOL_GROUP]
        cs = jnp.cumsum(ag, axis=1)
        cs0 = jnp.pad(cs, ((0, 0), (1, 0), (0, 0)))
        lower = jnp.pad(cs0[:, :S + 1 - w], ((0, 0), (w - 1, 0), (0, 0)))
        count = jnp.minimum(pos + 1.0, float(w))
        pooled = (cs - lower) / count - ag
        outs.append(jnp.einsum('bsc,cd->bsd', pooled, pool_w[g].astype(jnp.float32)))
    y = jnp.concatenate(outs, axis=-1) * pool_scale.astype(jnp.float32)
    return y.astype(a.dtype)


def hgrn2_mixer(q, f, i, g, lb, norm_w):
    B, S, _ = q.shape
    H, K, V, C = HGRN_HEADS, HGRN_KDIM, HGRN_VDIM, HGRN_CHUNK
    n = S // C
    fz = f.astype(jnp.float32)
    lbf = lb.astype(jnp.float32)
    k = (1.0 - lbf) * jax.nn.sigmoid(-fz)
    log_f = jnp.log1p(-jnp.minimum(k, MAX_ONE_MINUS_F))
    qf = jax.nn.silu(q.astype(jnp.float32))
    vf = i.astype(jnp.float32)

    def to_chunks(t, d):
        return t.reshape(B, n, C, H, d).transpose(1, 0, 3, 2, 4)

    xs = (to_chunks(qf, K), to_chunks(log_f, K), to_chunks(k, K), to_chunks(vf, V))
    causal = (jnp.arange(C)[:, None] >= jnp.arange(C)[None, :])[:, :, None]

    def step(state, inp):
        qc, lfc, kc, vc = inp
        b = jnp.cumsum(lfc, axis=2)
        diff = b[:, :, :, None, :] - b[:, :, None, :, :]
        decay = jnp.exp(jnp.where(causal, diff, -jnp.inf))
        scores = jnp.einsum('bhtk,bhtsk,bhsk->bhts', qc, decay, kc)
        o = (jnp.einsum('bhts,bhsv->bhtv', scores, vc)
             + jnp.einsum('bhtk,bhkv->bhtv', qc * jnp.exp(b), state))
        b_last = b[:, :, -1:, :]
        state = (jnp.exp(b_last[:, :, 0, :])[..., None] * state
                 + jnp.einsum('bhsk,bhsv->bhkv', kc * jnp.exp(b_last - b), vc))
        return state, o

    state0 = jnp.zeros((B, H, K, V), jnp.float32)
    _, o = lax.scan(step, state0, xs)
    o = o.transpose(1, 0, 3, 2, 4).reshape(B, S, H, V)
    o = o * lax.rsqrt(jnp.mean(o * o, axis=-1, keepdims=True) + EPS)
    o = o.reshape(B, S, H * V) * norm_w.astype(jnp.float32) * jax.nn.silu(g.astype(jnp.float32))
    return o.astype(q.dtype)


def swa_attention(q, k, v, q_norm_w, k_norm_w, sinks):
    B, S, _ = q.shape
    Hq, Hkv, hd, W = ATT_HEADS, ATT_KV_HEADS, HEAD_DIM, WINDOW
    G = Hq // Hkv
    nb = S // W
    q = rms_norm(q.reshape(B, S, Hq, hd), q_norm_w)
    k = rms_norm(k.reshape(B, S, Hkv, hd), k_norm_w)
    v = v.reshape(B, S, Hkv, hd)
    qb = q.reshape(B, nb, W, Hkv, G, hd)

    def band(t):
        tb = jnp.pad(t, ((0, 0), (W, 0), (0, 0), (0, 0))).reshape(B, nb + 1, W, Hkv, hd)
        return jnp.concatenate([tb[:, :-1], tb[:, 1:]], axis=2)

    kb, vb = band(k), band(v)
    logits = jnp.einsum('bnqhgd,bnkhd->bnhgqk', qb, kb).astype(jnp.float32) * (hd ** -0.5)
    dist = jnp.arange(W)[:, None] + W - jnp.arange(2 * W)[None, :]
    kpos = jnp.arange(nb)[:, None] * W - W + jnp.arange(2 * W)[None, :]
    valid = ((dist >= 0) & (dist < W))[None] & (kpos >= 0)[:, None, :]
    slopes = alibi_slopes(Hq).reshape(Hkv, G)
    logits = logits - slopes[:, :, None, None] * dist.astype(jnp.float32)
    logits = jnp.where(valid[None, :, None, None], logits, -jnp.inf)
    sink = sinks.astype(jnp.float32).reshape(1, 1, Hkv, G, 1, 1)
    m = jnp.maximum(jnp.max(logits, axis=-1, keepdims=True), sink)
    p = jnp.exp(logits - m)
    p = p / (jnp.sum(p, axis=-1, keepdims=True) + jnp.exp(sink - m))
    o = jnp.einsum('bnhgqk,bnkhd->bnqhgd', p, vb.astype(jnp.float32))
    return o.reshape(B, S, Hq * hd).astype(q.dtype)


def grouped_moe(h, router_w, router_bias, w_gate, w_up, w_down):
    B, S, D = h.shape
    t = h.reshape(B * S, D)
    scores = jax.nn.softmax((t @ router_w).astype(jnp.float32), axis=-1)
    sel = (scores + router_bias.astype(jnp.float32)).reshape(-1, N_GROUPS, EXPERTS_PER_GROUP)
    group_score = jnp.sum(lax.top_k(sel, TOP_K)[0], axis=-1)
    best = jnp.argmax(group_score, axis=-1)
    in_group = (best[:, None] == jnp.arange(N_GROUPS)[None, :])[:, :, None]
    sel = jnp.where(in_group, sel, -jnp.inf).reshape(-1, N_EXPERTS)
    _, idx = lax.top_k(sel, TOP_K)
    w = jnp.take_along_axis(scores, idx, axis=-1)
    w = w / jnp.sum(w, axis=-1, keepdims=True)
    gates = jnp.sum(jax.nn.one_hot(idx, N_EXPERTS, dtype=jnp.float32) * w[..., None], axis=1)
    y = jnp.zeros((B * S, D), jnp.float32)
    for e in range(N_EXPERTS):
        he = jax.nn.silu(t @ w_gate[e]) * (t @ w_up[e])
        y = y + gates[:, e:e + 1] * (he @ w_down[e]).astype(jnp.float32)
    return y.reshape(B, S, D).astype(h.dtype)


def setup_inputs(seed: int = 0) -> dict:
    key = jax.random.key(seed)
    ks = jax.random.split(key, 21)

    def nrm(k, shape, s):
        return jax.random.normal(k, shape, jnp.float32) * s

    return {
        "x": nrm(ks[0], (BATCH, SEQ, D_MODEL), 1.0),
        "c": nrm(ks[1], (BATCH, D_MODEL), 1.0),
        "ada_w": nrm(ks[2], (DEPTH, D_MODEL, 6 * D_MODEL), 0.1 * D_MODEL ** -0.5),
        "ada_b": nrm(ks[3], (DEPTH, 6 * D_MODEL), 0.2),
        "norm1_w": 1.0 + nrm(ks[4], (DEPTH, D_MODEL), 0.05),
        "norm2_w": 1.0 + nrm(ks[5], (DEPTH, D_MODEL), 0.05),
        "w_in": nrm(ks[6], (DEPTH, D_MODEL, IN_WIDTH), D_MODEL ** -0.5),
        "pool_w": nrm(ks[7], (DEPTH, len(POOL_WINDOWS), POOL_GROUP, POOL_GROUP), POOL_GROUP ** -0.5),
        "pool_scale": 1.0 + nrm(ks[8], (DEPTH, POOL_WIDTH), 0.1),
        "hgrn_lb_raw": nrm(ks[9], (DEPTH, HGRN_HEADS * HGRN_KDIM), 0.5),
        "hgrn_norm_w": 1.0 + nrm(ks[10], (DEPTH, HGRN_WIDTH), 0.05),
        "q_norm_w": 1.0 + nrm(ks[11], (DEPTH, HEAD_DIM), 0.05),
        "k_norm_w": 1.0 + nrm(ks[12], (DEPTH, HEAD_DIM), 0.05),
        "attn_sinks": nrm(ks[13], (DEPTH, ATT_HEADS), 1.0),
        "w_out": nrm(ks[14], (DEPTH, MIX_WIDTH, D_MODEL), MIX_WIDTH ** -0.5),
        "router_w": nrm(ks[15], (D_MODEL, N_EXPERTS), D_MODEL ** -0.5),
        "router_bias": nrm(ks[16], (N_EXPERTS,), 0.01),
        "expert_w_gate": nrm(ks[17], (DEPTH, N_EXPERTS, D_MODEL, D_EXPERT), D_MODEL ** -0.5),
        "expert_w_up": nrm(ks[18], (DEPTH, N_EXPERTS, D_MODEL, D_EXPERT), D_MODEL ** -0.5),
        "expert_w_down": nrm(ks[19], (DEPTH, N_EXPERTS, D_EXPERT, D_MODEL), D_EXPERT ** -0.5),
    }


def reference(x, c, ada_w, ada_b, norm1_w, norm2_w, w_in, pool_w, pool_scale, hgrn_lb_raw,
              hgrn_norm_w, q_norm_w, k_norm_w, attn_sinks, w_out, router_w, router_bias,
              expert_w_gate, expert_w_up, expert_w_down):
    p = jax.nn.softmax(hgrn_lb_raw.astype(jnp.float32), axis=0)
    lower_bounds = jnp.maximum(jnp.cumsum(p, axis=0) - p[0:1], 0.0)
    split_points = []
    acc = 0
    for s in SPLIT_SIZES[:-1]:
        acc += s
        split_points.append(acc)
    cond = jax.nn.silu(c)
    for l in range(DEPTH):
        mod = (cond @ ada_w[l] + ada_b[l]).astype(jnp.float32)
        sh1, sc1, g1, sh2, sc2, g2 = jnp.split(mod, 6, axis=-1)
        h = (rms_norm(x, norm1_w[l]).astype(jnp.float32) * (1.0 + sc1[:, None]) + sh1[:, None]).astype(x.dtype)
        z = h @ w_in[l]
        a_in, hq, hf, hi, hg, aq, ak, av = jnp.split(z, split_points, axis=-1)
        y_pool = pool_mixer(a_in, pool_w[l], pool_scale[l])
        y_hgrn = hgrn2_mixer(hq, hf, hi, hg, lower_bounds[l], hgrn_norm_w[l])
        y_attn = swa_attention(aq, ak, av, q_norm_w[l], k_norm_w[l], attn_sinks[l])
        mix = jnp.concatenate([y_pool, y_hgrn, y_attn], axis=-1) @ w_out[l]
        x = x + (g1[:, None] * mix.astype(jnp.float32)).astype(x.dtype)
        h2 = (rms_norm(x, norm2_w[l]).astype(jnp.float32) * (1.0 + sc2[:, None]) + sh2[:, None]).astype(x.dtype)
        y_moe = grouped_moe(h2, router_w, router_bias, expert_w_gate[l], expert_w_up[l], expert_w_down[l])
        x = x + (g2[:, None] * y_moe.astype(jnp.float32)).astype(x.dtype)
    return x
```

```python
import functools

import numpy as np
import jax
import jax.numpy as jnp
from jax import lax
from jax.experimental import pallas as pl
from jax.experimental.pallas import tpu as pltpu

F32 = jnp.float32
BF16 = jnp.bfloat16

D_MODEL = 1024
POOL_WINDOWS = (2, 4, 8, 16)
POOL_WIDTH = 256
POOL_GROUP = 64
POOL_HALO = 16
HGRN_HEADS = 4
HGRN_KDIM = 64
HGRN_WIDTH = 256
HGRN_CHUNK = 64
ATT_HEADS = 8
ATT_KV_HEADS = 2
HEAD_DIM = 64
ATT_WIDTH = 512
KV_WIDTH = 128
WINDOW = 128
N_EXPERTS = 16
N_GROUPS = 4
EXPERTS_PER_GROUP = 4
D_EXPERT = 512
EPS = 1e-6
MAX_ONE_MINUS_F = 1.0 - 1e-6
NEG = -1e30

VMEM_LIMIT = 48 * 1024 * 1024

SEQ_TILE = 512
MOE_TOK_TILE = 256
ROW_ALIGN = 8
MOE_SLOTS = 640
EXPERT_TILE = 512
CHUNK_SIZES = (256, 128, 64, 32, 16, 8)


def _sigmoid(x):
    return 1.0 / (1.0 + jnp.exp(-x))


def _silu(x):
    return x * _sigmoid(x)


def _cparams(sem):
    return pltpu.CompilerParams(dimension_semantics=sem, vmem_limit_bytes=VMEM_LIMIT)


def _dot(a, b):
    return jnp.dot(a, b, preferred_element_type=F32)


def _dot_nt(a, b):
    return lax.dot_general(a, b, (((1,), (1,)), ((), ())), preferred_element_type=F32)


def _dot_tn(a, b):
    return lax.dot_general(a, b, (((0,), (0,)), ((), ())), preferred_element_type=F32)


def _split2(x):
    hi = x.astype(BF16)
    lo = (x - hi.astype(F32)).astype(BF16)
    return hi, lo


def _head_sumsq(x, bd):
    hi, lo = _split2(x * x)
    return _dot(hi, bd) + _dot(lo, bd)


def _mod_kernel(c_ref, w_ref, b_ref, o_ref):
    cond = _silu(c_ref[...])
    o_ref[0] = _dot(cond.astype(BF16), w_ref[0].astype(BF16)) + b_ref[0]


def _modulation(c, ada_w, ada_b):
    depth, d, n = ada_w.shape
    b = c.shape[0]
    nb = n // d
    return pl.pallas_call(
        _mod_kernel,
        grid=(depth, nb),
        in_specs=[pl.BlockSpec((b, d), lambda l, j: (0, 0)),
                  pl.BlockSpec((1, d, d), lambda l, j: (l, 0, j)),
                  pl.BlockSpec((1, 1, d), lambda l, j: (l, 0, j))],
        out_specs=pl.BlockSpec((1, b, d), lambda l, j: (l, 0, j)),
        out_shape=jax.ShapeDtypeStruct((depth, b, n), F32),
        compiler_params=_cparams(("parallel", "parallel")),
        name="adaln_mod",
    )(c, ada_w, ada_b.reshape(depth, 1, n))


def _in_kernel(x_ref, sc_ref, sh_ref, nw_ref, w_ref, za_ref, zh_ref, zq_ref, zkv_ref):
    x = x_ref[0]
    ms = jnp.mean(x * x, axis=-1, keepdims=True)
    h = (x * lax.rsqrt(ms + EPS) * nw_ref[...]) * (1.0 + sc_ref[0]) + sh_ref[0]
    z = _dot(h.astype(BF16), w_ref[...])
    za_ref[0] = z[:, :POOL_WIDTH]
    zh_ref[0] = z[:, POOL_WIDTH:POOL_WIDTH + 4 * HGRN_WIDTH]
    zq_ref[0] = z[:, POOL_WIDTH + 4 * HGRN_WIDTH:POOL_WIDTH + 4 * HGRN_WIDTH + ATT_WIDTH]
    zkv_ref[0] = z[:, POOL_WIDTH + 4 * HGRN_WIDTH + ATT_WIDTH:]


def _in_proj(x, sc, sh, nw, w_in_b):
    b, s, d = x.shape
    n = w_in_b.shape[1]
    ts = min(SEQ_TILE, s)
    tok = lambda w: pl.BlockSpec((1, ts, w), lambda bi, i: (bi, i, 0))
    vec = pl.BlockSpec((1, 1, d), lambda bi, i: (bi, 0, 0))
    widths = (POOL_WIDTH, 4 * HGRN_WIDTH, ATT_WIDTH, 2 * KV_WIDTH)
    return pl.pallas_call(
        _in_kernel,
        grid=(b, s // ts),
        in_specs=[tok(d), vec, vec,
                  pl.BlockSpec((1, d), lambda bi, i: (0, 0)),
                  pl.BlockSpec((d, n), lambda bi, i: (0, 0))],
        out_specs=[tok(w) for w in widths],
        out_shape=[jax.ShapeDtypeStruct((b, s, w), F32) for w in widths],
        compiler_params=_cparams(("parallel", "parallel")),
        name="in_proj",
    )(x, sc.reshape(b, 1, d), sh.reshape(b, 1, d), nw.reshape(1, d), w_in_b)


def _pool_kernel(a_ref, w_ref, scale_ref, o_ref, buf_ref):
    i = pl.program_id(1)
    ts = a_ref.shape[1]
    a = a_ref[0]

    @pl.when(i == 0)
    def _():
        buf_ref[0:POOL_HALO, :] = jnp.zeros((POOL_HALO, POOL_WIDTH), F32)

    buf_ref[POOL_HALO:POOL_HALO + ts, :] = a
    lane = lax.broadcasted_iota(jnp.int32, (1, POOL_WIDTH), 1)
    win = jnp.left_shift(2, lane // POOL_GROUP)
    acc = a
    for j in range(1, POOL_HALO):
        shifted = buf_ref[POOL_HALO - j:POOL_HALO - j + ts, :]
        acc = acc + jnp.where(win > j, shifted, 0.0)
    pos = i * ts + lax.broadcasted_iota(jnp.int32, (ts, 1), 0)
    count = jnp.minimum(pos + 1, win).astype(F32)
    pooled = acc / count - a
    y = _dot(pooled.astype(BF16), w_ref[...]) * scale_ref[...]
    o_ref[0] = y.astype(o_ref.dtype)
    buf_ref[0:POOL_HALO, :] = a[ts - POOL_HALO:, :]


def _pool_mixer(za, pool_bd, pool_scale):
    b, s, w = za.shape
    ts = min(SEQ_TILE, s)
    return pl.pallas_call(
        _pool_kernel,
        grid=(b, s // ts),
        in_specs=[pl.BlockSpec((1, ts, w), lambda bi, i: (bi, i, 0)),
                  pl.BlockSpec((w, w), lambda bi, i: (0, 0)),
                  pl.BlockSpec((1, w), lambda bi, i: (0, 0))],
        out_specs=pl.BlockSpec((1, ts, w), lambda bi, i: (bi, i, 0)),
        out_shape=jax.ShapeDtypeStruct((b, s, w), BF16),
        scratch_shapes=[pltpu.VMEM((POOL_HALO + ts, w), F32)],
        compiler_params=_cparams(("parallel", "arbitrary")),
        name="pool_mixer",
    )(za, pool_bd, pool_scale.reshape(1, w))


def _block_diag(blocks):
    g, n, _ = blocks.shape
    eye = jnp.eye(g, dtype=blocks.dtype)
    return (eye[:, None, :, None] * blocks[:, :, None, :]).reshape(g * n, g * n)


def _head_ones(width, head):
    idx = np.arange(width) // head
    return jnp.asarray((idx[:, None] == idx[None, :]).astype(np.float32), dtype=BF16)


HGRN_LEVELS = (32, 16, 8, 4, 2, 1)
N_LEVEL_MASKS = len(HGRN_LEVELS) + 1
ROW_B = len(HGRN_LEVELS) * HGRN_CHUNK
ROW_R = ROW_B + HGRN_CHUNK


def _hgrn_constants():
    c = HGRN_CHUNK
    m = np.zeros((len(HGRN_LEVELS) + 2, c, c), np.float32)
    masks = np.zeros((N_LEVEL_MASKS, c, c), np.float32)
    for lvl, n in enumerate(HGRN_LEVELS):
        for t in range(c):
            blk = t // (2 * n)
            mid = blk * 2 * n + n
            if t >= mid:
                m[lvl, t, mid:t + 1] = 1.0
                masks[lvl, t, blk * 2 * n:mid] = 1.0
            else:
                m[lvl, t, t + 1:mid] = 1.0
    for t in range(c):
        m[-2, t, :t + 1] = 1.0
        m[-1, t, t + 1:] = 1.0
        masks[-1, t, t] = 1.0
    m = m.reshape(-1, c)
    m3 = np.concatenate([m, m, m], axis=1)
    masks = np.tile(masks, (1, HGRN_HEADS, 1))
    return jnp.asarray(m3, dtype=BF16), jnp.asarray(masks, dtype=F32)


def _hgrn_kernel(zh_ref, lb_ref, gw_ref, m3_ref, lmask_ref, bd_ref, o_ref, st_ref):
    i = pl.program_id(1)
    ts = zh_ref.shape[1]
    c = HGRN_CHUNK
    w = HGRN_WIDTH

    @pl.when(i == 0)
    def _():
        st_ref[...] = jnp.zeros((w, w), F32)

    lane_head = lax.broadcasted_iota(jnp.int32, (1, w), 1) // HGRN_KDIM
    one_minus_lb = 1.0 - lb_ref[...]
    gw = gw_ref[...]
    bd = bd_ref[...]
    on_diag_block = bd > 0

    def chunk(ci, carry):
        r0 = pl.multiple_of(ci * c, c)
        q = zh_ref[0, pl.ds(r0, c), 0:w]
        f = zh_ref[0, pl.ds(r0, c), w:2 * w]
        v = zh_ref[0, pl.ds(r0, c), 2 * w:3 * w].astype(BF16)
        g = zh_ref[0, pl.ds(r0, c), 3 * w:4 * w]
        kk = one_minus_lb * _sigmoid(-f)
        lf = jnp.log(1.0 - jnp.minimum(kk, MAX_ONE_MINUS_F))
        qf = _silu(q)
        hi = lf.astype(BF16)
        r1 = lf - hi.astype(F32)
        mid = r1.astype(BF16)
        lo = (r1 - mid.astype(F32)).astype(BF16)
        sums = _dot(m3_ref[...], jnp.concatenate([hi, mid, lo], axis=0))
        dec = jnp.exp(sums)

        scores = jnp.zeros((HGRN_HEADS * c, c), F32)
        for lvl in range(N_LEVEL_MASKS):
            if lvl < len(HGRN_LEVELS):
                e = dec[lvl * c:(lvl + 1) * c]
                ql = (qf * e).astype(BF16)
                kl = (kk * e).astype(BF16)
            else:
                ql = qf.astype(BF16)
                kl = kk.astype(BF16)
            zero = jnp.zeros_like(ql)
            qs = jnp.concatenate([jnp.where(lane_head == h, ql, zero) for h in range(HGRN_HEADS)], axis=0)
            scores = scores + _dot_nt(qs, kl) * lmask_ref[lvl]
        r = _dot(scores.astype(BF16), v)
        o = jnp.zeros((c, w), F32)
        for h in range(HGRN_HEADS):
            o = o + jnp.where(lane_head == h, r[h * c:(h + 1) * c], 0.0)
        eb = dec[ROW_B:ROW_B + c]
        er = dec[ROW_R:ROW_R + c]
        st = st_ref[...]
        o = o + _dot_nt((qf * eb).astype(BF16), st.astype(BF16))
        upd = _dot_tn(v, (kk * er).astype(BF16))
        st_ref[...] = st * eb[c - 1:c, :] + jnp.where(on_diag_block, upd, 0.0)
        ss = _head_sumsq(o, bd)
        y = o * lax.rsqrt(ss * (1.0 / HGRN_KDIM) + EPS) * gw * _silu(g)
        o_ref[0, pl.ds(r0, c), :] = y.astype(o_ref.dtype)
        return carry

    lax.fori_loop(0, ts // c, chunk, 0)


def _hgrn_mixer(zh, lb, norm_w, consts):
    b, s, _ = zh.shape
    w = HGRN_WIDTH
    ts = min(SEQ_TILE, s)
    m3, lmask, bd = consts
    full = lambda a: pl.BlockSpec(a.shape, lambda bi, i: (0,) * a.ndim)
    return pl.pallas_call(
        _hgrn_kernel,
        grid=(b, s // ts),
        in_specs=[pl.BlockSpec((1, ts, 4 * w), lambda bi, i: (bi, i, 0)),
                  pl.BlockSpec((1, w), lambda bi, i: (0, 0)),
                  pl.BlockSpec((1, w), lambda bi, i: (0, 0)),
                  full(m3), full(lmask), full(bd)],
        out_specs=pl.BlockSpec((1, ts, w), lambda bi, i: (bi, i, 0)),
        out_shape=jax.ShapeDtypeStruct((b, s, w), BF16),
        scratch_shapes=[pltpu.VMEM((w, w), F32)],
        compiler_params=_cparams(("parallel", "arbitrary")),
        name="hgrn_mixer",
    )(zh, lb.reshape(1, w), norm_w.reshape(1, w), m3, lmask, bd)


def _attn_bias():
    qi = np.arange(WINDOW)[:, None]
    kj = np.arange(2 * WINDOW)[None, :]
    dist = qi + WINDOW - kj
    valid = (dist >= 0) & (dist < WINDOW)
    slopes = np.exp2(-8.0 * np.arange(1, ATT_HEADS + 1) / ATT_HEADS)
    bias = np.where(valid[None], -slopes[:, None, None] * dist[None], NEG)
    return jnp.asarray(bias, dtype=F32)


def _attn_kernel(sink_ref, zq_ref, zkv_ref, qw_ref, kw_ref, bias_ref, bdq_ref, bdk_ref, o_ref,
                 qbuf, kbuf, vbuf):
    i = pl.program_id(1)
    ts = zq_ref.shape[1]
    hw = 2 * HEAD_DIM

    @pl.when(i == 0)
    def _():
        kbuf[:, 0:WINDOW, :] = jnp.zeros((4, WINDOW, hw), BF16)
        vbuf[:, 0:WINDOW, :] = jnp.zeros((4, WINDOW, hw), BF16)

    q = zq_ref[0]
    ssq = _head_sumsq(q, bdq_ref[...])
    qbuf[...] = (q * lax.rsqrt(ssq * (1.0 / HEAD_DIM) + EPS) * (qw_ref[...] * HEAD_DIM ** -0.5)).astype(BF16)
    kv = zkv_ref[0]
    k = kv[:, :KV_WIDTH]
    v = kv[:, KV_WIDTH:]
    ssk = _head_sumsq(k, bdk_ref[...])
    kn = k * lax.rsqrt(ssk * (1.0 / HEAD_DIM) + EPS) * kw_ref[...]
    kr = pltpu.roll(kn, HEAD_DIM, 1)
    vr = pltpu.roll(v, HEAD_DIM, 1)
    low = lax.broadcasted_iota(jnp.int32, (1, hw), 1) < HEAD_DIM
    for j in range(ATT_KV_HEADS):
        for half in range(2):
            keep = low if half == 0 else jnp.logical_not(low)
            ksrc = kn if j == half else kr
            vsrc = v if j == half else vr
            kbuf[2 * j + half, WINDOW:WINDOW + ts, :] = jnp.where(keep, ksrc, 0.0).astype(BF16)
            vbuf[2 * j + half, WINDOW:WINDOW + ts, :] = jnp.where(keep, vsrc, 0.0).astype(BF16)

    prev_cols = lax.broadcasted_iota(jnp.int32, (1, 2 * WINDOW), 1) < WINDOW

    def block(n, carry):
        r0 = pl.multiple_of(n * WINDOW, WINDOW)
        no_prev = jnp.logical_and(i == 0, n == 0)
        pen = jnp.where(jnp.logical_and(prev_cols, no_prev), NEG, 0.0)
        for hp in range(ATT_HEADS // 2):
            qp = qbuf[pl.ds(r0, WINDOW), hp * hw:(hp + 1) * hw]
            acc = jnp.zeros((WINDOW, hw), F32)
            for half in range(2):
                h = 2 * hp + half
                j = h // (ATT_HEADS // ATT_KV_HEADS)
                keys = kbuf[2 * j + half, pl.ds(r0, 2 * WINDOW), :]
                vals = vbuf[2 * j + half, pl.ds(r0, 2 * WINDOW), :]
                logits = _dot_nt(qp, keys) + bias_ref[h] + pen
                sink = sink_ref[h]
                m = jnp.maximum(jnp.max(logits, axis=-1, keepdims=True), sink)
                p = jnp.exp(logits - m)
                denom = jnp.sum(p, axis=-1, keepdims=True) + jnp.exp(sink - m)
                acc = acc + _dot(p.astype(BF16), vals) * (1.0 / denom)
            o_ref[0, pl.ds(r0, WINDOW), hp * hw:(hp + 1) * hw] = acc.astype(o_ref.dtype)
        return carry

    lax.fori_loop(0, ts // WINDOW, block, 0)
    kbuf[:, 0:WINDOW, :] = kbuf[:, ts:ts + WINDOW, :]
    vbuf[:, 0:WINDOW, :] = vbuf[:, ts:ts + WINDOW, :]


def _attn_mixer(zq, zkv, q_norm_w, k_norm_w, sinks, consts):
    b, s, _ = zq.shape
    ts = min(SEQ_TILE, s)
    bias, bdq, bdk = consts
    hw = 2 * HEAD_DIM
    qw = jnp.tile(q_norm_w, ATT_HEADS).reshape(1, ATT_WIDTH)
    kw = jnp.tile(k_norm_w, ATT_KV_HEADS).reshape(1, KV_WIDTH)
    full = lambda a: pl.BlockSpec(a.shape, lambda bi, i, sk: (0,) * a.ndim)
    grid_spec = pltpu.PrefetchScalarGridSpec(
        num_scalar_prefetch=1,
        grid=(b, s // ts),
        in_specs=[pl.BlockSpec((1, ts, ATT_WIDTH), lambda bi, i, sk: (bi, i, 0)),
                  pl.BlockSpec((1, ts, 2 * KV_WIDTH), lambda bi, i, sk: (bi, i, 0)),
                  full(qw), full(kw), full(bias), full(bdq), full(bdk)],
        out_specs=pl.BlockSpec((1, ts, ATT_WIDTH), lambda bi, i, sk: (bi, i, 0)),
        scratch_shapes=[pltpu.VMEM((ts, ATT_WIDTH), BF16),
                        pltpu.VMEM((4, WINDOW + ts, hw), BF16),
                        pltpu.VMEM((4, WINDOW + ts, hw), BF16)],
    )
    return pl.pallas_call(
        _attn_kernel,
        grid_spec=grid_spec,
        out_shape=jax.ShapeDtypeStruct((b, s, ATT_WIDTH), BF16),
        compiler_params=_cparams(("parallel", "arbitrary")),
        name="swa_mixer",
    )(sinks, zq, zkv, qw, kw, bias, bdq, bdk)


def _router_constants(tm):
    t = np.arange(tm)
    before = (t[:, None] < t[None, :]).astype(np.float32)
    e = np.arange(N_EXPERTS)
    lower = (e[None, :] < e[:, None]).astype(np.float32)
    return (jnp.asarray(before, dtype=BF16), jnp.asarray(np.ones((tm, tm), np.float32), dtype=BF16),
            jnp.asarray(lower, dtype=BF16))


def _out_kernel(yp_ref, yh_ref, ya_ref, x_ref, g1_ref, sc_ref, sh_ref, nw_ref, wo_ref, rwt_ref, rb_ref,
                before_ref, ones_ref, lower_ref, x1_ref, h2_ref, ls_ref, gate_ref, cnt_ref):
    tm = x_ref.shape[0]
    p0, p1 = POOL_WIDTH, POOL_WIDTH + HGRN_WIDTH
    mix = (_dot(yp_ref[...], wo_ref[0:p0, :]) + _dot(yh_ref[...], wo_ref[p0:p1, :])
           + _dot(ya_ref[...], wo_ref[p1:, :]))
    x1 = x_ref[...] + g1_ref[0] * mix
    x1_ref[...] = x1
    ms = jnp.mean(x1 * x1, axis=-1, keepdims=True)
    h2 = ((x1 * lax.rsqrt(ms + EPS) * nw_ref[...]) * (1.0 + sc_ref[0]) + sh_ref[0]).astype(BF16)
    h2_ref[...] = h2

    logits = _dot_nt(rwt_ref[...], h2)
    ex = jnp.exp(logits - jnp.max(logits, axis=0, keepdims=True))
    scores = ex / jnp.sum(ex, axis=0, keepdims=True)
    sel = scores + rb_ref[...]
    group_scores = []
    for g in range(N_GROUPS):
        rows = [sel[g * EXPERTS_PER_GROUP + a:g * EXPERTS_PER_GROUP + a + 1] for a in range(EXPERTS_PER_GROUP)]
        best_pair = None
        for a in range(EXPERTS_PER_GROUP):
            for bb in range(a + 1, EXPERTS_PER_GROUP):
                pair = rows[a] + rows[bb]
                best_pair = pair if best_pair is None else jnp.maximum(best_pair, pair)
        group_scores.append(best_pair)
    top = functools.reduce(jnp.maximum, group_scores)
    best = jnp.full((1, tm), N_GROUPS - 1, jnp.int32)
    for g in reversed(range(N_GROUPS - 1)):
        best = jnp.where(group_scores[g] == top, g, best)
    row = lax.broadcasted_iota(jnp.int32, (N_EXPERTS, tm), 0)
    cand = jnp.where(row // EXPERTS_PER_GROUP == best, sel, NEG)
    m1 = jnp.max(cand, axis=0, keepdims=True)
    i1 = jnp.min(jnp.where(cand == m1, row, N_EXPERTS), axis=0, keepdims=True)
    oh1 = row == i1
    cand = jnp.where(oh1, NEG, cand)
    m2 = jnp.max(cand, axis=0, keepdims=True)
    i2 = jnp.min(jnp.where(cand == m2, row, N_EXPERTS), axis=0, keepdims=True)
    oh2 = row == i2
    s1 = jnp.sum(jnp.where(oh1, scores, 0.0), axis=0, keepdims=True)
    s2 = jnp.sum(jnp.where(oh2, scores, 0.0), axis=0, keepdims=True)
    gate_ref[...] = jnp.concatenate([s1 / (s1 + s2), s2 / (s1 + s2)], axis=0)

    chosen = jnp.where(jnp.logical_or(oh1, oh2), 1.0, 0.0).astype(BF16)
    rank = _dot(chosen, before_ref[...])
    count = _dot(chosen, ones_ref[...])
    aligned = jnp.floor((count + (ROW_ALIGN - 1)) * (1.0 / ROW_ALIGN)) * ROW_ALIGN
    slot = _dot(lower_ref[...], aligned.astype(BF16)) + rank
    ls_ref[...] = jnp.concatenate([jnp.sum(jnp.where(oh1, slot, 0.0), axis=0, keepdims=True),
                                   jnp.sum(jnp.where(oh2, slot, 0.0), axis=0, keepdims=True)], axis=0)
    cnt_ref[0] = count[:, 0:128]


def _out_proj_router(yp, yh, ya, x, g1, sc, sh, nw, wo_b, rwt_b, rb, consts):
    b, s, d = x.shape
    t = b * s
    tm = MOE_TOK_TILE
    per_batch = s // tm
    nt = t // tm
    before, ones, lower = consts
    tok = lambda w: pl.BlockSpec((tm, w), lambda i: (i, 0))
    vec = pl.BlockSpec((1, 1, d), lambda i: (i // per_batch, 0, 0))
    full = lambda a: pl.BlockSpec(a.shape, lambda i: (0,) * a.ndim)
    lanes = pl.BlockSpec((2, tm), lambda i: (0, i))
    return pl.pallas_call(
        _out_kernel,
        grid=(nt,),
        in_specs=[tok(POOL_WIDTH), tok(HGRN_WIDTH), tok(ATT_WIDTH), tok(d), vec, vec, vec,
                  pl.BlockSpec((1, d), lambda i: (0, 0)), full(wo_b), full(rwt_b),
                  pl.BlockSpec((N_EXPERTS, 1), lambda i: (0, 0)), full(before), full(ones), full(lower)],
        out_specs=[tok(d), tok(d), lanes, lanes, pl.BlockSpec((1, N_EXPERTS, 128), lambda i: (i, 0, 0))],
        out_shape=[jax.ShapeDtypeStruct((t, d), F32), jax.ShapeDtypeStruct((t, d), BF16),
                   jax.ShapeDtypeStruct((2, t), F32), jax.ShapeDtypeStruct((2, t), F32),
                   jax.ShapeDtypeStruct((nt, N_EXPERTS, 128), F32)],
        compiler_params=_cparams(("parallel",)),
        name="out_proj_router",
    )(yp.reshape(t, -1), yh.reshape(t, -1), ya.reshape(t, -1), x.reshape(t, d),
      g1.reshape(b, 1, d), sc.reshape(b, 1, d), sh.reshape(b, 1, d), nw.reshape(1, d), wo_b, rwt_b,
      rb.reshape(N_EXPERTS, 1), before, ones, lower)


def _moe_tables(cnt_out):
    cnt = jnp.round(cnt_out[:, :, 0]).astype(jnp.int32)
    cnt = (cnt + ROW_ALIGN - 1) // ROW_ALIGN * ROW_ALIGN
    total = jnp.sum(cnt, axis=0)
    padded = (total + EXPERT_TILE - 1) // EXPERT_TILE * EXPERT_TILE
    ends = jnp.cumsum(padded)
    first = ends - padded
    start = first[None, :] + jnp.cumsum(cnt, axis=0) - cnt
    loff = jnp.cumsum(cnt, axis=1) - cnt
    n_tiles = _moe_rows(cnt.shape[0] * MOE_TOK_TILE) // EXPERT_TILE
    n_used = ends[-1] // EXPERT_TILE
    tile_row = jnp.minimum(jnp.arange(n_tiles), n_used - 1) * EXPERT_TILE
    tile_expert = jnp.sum((ends[None, :] <= tile_row[:, None]).astype(jnp.int32), axis=1)
    pieces = jnp.sum(cnt, axis=1) // ROW_ALIGN
    tail_cnt = padded - total
    free_tiles = n_tiles - n_used
    tail_pieces = jnp.sum(tail_cnt) // ROW_ALIGN + free_tiles * (EXPERT_TILE // ROW_ALIGN)
    tail = jnp.concatenate([first + total, tail_cnt, jnp.stack([ends[-1], free_tiles, tail_pieces])])
    return ((start.reshape(-1), cnt.reshape(-1), loff.reshape(-1), pieces.astype(jnp.int32)),
            tail.astype(jnp.int32), tile_expert.astype(jnp.int32), n_used.reshape(1).astype(jnp.int32))


def _moe_rows(n_tokens):
    per_tile = 2 * MOE_TOK_TILE + N_EXPERTS * (ROW_ALIGN - 1)
    rows = (n_tokens // MOE_TOK_TILE) * per_tile + N_EXPERTS * (EXPERT_TILE - ROW_ALIGN)
    return (rows + EXPERT_TILE - 1) // EXPERT_TILE * EXPERT_TILE


def _chunk_copies(count, make_copy):
    for size in CHUNK_SIZES:
        offset = jnp.bitwise_and(count, ~(2 * size - 1))

        @pl.when(jnp.bitwise_and(count, size) != 0)
        def _():
            make_copy(pl.multiple_of(offset, ROW_ALIGN), size).start()


def _wait_pieces(n_pieces, piece_copy):
    def body(j, carry):
        piece_copy.wait()
        return carry

    lax.fori_loop(0, n_pieces, body, 0)


def _sort_kernel(start_ref, cnt_ref, loff_ref, pieces_ref, tail_ref, h2_ref, ls_ref, xs_ref, buf_ref, sem_ref):
    i = pl.program_id(0)
    n = pl.num_programs(0)
    tm = h2_ref.shape[0]
    slot = i % 2

    def wait_tile(tile, s):
        _wait_pieces(pieces_ref[tile], pltpu.make_async_copy(
            buf_ref.at[s, pl.ds(0, ROW_ALIGN), :], xs_ref.at[pl.ds(0, ROW_ALIGN), :], sem_ref.at[s]))

    @pl.when(i >= 2)
    def _():
        wait_tile(i - 2, slot)

    srow = lax.broadcasted_iota(jnp.int32, (MOE_SLOTS, tm), 0).astype(F32)
    ls = ls_ref[...]
    perm = jnp.where(jnp.logical_or(srow == ls[0:1], srow == ls[1:2]), 1.0, 0.0).astype(BF16)
    buf_ref[slot] = _dot(perm, h2_ref[...])

    def per_expert(e, carry):
        k = i * N_EXPERTS + e
        lo = pl.multiple_of(loff_ref[k], ROW_ALIGN)
        st = pl.multiple_of(start_ref[k], ROW_ALIGN)
        _chunk_copies(cnt_ref[k], lambda off, size: pltpu.make_async_copy(
            buf_ref.at[slot, pl.ds(lo + off, size), :], xs_ref.at[pl.ds(st + off, size), :], sem_ref.at[slot]))
        return carry

    lax.fori_loop(0, N_EXPERTS, per_expert, 0)

    @pl.when(i == n - 1)
    def _():
        @pl.when(n > 1)
        def _():
            wait_tile(i - 1, 1 - slot)
        wait_tile(i, slot)

        buf_ref[0] = jnp.zeros(buf_ref.shape[1:], F32)
        zeros_to = lambda row, size: pltpu.make_async_copy(
            buf_ref.at[0, pl.ds(0, size), :], xs_ref.at[pl.ds(row, size), :], sem_ref.at[0])

        def expert_tail(e, carry):
            st = pl.multiple_of(tail_ref[e], ROW_ALIGN)
            _chunk_copies(tail_ref[N_EXPERTS + e], lambda off, size: zeros_to(st + off, size))
            return carry

        lax.fori_loop(0, N_EXPERTS, expert_tail, 0)

        def free_tile(j, carry):
            zeros_to(pl.multiple_of(tail_ref[2 * N_EXPERTS] + j * EXPERT_TILE, ROW_ALIGN), EXPERT_TILE).start()
            return carry

        lax.fori_loop(0, tail_ref[2 * N_EXPERTS + 1], free_tile, 0)
        _wait_pieces(tail_ref[2 * N_EXPERTS + 2], zeros_to(0, ROW_ALIGN))


def _sort_tokens(h2, ls, tables, tail, n_rows):
    t, d = h2.shape
    tm = MOE_TOK_TILE
    grid_spec = pltpu.PrefetchScalarGridSpec(
        num_scalar_prefetch=5,
        grid=(t // tm,),
        in_specs=[pl.BlockSpec((tm, d), lambda i, *_: (i, 0)),
                  pl.BlockSpec((2, tm), lambda i, *_: (0, i))],
        out_specs=pl.BlockSpec(memory_space=pl.ANY),
        scratch_shapes=[pltpu.VMEM((2, MOE_SLOTS, d), F32), pltpu.SemaphoreType.DMA((2,))],
    )
    return pl.pallas_call(
        _sort_kernel,
        grid_spec=grid_spec,
        out_shape=jax.ShapeDtypeStruct((n_rows, d), F32),
        compiler_params=_cparams(("arbitrary",)),
        name="moe_sort",
    )(*tables, tail, h2, ls)


def _gmm_kernel(te_ref, nu_ref, xs_ref, wg_ref, wu_ref, wd_ref, ys_ref, wgb, wub, wdb):
    i = pl.program_id(0)
    used = i < nu_ref[0]
    new_expert = jnp.logical_or(i == 0, te_ref[i] != te_ref[jnp.maximum(i - 1, 0)])

    @pl.when(jnp.logical_and(used, new_expert))
    def _():
        wgb[...] = wg_ref[0].astype(BF16)
        wub[...] = wu_ref[0].astype(BF16)
        wdb[...] = wd_ref[0].astype(BF16)

    @pl.when(used)
    def _():
        x = xs_ref[...].astype(BF16)
        he = _silu(_dot(x, wgb[...])) * _dot(x, wub[...])
        ys_ref[...] = _dot(he.astype(BF16), wdb[...])

    @pl.when(jnp.logical_not(used))
    def _():
        ys_ref[...] = jnp.zeros(ys_ref.shape, F32)


def _grouped_mlp(xs, tile_expert, n_used, wg, wu, wd):
    n_rows, d = xs.shape
    te = EXPERT_TILE
    row_map = lambda i, tex, nu: (jnp.minimum(i, nu[0] - 1), 0)
    grid_spec = pltpu.PrefetchScalarGridSpec(
        num_scalar_prefetch=2,
        grid=(n_rows // te,),
        in_specs=[pl.BlockSpec((te, d), row_map),
                  pl.BlockSpec((1, d, D_EXPERT), lambda i, tex, nu: (tex[i], 0, 0)),
                  pl.BlockSpec((1, d, D_EXPERT), lambda i, tex, nu: (tex[i], 0, 0)),
                  pl.BlockSpec((1, D_EXPERT, d), lambda i, tex, nu: (tex[i], 0, 0))],
        out_specs=pl.BlockSpec((te, d), lambda i, tex, nu: (i, 0)),
        scratch_shapes=[pltpu.VMEM((d, D_EXPERT), BF16), pltpu.VMEM((d, D_EXPERT), BF16),
                        pltpu.VMEM((D_EXPERT, d), BF16)],
    )
    return pl.pallas_call(
        _gmm_kernel,
        grid_spec=grid_spec,
        out_shape=jax.ShapeDtypeStruct((n_rows, d), F32),
        compiler_params=_cparams(("arbitrary",)),
        name="moe_grouped_mlp",
    )(tile_expert, n_used, xs, wg, wu, wd)


def _combine_kernel(start_ref, cnt_ref, loff_ref, pieces_ref, ls_ref, gate_ref, x1_ref, g2_ref, ys_ref, o_ref,
                    buf_ref, sem_ref):
    i = pl.program_id(0)
    n = pl.num_programs(0)
    tm = x1_ref.shape[0]
    slot = i % 2

    def fetch(tile, s):
        def per_expert(e, carry):
            k = tile * N_EXPERTS + e
            lo = pl.multiple_of(loff_ref[k], ROW_ALIGN)
            st = pl.multiple_of(start_ref[k], ROW_ALIGN)
            _chunk_copies(cnt_ref[k], lambda off, size: pltpu.make_async_copy(
                ys_ref.at[pl.ds(st + off, size), :], buf_ref.at[s, pl.ds(lo + off, size), :], sem_ref.at[s]))
            return carry

        lax.fori_loop(0, N_EXPERTS, per_expert, 0)

    @pl.when(i == 0)
    def _():
        fetch(0, 0)

    @pl.when(i + 1 < n)
    def _():
        fetch(i + 1, 1 - slot)

    _wait_pieces(pieces_ref[i], pltpu.make_async_copy(
        ys_ref.at[pl.ds(0, ROW_ALIGN), :], buf_ref.at[slot, pl.ds(0, ROW_ALIGN), :], sem_ref.at[slot]))

    srow = lax.broadcasted_iota(jnp.int32, (MOE_SLOTS, tm), 0).astype(F32)
    ls = ls_ref[...]
    gate = gate_ref[...]
    weights = (jnp.where(srow == ls[0:1], gate[0:1], 0.0) + jnp.where(srow == ls[1:2], gate[1:2], 0.0)).astype(BF16)
    rows = jnp.where(lax.broadcasted_iota(jnp.int32, (MOE_SLOTS, 1), 0) < pieces_ref[i] * ROW_ALIGN,
                     buf_ref[slot], 0.0).astype(BF16)
    y = _dot_tn(weights, rows)
    o_ref[...] = x1_ref[...] + g2_ref[0] * y


def _combine(ys, ls, gate, x1, g2, tables, per_batch):
    t, d = x1.shape
    tm = MOE_TOK_TILE
    grid_spec = pltpu.PrefetchScalarGridSpec(
        num_scalar_prefetch=4,
        grid=(t // tm,),
        in_specs=[pl.BlockSpec((2, tm), lambda i, *_: (0, i)),
                  pl.BlockSpec((2, tm), lambda i, *_: (0, i)),
                  pl.BlockSpec((tm, d), lambda i, *_: (i, 0)),
                  pl.BlockSpec((1, 1, d), lambda i, *_: (i // per_batch, 0, 0)),
                  pl.BlockSpec(memory_space=pl.ANY)],
        out_specs=pl.BlockSpec((tm, d), lambda i, *_: (i, 0)),
        scratch_shapes=[pltpu.VMEM((2, MOE_SLOTS, d), F32), pltpu.SemaphoreType.DMA((2,))],
    )
    return pl.pallas_call(
        _combine_kernel,
        grid_spec=grid_spec,
        out_shape=jax.ShapeDtypeStruct((t, d), F32),
        compiler_params=_cparams(("arbitrary",)),
        name="moe_combine",
    )(*tables, ls, gate, x1, g2.reshape(-1, 1, d), ys)


def kernel(x, c, ada_w, ada_b, norm1_w, norm2_w, w_in, pool_w, pool_scale, hgrn_lb_raw, hgrn_norm_w, q_norm_w,
           k_norm_w, attn_sinks, w_out, router_w, router_bias, expert_w_gate, expert_w_up, expert_w_down):
    b, s, d = x.shape
    depth = ada_w.shape[0]
    t = b * s
    per_batch = s // MOE_TOK_TILE
    n_rows = _moe_rows(t)

    p = jax.nn.softmax(hgrn_lb_raw.astype(F32), axis=0)
    lower_bounds = jnp.maximum(jnp.cumsum(p, axis=0) - p[0:1], 0.0)

    mod = _modulation(c, ada_w, ada_b)
    hgrn_consts = _hgrn_constants() + (_head_ones(HGRN_WIDTH, HGRN_KDIM),)
    attn_consts = (_attn_bias(), _head_ones(ATT_WIDTH, HEAD_DIM), _head_ones(KV_WIDTH, HEAD_DIM))
    router_consts = _router_constants(MOE_TOK_TILE)
    rwt_b = router_w.T.astype(BF16)

    for l in range(depth):
        sh1, sc1, g1, sh2, sc2, g2 = [mod[l, :, j * d:(j + 1) * d] for j in range(6)]
        za, zh, zq, zkv = _in_proj(x, sc1, sh1, norm1_w[l], w_in[l].astype(BF16))
        yp = _pool_mixer(za, _block_diag(pool_w[l]).astype(BF16), pool_scale[l])
        yh = _hgrn_mixer(zh, lower_bounds[l], hgrn_norm_w[l], hgrn_consts)
        ya = _attn_mixer(zq, zkv, q_norm_w[l], k_norm_w[l], attn_sinks[l], attn_consts)
        x1, h2, ls, gate, cnt_out = _out_proj_router(
            yp, yh, ya, x, g1, sc2, sh2, norm2_w[l], w_out[l].astype(BF16), rwt_b, router_bias, router_consts)
        tables, tail, tile_expert, n_used = _moe_tables(cnt_out)
        xs = _sort_tokens(h2, ls, tables, tail, n_rows)
        ys = _grouped_mlp(xs, tile_expert, n_used, expert_w_gate[l], expert_w_up[l], expert_w_down[l])
        x = _combine(ys, ls, gate, x1, g2, tables, per_batch).reshape(b, s, d)
    return x
```

```python
import functools

import numpy as np
import jax
import jax.numpy as jnp
from jax import lax
from jax.experimental import pallas as pl
from jax.experimental.pallas import tpu as pltpu

F32 = jnp.float32
BF16 = jnp.bfloat16

D_MODEL = 1024
POOL_WINDOWS = (2, 4, 8, 16)
POOL_WIDTH = 256
POOL_GROUP = 64
POOL_HALO = 16
HGRN_HEADS = 4
HGRN_KDIM = 64
HGRN_WIDTH = 256
HGRN_CHUNK = 64
ATT_HEADS = 8
ATT_KV_HEADS = 2
HEAD_DIM = 64
ATT_WIDTH = 512
KV_WIDTH = 128
WINDOW = 128
N_EXPERTS = 16
N_GROUPS = 4
EXPERTS_PER_GROUP = 4
D_EXPERT = 512
EPS = 1e-6
MAX_ONE_MINUS_F = 1.0 - 1e-6
NEG = -1e30

VMEM_LIMIT = 48 * 1024 * 1024

SEQ_TILE = 512
MOE_TOK_TILE = 256
ROW_ALIGN = 8
MOE_SLOTS = 640
EXPERT_TILE = 512
CHUNK_SIZES = (256, 128, 64, 32, 16, 8)


def _sigmoid(x):
    return 1.0 / (1.0 + jnp.exp(-x))


def _silu(x):
    return x * _sigmoid(x)


def _cparams(sem):
    return pltpu.CompilerParams(dimension_semantics=sem, vmem_limit_bytes=VMEM_LIMIT)


def _dot(a, b):
    return jnp.dot(a, b, preferred_element_type=F32)


def _dot_nt(a, b):
    return lax.dot_general(a, b, (((1,), (1,)), ((), ())), preferred_element_type=F32)


def _dot_tn(a, b):
    return lax.dot_general(a, b, (((0,), (0,)), ((), ())), preferred_element_type=F32)


def _split2(x):
    hi = x.astype(BF16)
    lo = (x - hi.astype(F32)).astype(BF16)
    return hi, lo


def _head_sumsq(x, bd):
    hi, lo = _split2(x * x)
    return _dot(hi, bd) + _dot(lo, bd)


def _mod_kernel(c_ref, w_ref, b_ref, o_ref):
    cond = _silu(c_ref[...])
    o_ref[0] = _dot(cond.astype(BF16), w_ref[0].astype(BF16)) + b_ref[0]


def _modulation(c, ada_w, ada_b):
    depth, d, n = ada_w.shape
    b = c.shape[0]
    nb = n // d
    return pl.pallas_call(
        _mod_kernel,
        grid=(depth, nb),
        in_specs=[pl.BlockSpec((b, d), lambda l, j: (0, 0)),
                  pl.BlockSpec((1, d, d), lambda l, j: (l, 0, j)),
                  pl.BlockSpec((1, 1, d), lambda l, j: (l, 0, j))],
        out_specs=pl.BlockSpec((1, b, d), lambda l, j: (l, 0, j)),
        out_shape=jax.ShapeDtypeStruct((depth, b, n), F32),
        compiler_params=_cparams(("parallel", "parallel")),
        name="adaln_mod",
    )(c, ada_w, ada_b.reshape(depth, 1, n))


def _in_kernel(x_ref, sc_ref, sh_ref, nw_ref, w_ref, za_ref, zh_ref, zq_ref, zkv_ref):
    x = x_ref[0]
    ms = jnp.mean(x * x, axis=-1, keepdims=True)
    h = (x * lax.rsqrt(ms + EPS) * nw_ref[...]) * (1.0 + sc_ref[0]) + sh_ref[0]
    z = _dot(h.astype(BF16), w_ref[...])
    za_ref[0] = z[:, :POOL_WIDTH]
    zh_ref[0] = z[:, POOL_WIDTH:POOL_WIDTH + 4 * HGRN_WIDTH]
    zq_ref[0] = z[:, POOL_WIDTH + 4 * HGRN_WIDTH:POOL_WIDTH + 4 * HGRN_WIDTH + ATT_WIDTH]
    zkv_ref[0] = z[:, POOL_WIDTH + 4 * HGRN_WIDTH + ATT_WIDTH:]


def _in_proj(x, sc, sh, nw, w_in_b, layer):
    b, s, d = x.shape
    n = w_in_b.shape[2]
    ts = min(SEQ_TILE, s)
    tok = lambda w: pl.BlockSpec((1, ts, w), lambda bi, i: (bi, i, 0))
    vec = pl.BlockSpec((1, 1, d), lambda bi, i: (bi, 0, 0))
    widths = (POOL_WIDTH, 4 * HGRN_WIDTH, ATT_WIDTH, 2 * KV_WIDTH)
    return pl.pallas_call(
        _in_kernel,
        grid=(b, s // ts),
        in_specs=[tok(d), vec, vec,
                  pl.BlockSpec((1, d), lambda bi, i: (0, 0)),
                  pl.BlockSpec((None, d, n), lambda bi, i: (layer, 0, 0))],
        out_specs=[tok(w) for w in widths],
        out_shape=[jax.ShapeDtypeStruct((b, s, w), F32) for w in widths],
        compiler_params=_cparams(("parallel", "parallel")),
        name="in_proj",
    )(x, sc.reshape(b, 1, d), sh.reshape(b, 1, d), nw.reshape(1, d), w_in_b)


def _pool_kernel(a_ref, w_ref, scale_ref, o_ref, buf_ref):
    i = pl.program_id(1)
    ts = a_ref.shape[1]
    a = a_ref[0]

    @pl.when(i == 0)
    def _():
        buf_ref[0:POOL_HALO, :] = jnp.zeros((POOL_HALO, POOL_WIDTH), F32)

    buf_ref[POOL_HALO:POOL_HALO + ts, :] = a
    lane = lax.broadcasted_iota(jnp.int32, (1, POOL_WIDTH), 1)
    win = jnp.left_shift(2, lane // POOL_GROUP)
    acc = a
    for j in range(1, POOL_HALO):
        shifted = buf_ref[POOL_HALO - j:POOL_HALO - j + ts, :]
        acc = acc + jnp.where(win > j, shifted, 0.0)
    pos = i * ts + lax.broadcasted_iota(jnp.int32, (ts, 1), 0)
    count = jnp.minimum(pos + 1, win).astype(F32)
    pooled = acc / count - a
    y = _dot(pooled.astype(BF16), w_ref[...]) * scale_ref[...]
    o_ref[0] = y.astype(o_ref.dtype)
    buf_ref[0:POOL_HALO, :] = a[ts - POOL_HALO:, :]


def _pool_mixer(za, pool_bd, pool_scale):
    b, s, w = za.shape
    ts = min(SEQ_TILE, s)
    return pl.pallas_call(
        _pool_kernel,
        grid=(b, s // ts),
        in_specs=[pl.BlockSpec((1, ts, w), lambda bi, i: (bi, i, 0)),
                  pl.BlockSpec((w, w), lambda bi, i: (0, 0)),
                  pl.BlockSpec((1, w), lambda bi, i: (0, 0))],
        out_specs=pl.BlockSpec((1, ts, w), lambda bi, i: (bi, i, 0)),
        out_shape=jax.ShapeDtypeStruct((b, s, w), BF16),
        scratch_shapes=[pltpu.VMEM((POOL_HALO + ts, w), F32)],
        compiler_params=_cparams(("parallel", "arbitrary")),
        name="pool_mixer",
    )(za, pool_bd, pool_scale.reshape(1, w))


def _block_diag(blocks):
    g, n, _ = blocks.shape
    eye = jnp.eye(g, dtype=blocks.dtype)
    return (eye[:, None, :, None] * blocks[:, :, None, :]).reshape(g * n, g * n)


def _head_ones(width, head):
    idx = np.arange(width) // head
    return jnp.asarray((idx[:, None] == idx[None, :]).astype(np.float32), dtype=BF16)


HGRN_LEVELS = (32, 16, 8, 4, 2, 1)
N_LEVEL_MASKS = len(HGRN_LEVELS) + 1
ROW_B = len(HGRN_LEVELS) * HGRN_CHUNK
ROW_R = ROW_B + HGRN_CHUNK


def _hgrn_constants():
    c = HGRN_CHUNK
    m = np.zeros((len(HGRN_LEVELS) + 2, c, c), np.float32)
    masks = np.zeros((N_LEVEL_MASKS, c, c), np.float32)
    for lvl, n in enumerate(HGRN_LEVELS):
        for t in range(c):
            blk = t // (2 * n)
            mid = blk * 2 * n + n
            if t >= mid:
                m[lvl, t, mid:t + 1] = 1.0
                masks[lvl, t, blk * 2 * n:mid] = 1.0
            else:
                m[lvl, t, t + 1:mid] = 1.0
    for t in range(c):
        m[-2, t, :t + 1] = 1.0
        m[-1, t, t + 1:] = 1.0
        masks[-1, t, t] = 1.0
    m = m.reshape(-1, c)
    m3 = np.concatenate([m, m, m], axis=1)
    masks = np.tile(masks, (1, HGRN_HEADS, 1))
    return jnp.asarray(m3, dtype=BF16), jnp.asarray(masks, dtype=F32)


def _hgrn_kernel(zh_ref, lb_ref, gw_ref, m3_ref, lmask_ref, bd_ref, o_ref, st_ref):
    i = pl.program_id(1)
    ts = zh_ref.shape[1]
    c = HGRN_CHUNK
    w = HGRN_WIDTH

    @pl.when(i == 0)
    def _():
        st_ref[...] = jnp.zeros((w, w), F32)

    lane_head = lax.broadcasted_iota(jnp.int32, (1, w), 1) // HGRN_KDIM
    one_minus_lb = 1.0 - lb_ref[...]
    gw = gw_ref[...]
    bd = bd_ref[...]
    on_diag_block = bd > 0

    def chunk(ci, carry):
        r0 = pl.multiple_of(ci * c, c)
        q = zh_ref[0, pl.ds(r0, c), 0:w]
        f = zh_ref[0, pl.ds(r0, c), w:2 * w]
        v = zh_ref[0, pl.ds(r0, c), 2 * w:3 * w].astype(BF16)
        g = zh_ref[0, pl.ds(r0, c), 3 * w:4 * w]
        kk = one_minus_lb * _sigmoid(-f)
        lf = jnp.log(1.0 - jnp.minimum(kk, MAX_ONE_MINUS_F))
        qf = _silu(q)
        hi = lf.astype(BF16)
        r1 = lf - hi.astype(F32)
        mid = r1.astype(BF16)
        lo = (r1 - mid.astype(F32)).astype(BF16)
        sums = _dot(m3_ref[...], jnp.concatenate([hi, mid, lo], axis=0))
        dec = jnp.exp(sums)

        scores = jnp.zeros((HGRN_HEADS * c, c), F32)
        for lvl in range(N_LEVEL_MASKS):
            if lvl < len(HGRN_LEVELS):
                e = dec[lvl * c:(lvl + 1) * c]
                ql = (qf * e).astype(BF16)
                kl = (kk * e).astype(BF16)
            else:
                ql = qf.astype(BF16)
                kl = kk.astype(BF16)
            zero = jnp.zeros_like(ql)
            qs = jnp.concatenate([jnp.where(lane_head == h, ql, zero) for h in range(HGRN_HEADS)], axis=0)
            scores = scores + _dot_nt(qs, kl) * lmask_ref[lvl]
        r = _dot(scores.astype(BF16), v)
        o = jnp.zeros((c, w), F32)
        for h in range(HGRN_HEADS):
            o = o + jnp.where(lane_head == h, r[h * c:(h + 1) * c], 0.0)
        eb = dec[ROW_B:ROW_B + c]
        er = dec[ROW_R:ROW_R + c]
        st = st_ref[...]
        o = o + _dot_nt((qf * eb).astype(BF16), st.astype(BF16))
        upd = _dot_tn(v, (kk * er).astype(BF16))
        st_ref[...] = st * eb[c - 1:c, :] + jnp.where(on_diag_block, upd, 0.0)
        ss = _head_sumsq(o, bd)
        y = o * lax.rsqrt(ss * (1.0 / HGRN_KDIM) + EPS) * gw * _silu(g)
        o_ref[0, pl.ds(r0, c), :] = y.astype(o_ref.dtype)
        return carry

    lax.fori_loop(0, ts // c, chunk, 0)


def _hgrn_mixer(zh, lb, norm_w, consts):
    b, s, _ = zh.shape
    w = HGRN_WIDTH
    ts = min(SEQ_TILE, s)
    m3, lmask, bd = consts
    full = lambda a: pl.BlockSpec(a.shape, lambda bi, i: (0,) * a.ndim)
    return pl.pallas_call(
        _hgrn_kernel,
        grid=(b, s // ts),
        in_specs=[pl.BlockSpec((1, ts, 4 * w), lambda bi, i: (bi, i, 0)),
                  pl.BlockSpec((1, w), lambda bi, i: (0, 0)),
                  pl.BlockSpec((1, w), lambda bi, i: (0, 0)),
                  full(m3), full(lmask), full(bd)],
        out_specs=pl.BlockSpec((1, ts, w), lambda bi, i: (bi, i, 0)),
        out_shape=jax.ShapeDtypeStruct((b, s, w), BF16),
        scratch_shapes=[pltpu.VMEM((w, w), F32)],
        compiler_params=_cparams(("parallel", "arbitrary")),
        name="hgrn_mixer",
    )(zh, lb.reshape(1, w), norm_w.reshape(1, w), m3, lmask, bd)


def _attn_bias():
    qi = np.arange(WINDOW)[:, None]
    kj = np.arange(2 * WINDOW)[None, :]
    dist = qi + WINDOW - kj
    valid = (dist >= 0) & (dist < WINDOW)
    slopes = np.exp2(-8.0 * np.arange(1, ATT_HEADS + 1) / ATT_HEADS)
    bias = np.where(valid[None], -slopes[:, None, None] * dist[None], NEG)
    return jnp.asarray(bias, dtype=F32)


def _attn_kernel(sink_ref, zq_ref, zkv_ref, qw_ref, kw_ref, bias_ref, bdq_ref, bdk_ref, o_ref,
                 qbuf, kbuf, vbuf):
    i = pl.program_id(1)
    ts = zq_ref.shape[1]
    hw = 2 * HEAD_DIM

    @pl.when(i == 0)
    def _():
        kbuf[:, 0:WINDOW, :] = jnp.zeros((4, WINDOW, hw), BF16)
        vbuf[:, 0:WINDOW, :] = jnp.zeros((4, WINDOW, hw), BF16)

    q = zq_ref[0]
    ssq = _head_sumsq(q, bdq_ref[...])
    qbuf[...] = (q * lax.rsqrt(ssq * (1.0 / HEAD_DIM) + EPS) * (qw_ref[...] * HEAD_DIM ** -0.5)).astype(BF16)
    kv = zkv_ref[0]
    k = kv[:, :KV_WIDTH]
    v = kv[:, KV_WIDTH:]
    ssk = _head_sumsq(k, bdk_ref[...])
    kn = k * lax.rsqrt(ssk * (1.0 / HEAD_DIM) + EPS) * kw_ref[...]
    kr = pltpu.roll(kn, HEAD_DIM, 1)
    vr = pltpu.roll(v, HEAD_DIM, 1)
    low = lax.broadcasted_iota(jnp.int32, (1, hw), 1) < HEAD_DIM
    for j in range(ATT_KV_HEADS):
        for half in range(2):
            keep = low if half == 0 else jnp.logical_not(low)
            ksrc = kn if j == half else kr
            vsrc = v if j == half else vr
            kbuf[2 * j + half, WINDOW:WINDOW + ts, :] = jnp.where(keep, ksrc, 0.0).astype(BF16)
            vbuf[2 * j + half, WINDOW:WINDOW + ts, :] = jnp.where(keep, vsrc, 0.0).astype(BF16)

    prev_cols = lax.broadcasted_iota(jnp.int32, (1, 2 * WINDOW), 1) < WINDOW

    def block(n, carry):
        r0 = pl.multiple_of(n * WINDOW, WINDOW)
        no_prev = jnp.logical_and(i == 0, n == 0)
        pen = jnp.where(jnp.logical_and(prev_cols, no_prev), NEG, 0.0)
        for hp in range(ATT_HEADS // 2):
            qp = qbuf[pl.ds(r0, WINDOW), hp * hw:(hp + 1) * hw]
            acc = jnp.zeros((WINDOW, hw), F32)
            for half in range(2):
                h = 2 * hp + half
                j = h // (ATT_HEADS // ATT_KV_HEADS)
                keys = kbuf[2 * j + half, pl.ds(r0, 2 * WINDOW), :]
                vals = vbuf[2 * j + half, pl.ds(r0, 2 * WINDOW), :]
                logits = _dot_nt(qp, keys) + bias_ref[h] + pen
                sink = sink_ref[h]
                m = jnp.maximum(jnp.max(logits, axis=-1, keepdims=True), sink)
                p = jnp.exp(logits - m)
                denom = jnp.sum(p, axis=-1, keepdims=True) + jnp.exp(sink - m)
                acc = acc + _dot(p.astype(BF16), vals) * (1.0 / denom)
            o_ref[0, pl.ds(r0, WINDOW), hp * hw:(hp + 1) * hw] = acc.astype(o_ref.dtype)
        return carry

    lax.fori_loop(0, ts // WINDOW, block, 0)
    kbuf[:, 0:WINDOW, :] = kbuf[:, ts:ts + WINDOW, :]
    vbuf[:, 0:WINDOW, :] = vbuf[:, ts:ts + WINDOW, :]


def _attn_mixer(zq, zkv, q_norm_w, k_norm_w, sinks, consts):
    b, s, _ = zq.shape
    ts = min(SEQ_TILE, s)
    bias, bdq, bdk = consts
    hw = 2 * HEAD_DIM
    qw = jnp.tile(q_norm_w, ATT_HEADS).reshape(1, ATT_WIDTH)
    kw = jnp.tile(k_norm_w, ATT_KV_HEADS).reshape(1, KV_WIDTH)
    full = lambda a: pl.BlockSpec(a.shape, lambda bi, i, sk: (0,) * a.ndim)
    grid_spec = pltpu.PrefetchScalarGridSpec(
        num_scalar_prefetch=1,
        grid=(b, s // ts),
        in_specs=[pl.BlockSpec((1, ts, ATT_WIDTH), lambda bi, i, sk: (bi, i, 0)),
                  pl.BlockSpec((1, ts, 2 * KV_WIDTH), lambda bi, i, sk: (bi, i, 0)),
                  full(qw), full(kw), full(bias), full(bdq), full(bdk)],
        out_specs=pl.BlockSpec((1, ts, ATT_WIDTH), lambda bi, i, sk: (bi, i, 0)),
        scratch_shapes=[pltpu.VMEM((ts, ATT_WIDTH), BF16),
                        pltpu.VMEM((4, WINDOW + ts, hw), BF16),
                        pltpu.VMEM((4, WINDOW + ts, hw), BF16)],
    )
    return pl.pallas_call(
        _attn_kernel,
        grid_spec=grid_spec,
        out_shape=jax.ShapeDtypeStruct((b, s, ATT_WIDTH), BF16),
        compiler_params=_cparams(("parallel", "arbitrary")),
        name="swa_mixer",
    )(sinks, zq, zkv, qw, kw, bias, bdq, bdk)


def _router_constants(tm):
    t = np.arange(tm)
    before = (t[:, None] < t[None, :]).astype(np.float32)
    e = np.arange(N_EXPERTS)
    lower = (e[None, :] < e[:, None]).astype(np.float32)
    return (jnp.asarray(before, dtype=BF16), jnp.asarray(np.ones((tm, tm), np.float32), dtype=BF16),
            jnp.asarray(lower, dtype=BF16))


def _out_kernel(yp_ref, yh_ref, ya_ref, x_ref, g1_ref, sc_ref, sh_ref, nw_ref, wo_ref, rwt_ref, rb_ref,
                before_ref, ones_ref, lower_ref, x1_ref, h2_ref, ls_ref, gate_ref, cnt_ref):
    tm = x_ref.shape[0]
    p0, p1 = POOL_WIDTH, POOL_WIDTH + HGRN_WIDTH
    mix = (_dot(yp_ref[...], wo_ref[0:p0, :]) + _dot(yh_ref[...], wo_ref[p0:p1, :])
           + _dot(ya_ref[...], wo_ref[p1:, :]))
    x1 = x_ref[...] + g1_ref[0] * mix
    x1_ref[...] = x1
    ms = jnp.mean(x1 * x1, axis=-1, keepdims=True)
    h2 = ((x1 * lax.rsqrt(ms + EPS) * nw_ref[...]) * (1.0 + sc_ref[0]) + sh_ref[0]).astype(BF16)
    h2_ref[...] = h2

    logits = _dot_nt(rwt_ref[...], h2)
    ex = jnp.exp(logits - jnp.max(logits, axis=0, keepdims=True))
    scores = ex / jnp.sum(ex, axis=0, keepdims=True)
    sel = scores + rb_ref[...]
    group_scores = []
    for g in range(N_GROUPS):
        rows = [sel[g * EXPERTS_PER_GROUP + a:g * EXPERTS_PER_GROUP + a + 1] for a in range(EXPERTS_PER_GROUP)]
        best_pair = None
        for a in range(EXPERTS_PER_GROUP):
            for bb in range(a + 1, EXPERTS_PER_GROUP):
                pair = rows[a] + rows[bb]
                best_pair = pair if best_pair is None else jnp.maximum(best_pair, pair)
        group_scores.append(best_pair)
    top = functools.reduce(jnp.maximum, group_scores)
    best = jnp.full((1, tm), N_GROUPS - 1, jnp.int32)
    for g in reversed(range(N_GROUPS - 1)):
        best = jnp.where(group_scores[g] == top, g, best)
    row = lax.broadcasted_iota(jnp.int32, (N_EXPERTS, tm), 0)
    cand = jnp.where(row // EXPERTS_PER_GROUP == best, sel, NEG)
    m1 = jnp.max(cand, axis=0, keepdims=True)
    i1 = jnp.min(jnp.where(cand == m1, row, N_EXPERTS), axis=0, keepdims=True)
    oh1 = row == i1
    cand = jnp.where(oh1, NEG, cand)
    m2 = jnp.max(cand, axis=0, keepdims=True)
    i2 = jnp.min(jnp.where(cand == m2, row, N_EXPERTS), axis=0, keepdims=True)
    oh2 = row == i2
    s1 = jnp.sum(jnp.where(oh1, scores, 0.0), axis=0, keepdims=True)
    s2 = jnp.sum(jnp.where(oh2, scores, 0.0), axis=0, keepdims=True)
    gate_ref[...] = jnp.concatenate([s1 / (s1 + s2), s2 / (s1 + s2)], axis=0)

    chosen = jnp.where(jnp.logical_or(oh1, oh2), 1.0, 0.0).astype(BF16)
    rank = _dot(chosen, before_ref[...])
    count = _dot(chosen, ones_ref[...])
    aligned = jnp.floor((count + (ROW_ALIGN - 1)) * (1.0 / ROW_ALIGN)) * ROW_ALIGN
    slot = _dot(lower_ref[...], aligned.astype(BF16)) + rank
    ls_ref[...] = jnp.concatenate([jnp.sum(jnp.where(oh1, slot, 0.0), axis=0, keepdims=True),
                                   jnp.sum(jnp.where(oh2, slot, 0.0), axis=0, keepdims=True)], axis=0)
    cnt_ref[0] = count[:, 0:128]


def _out_proj_router(yp, yh, ya, x, g1, sc, sh, nw, wo_b, layer, rwt_b, rb, consts):
    b, s, d = x.shape
    t = b * s
    tm = MOE_TOK_TILE
    per_batch = s // tm
    nt = t // tm
    before, ones, lower = consts
    tok = lambda w: pl.BlockSpec((tm, w), lambda i: (i, 0))
    vec = pl.BlockSpec((1, 1, d), lambda i: (i // per_batch, 0, 0))
    full = lambda a: pl.BlockSpec(a.shape, lambda i: (0,) * a.ndim)
    lanes = pl.BlockSpec((2, tm), lambda i: (0, i))
    return pl.pallas_call(
        _out_kernel,
        grid=(nt,),
        in_specs=[tok(POOL_WIDTH), tok(HGRN_WIDTH), tok(ATT_WIDTH), tok(d), vec, vec, vec,
                  pl.BlockSpec((1, d), lambda i: (0, 0)),
                  pl.BlockSpec((None,) + wo_b.shape[1:], lambda i: (layer, 0, 0)), full(rwt_b),
                  pl.BlockSpec((N_EXPERTS, 1), lambda i: (0, 0)), full(before), full(ones), full(lower)],
        out_specs=[tok(d), tok(d), lanes, lanes, pl.BlockSpec((1, N_EXPERTS, 128), lambda i: (i, 0, 0))],
        out_shape=[jax.ShapeDtypeStruct((t, d), F32), jax.ShapeDtypeStruct((t, d), BF16),
                   jax.ShapeDtypeStruct((2, t), F32), jax.ShapeDtypeStruct((2, t), F32),
                   jax.ShapeDtypeStruct((nt, N_EXPERTS, 128), F32)],
        compiler_params=_cparams(("parallel",)),
        name="out_proj_router",
    )(yp.reshape(t, -1), yh.reshape(t, -1), ya.reshape(t, -1), x.reshape(t, d),
      g1.reshape(b, 1, d), sc.reshape(b, 1, d), sh.reshape(b, 1, d), nw.reshape(1, d), wo_b, rwt_b,
      rb.reshape(N_EXPERTS, 1), before, ones, lower)


def _moe_tables(cnt_out):
    cnt = jnp.round(cnt_out[:, :, 0]).astype(jnp.int32)
    cnt = (cnt + ROW_ALIGN - 1) // ROW_ALIGN * ROW_ALIGN
    total = jnp.sum(cnt, axis=0)
    padded = (total + EXPERT_TILE - 1) // EXPERT_TILE * EXPERT_TILE
    ends = jnp.cumsum(padded)
    first = ends - padded
    start = first[None, :] + jnp.cumsum(cnt, axis=0) - cnt
    loff = jnp.cumsum(cnt, axis=1) - cnt
    n_tiles = _moe_rows(cnt.shape[0] * MOE_TOK_TILE) // EXPERT_TILE
    n_used = ends[-1] // EXPERT_TILE
    tile_row = jnp.minimum(jnp.arange(n_tiles), n_used - 1) * EXPERT_TILE
    tile_expert = jnp.sum((ends[None, :] <= tile_row[:, None]).astype(jnp.int32), axis=1)
    pieces = jnp.sum(cnt, axis=1) // ROW_ALIGN
    tail_cnt = padded - total
    free_tiles = n_tiles - n_used
    tail_pieces = jnp.sum(tail_cnt) // ROW_ALIGN + free_tiles * (EXPERT_TILE // ROW_ALIGN)
    tail = jnp.concatenate([first + total, tail_cnt, jnp.stack([ends[-1], free_tiles, tail_pieces])])
    return ((start.reshape(-1), cnt.reshape(-1), loff.reshape(-1), pieces.astype(jnp.int32)),
            tail.astype(jnp.int32), tile_expert.astype(jnp.int32), n_used.reshape(1).astype(jnp.int32))


def _moe_rows(n_tokens):
    per_tile = 2 * MOE_TOK_TILE + N_EXPERTS * (ROW_ALIGN - 1)
    rows = (n_tokens // MOE_TOK_TILE) * per_tile + N_EXPERTS * (EXPERT_TILE - ROW_ALIGN)
    return (rows + EXPERT_TILE - 1) // EXPERT_TILE * EXPERT_TILE


def _pack_halves(x):
    half = x.shape[1] // 2
    bits = lax.bitcast_convert_type(x, jnp.uint32)
    return jnp.bitwise_or(jnp.bitwise_and(bits[:, half:], jnp.uint32(0xFFFF0000)),
                          jnp.right_shift(bits[:, :half], jnp.uint32(16)))


def _unpack_halves(p):
    lo = lax.bitcast_convert_type(jnp.left_shift(p, jnp.uint32(16)), F32)
    hi = lax.bitcast_convert_type(jnp.bitwise_and(p, jnp.uint32(0xFFFF0000)), F32)
    return lo.astype(BF16), hi.astype(BF16)


def _chunk_copies(count, make_copy):
    for size in CHUNK_SIZES:
        offset = jnp.bitwise_and(count, ~(2 * size - 1))

        @pl.when(jnp.bitwise_and(count, size) != 0)
        def _():
            make_copy(pl.multiple_of(offset, ROW_ALIGN), size).start()


def _wait_pieces(n_pieces, piece_copy):
    def body(j, carry):
        piece_copy.wait()
        return carry

    lax.fori_loop(0, n_pieces, body, 0)


def _sort_kernel(start_ref, cnt_ref, loff_ref, pieces_ref, tail_ref, h2_ref, ls_ref, xs_ref, buf_ref, sem_ref):
    i = pl.program_id(0)
    n = pl.num_programs(0)
    tm = h2_ref.shape[0]
    slot = i % 2

    def wait_tile(tile, s):
        _wait_pieces(pieces_ref[tile], pltpu.make_async_copy(
            buf_ref.at[s, pl.ds(0, ROW_ALIGN), :], xs_ref.at[pl.ds(0, ROW_ALIGN), :], sem_ref.at[s]))

    @pl.when(i >= 2)
    def _():
        wait_tile(i - 2, slot)

    srow = lax.broadcasted_iota(jnp.int32, (MOE_SLOTS, tm), 0).astype(F32)
    ls = ls_ref[...]
    perm = jnp.where(jnp.logical_or(srow == ls[0:1], srow == ls[1:2]), 1.0, 0.0).astype(BF16)
    buf_ref[slot] = _pack_halves(_dot(perm, h2_ref[...]))

    def per_expert(e, carry):
        k = i * N_EXPERTS + e
        lo = pl.multiple_of(loff_ref[k], ROW_ALIGN)
        st = pl.multiple_of(start_ref[k], ROW_ALIGN)
        _chunk_copies(cnt_ref[k], lambda off, size: pltpu.make_async_copy(
            buf_ref.at[slot, pl.ds(lo + off, size), :], xs_ref.at[pl.ds(st + off, size), :], sem_ref.at[slot]))
        return carry

    lax.fori_loop(0, N_EXPERTS, per_expert, 0)

    @pl.when(i == n - 1)
    def _():
        @pl.when(n > 1)
        def _():
            wait_tile(i - 1, 1 - slot)
        wait_tile(i, slot)

        buf_ref[0] = jnp.zeros(buf_ref.shape[1:], jnp.uint32)
        zeros_to = lambda row, size: pltpu.make_async_copy(
            buf_ref.at[0, pl.ds(0, size), :], xs_ref.at[pl.ds(row, size), :], sem_ref.at[0])

        def expert_tail(e, carry):
            st = pl.multiple_of(tail_ref[e], ROW_ALIGN)
            _chunk_copies(tail_ref[N_EXPERTS + e], lambda off, size: zeros_to(st + off, size))
            return carry

        lax.fori_loop(0, N_EXPERTS, expert_tail, 0)

        def free_tile(j, carry):
            zeros_to(pl.multiple_of(tail_ref[2 * N_EXPERTS] + j * EXPERT_TILE, ROW_ALIGN), EXPERT_TILE).start()
            return carry

        lax.fori_loop(0, tail_ref[2 * N_EXPERTS + 1], free_tile, 0)
        _wait_pieces(tail_ref[2 * N_EXPERTS + 2], zeros_to(0, ROW_ALIGN))


def _sort_tokens(h2, ls, tables, tail, n_rows):
    t, d = h2.shape
    tm = MOE_TOK_TILE
    grid_spec = pltpu.PrefetchScalarGridSpec(
        num_scalar_prefetch=5,
        grid=(t // tm,),
        in_specs=[pl.BlockSpec((tm, d), lambda i, *_: (i, 0)),
                  pl.BlockSpec((2, tm), lambda i, *_: (0, i))],
        out_specs=pl.BlockSpec(memory_space=pl.ANY),
        scratch_shapes=[pltpu.VMEM((2, MOE_SLOTS, d // 2), jnp.uint32), pltpu.SemaphoreType.DMA((2,))],
    )
    return pl.pallas_call(
        _sort_kernel,
        grid_spec=grid_spec,
        out_shape=jax.ShapeDtypeStruct((n_rows, d // 2), jnp.uint32),
        compiler_params=_cparams(("arbitrary",)),
        name="moe_sort",
    )(*tables, tail, h2, ls)


def _gmm_kernel(te_ref, nu_ref, xs_ref, wg_ref, wu_ref, wd_ref, ys_ref, wgb, wub, wdb):
    i = pl.program_id(0)
    used = i < nu_ref[0]
    new_expert = jnp.logical_or(i == 0, te_ref[i] != te_ref[jnp.maximum(i - 1, 0)])

    @pl.when(jnp.logical_and(used, new_expert))
    def _():
        wgb[...] = wg_ref[...].astype(BF16)
        wub[...] = wu_ref[...].astype(BF16)
        wdb[...] = wd_ref[...].astype(BF16)

    @pl.when(used)
    def _():
        half = xs_ref.shape[1]
        lo, hi = _unpack_halves(xs_ref[...])
        a = _dot(lo, wgb[0:half, :]) + _dot(hi, wgb[half:, :])
        u = _dot(lo, wub[0:half, :]) + _dot(hi, wub[half:, :])
        y = _dot((_silu(a) * u).astype(BF16), wdb[...])
        ys_ref[...] = _pack_halves(y.astype(BF16).astype(F32))

    @pl.when(jnp.logical_not(used))
    def _():
        ys_ref[...] = jnp.zeros(ys_ref.shape, jnp.uint32)


def _grouped_mlp(xs, tile_expert, n_used, wg, wu, wd, layer):
    n_rows, half = xs.shape
    d = 2 * half
    te = EXPERT_TILE
    row_map = lambda i, tex, nu: (jnp.minimum(i, nu[0] - 1), 0)
    grid_spec = pltpu.PrefetchScalarGridSpec(
        num_scalar_prefetch=2,
        grid=(n_rows // te,),
        in_specs=[pl.BlockSpec((te, half), row_map),
                  pl.BlockSpec((None, None, d, D_EXPERT), lambda i, tex, nu: (layer, tex[i], 0, 0)),
                  pl.BlockSpec((None, None, d, D_EXPERT), lambda i, tex, nu: (layer, tex[i], 0, 0)),
                  pl.BlockSpec((None, None, D_EXPERT, d), lambda i, tex, nu: (layer, tex[i], 0, 0))],
        out_specs=pl.BlockSpec((te, half), lambda i, tex, nu: (i, 0)),
        scratch_shapes=[pltpu.VMEM((d, D_EXPERT), BF16), pltpu.VMEM((d, D_EXPERT), BF16),
                        pltpu.VMEM((D_EXPERT, d), BF16)],
    )
    return pl.pallas_call(
        _gmm_kernel,
        grid_spec=grid_spec,
        out_shape=jax.ShapeDtypeStruct((n_rows, half), jnp.uint32),
        compiler_params=_cparams(("arbitrary",)),
        name="moe_grouped_mlp",
    )(tile_expert, n_used, xs, wg, wu, wd)


def _combine_kernel(start_ref, cnt_ref, loff_ref, pieces_ref, ls_ref, gate_ref, x1_ref, g2_ref, ys_ref, o_ref,
                    buf_ref, sem_ref):
    i = pl.program_id(0)
    n = pl.num_programs(0)
    tm = x1_ref.shape[0]
    slot = i % 2

    def fetch(tile, s):
        def per_expert(e, carry):
            k = tile * N_EXPERTS + e
            lo = pl.multiple_of(loff_ref[k], ROW_ALIGN)
            st = pl.multiple_of(start_ref[k], ROW_ALIGN)
            _chunk_copies(cnt_ref[k], lambda off, size: pltpu.make_async_copy(
                ys_ref.at[pl.ds(st + off, size), :], buf_ref.at[s, pl.ds(lo + off, size), :], sem_ref.at[s]))
            return carry

        lax.fori_loop(0, N_EXPERTS, per_expert, 0)

    @pl.when(i == 0)
    def _():
        fetch(0, 0)

    @pl.when(i + 1 < n)
    def _():
        fetch(i + 1, 1 - slot)

    _wait_pieces(pieces_ref[i], pltpu.make_async_copy(
        ys_ref.at[pl.ds(0, ROW_ALIGN), :], buf_ref.at[slot, pl.ds(0, ROW_ALIGN), :], sem_ref.at[slot]))

    srow = lax.broadcasted_iota(jnp.int32, (MOE_SLOTS, tm), 0).astype(F32)
    ls = ls_ref[...]
    gate = gate_ref[...]
    weights = (jnp.where(srow == ls[0:1], gate[0:1], 0.0) + jnp.where(srow == ls[1:2], gate[1:2], 0.0)).astype(BF16)
    rows = jnp.where(lax.broadcasted_iota(jnp.int32, (MOE_SLOTS, 1), 0) < pieces_ref[i] * ROW_ALIGN,
                     buf_ref[slot], jnp.uint32(0))
    lo, hi = _unpack_halves(rows)
    half = lo.shape[1]
    o_ref[:, 0:half] = x1_ref[:, 0:half] + g2_ref[0, :, 0:half] * _dot_tn(weights, lo)
    o_ref[:, half:] = x1_ref[:, half:] + g2_ref[0, :, half:] * _dot_tn(weights, hi)


def _combine(ys, ls, gate, x1, g2, tables, per_batch):
    t, d = x1.shape
    tm = MOE_TOK_TILE
    grid_spec = pltpu.PrefetchScalarGridSpec(
        num_scalar_prefetch=4,
        grid=(t // tm,),
        in_specs=[pl.BlockSpec((2, tm), lambda i, *_: (0, i)),
                  pl.BlockSpec((2, tm), lambda i, *_: (0, i)),
                  pl.BlockSpec((tm, d), lambda i, *_: (i, 0)),
                  pl.BlockSpec((1, 1, d), lambda i, *_: (i // per_batch, 0, 0)),
                  pl.BlockSpec(memory_space=pl.ANY)],
        out_specs=pl.BlockSpec((tm, d), lambda i, *_: (i, 0)),
        scratch_shapes=[pltpu.VMEM((2, MOE_SLOTS, d // 2), jnp.uint32), pltpu.SemaphoreType.DMA((2,))],
    )
    return pl.pallas_call(
        _combine_kernel,
        grid_spec=grid_spec,
        out_shape=jax.ShapeDtypeStruct((t, d), F32),
        compiler_params=_cparams(("arbitrary",)),
        name="moe_combine",
    )(*tables, ls, gate, x1, g2.reshape(-1, 1, d), ys)


def kernel(x, c, ada_w, ada_b, norm1_w, norm2_w, w_in, pool_w, pool_scale, hgrn_lb_raw, hgrn_norm_w, q_norm_w,
           k_norm_w, attn_sinks, w_out, router_w, router_bias, expert_w_gate, expert_w_up, expert_w_down):
    b, s, d = x.shape
    depth = ada_w.shape[0]
    t = b * s
    per_batch = s // MOE_TOK_TILE
    n_rows = _moe_rows(t)

    p = jax.nn.softmax(hgrn_lb_raw.astype(F32), axis=0)
    lower_bounds = jnp.maximum(jnp.cumsum(p, axis=0) - p[0:1], 0.0)

    mod = _modulation(c, ada_w, ada_b)
    hgrn_consts = _hgrn_constants() + (_head_ones(HGRN_WIDTH, HGRN_KDIM),)
    attn_consts = (_attn_bias(), _head_ones(ATT_WIDTH, HEAD_DIM), _head_ones(KV_WIDTH, HEAD_DIM))
    router_consts = _router_constants(MOE_TOK_TILE)
    rwt_b = router_w.T.astype(BF16)
    w_in_b = w_in.astype(BF16)
    w_out_b = w_out.astype(BF16)

    for l in range(depth):
        sh1, sc1, g1, sh2, sc2, g2 = [mod[l, :, j * d:(j + 1) * d] for j in range(6)]
        za, zh, zq, zkv = _in_proj(x, sc1, sh1, norm1_w[l], w_in_b, l)
        yp = _pool_mixer(za, _block_diag(pool_w[l]).astype(BF16), pool_scale[l])
        yh = _hgrn_mixer(zh, lower_bounds[l], hgrn_norm_w[l], hgrn_consts)
        ya = _attn_mixer(zq, zkv, q_norm_w[l], k_norm_w[l], attn_sinks[l], attn_consts)
        x1, h2, ls, gate, cnt_out = _out_proj_router(
            yp, yh, ya, x, g1, sc2, sh2, norm2_w[l], w_out_b, l, rwt_b, router_bias, router_consts)
        tables, tail, tile_expert, n_used = _moe_tables(cnt_out)
        xs = _sort_tokens(h2, ls, tables, tail, n_rows)
        ys = _grouped_mlp(xs, tile_expert, n_used, expert_w_gate, expert_w_up, expert_w_down, l)
        x = _combine(ys, ls, gate, x1, g2, tables, per_batch).reshape(b, s, d)
    return x
```

```python
import functools

import numpy as np
import jax
import jax.numpy as jnp
from jax import lax
from jax.experimental import pallas as pl
from jax.experimental.pallas import tpu as pltpu

F32 = jnp.float32
BF16 = jnp.bfloat16

D_MODEL = 1024
POOL_WINDOWS = (2, 4, 8, 16)
POOL_WIDTH = 256
POOL_GROUP = 64
POOL_HALO = 16
HGRN_HEADS = 4
HGRN_KDIM = 64
HGRN_WIDTH = 256
HGRN_CHUNK = 64
ATT_HEADS = 8
ATT_KV_HEADS = 2
HEAD_DIM = 64
ATT_WIDTH = 512
KV_WIDTH = 128
WINDOW = 128
N_EXPERTS = 16
N_GROUPS = 4
EXPERTS_PER_GROUP = 4
D_EXPERT = 512
EPS = 1e-6
MAX_ONE_MINUS_F = 1.0 - 1e-6
NEG = -1e30

VMEM_LIMIT = 48 * 1024 * 1024

SEQ_TILE = 512
MOE_TOK_TILE = 256
ROW_ALIGN = 16
MOE_SLOTS = 768
EXPERT_TILE = 512
CHUNK_SIZES = (256, 128, 64, 32, 16)
MAX_PIECES = MOE_SLOTS // ROW_ALIGN
WAIT_PIECES = (32, 16, 8, 4, 2, 1)


def _sigmoid(x):
    return 1.0 / (1.0 + jnp.exp(-x))


def _silu(x):
    return x * _sigmoid(x)


def _cparams(sem):
    return pltpu.CompilerParams(dimension_semantics=sem, vmem_limit_bytes=VMEM_LIMIT)


def _dot(a, b):
    return jnp.dot(a, b, preferred_element_type=F32)


def _dot_nt(a, b):
    return lax.dot_general(a, b, (((1,), (1,)), ((), ())), preferred_element_type=F32)


def _dot_tn(a, b):
    return lax.dot_general(a, b, (((0,), (0,)), ((), ())), preferred_element_type=F32)


def _split2(x):
    hi = x.astype(BF16)
    lo = (x - hi.astype(F32)).astype(BF16)
    return hi, lo


def _head_sumsq(x, bd):
    hi, lo = _split2(x * x)
    return _dot(hi, bd) + _dot(lo, bd)


def _mod_kernel(c_ref, w_ref, b_ref, o_ref):
    cond = _silu(c_ref[...])
    o_ref[0] = _dot(cond.astype(BF16), w_ref[0].astype(BF16)) + b_ref[0]


def _modulation(c, ada_w, ada_b):
    depth, d, n = ada_w.shape
    b = c.shape[0]
    nb = n // d
    return pl.pallas_call(
        _mod_kernel,
        grid=(depth, nb),
        in_specs=[pl.BlockSpec((b, d), lambda l, j: (0, 0)),
                  pl.BlockSpec((1, d, d), lambda l, j: (l, 0, j)),
                  pl.BlockSpec((1, 1, d), lambda l, j: (l, 0, j))],
        out_specs=pl.BlockSpec((1, b, d), lambda l, j: (l, 0, j)),
        out_shape=jax.ShapeDtypeStruct((depth, b, n), F32),
        compiler_params=_cparams(("parallel", "parallel")),
        name="adaln_mod",
    )(c, ada_w, ada_b.reshape(depth, 1, n))


def _in_kernel(x_ref, sc_ref, sh_ref, nw_ref, w_ref, za_ref, zh_ref, zq_ref, zkv_ref):
    x = x_ref[0]
    ms = jnp.mean(x * x, axis=-1, keepdims=True)
    h = (x * lax.rsqrt(ms + EPS) * nw_ref[...]) * (1.0 + sc_ref[0]) + sh_ref[0]
    z = _dot(h.astype(BF16), w_ref[...])
    za_ref[0] = z[:, :POOL_WIDTH]
    zh_ref[0] = z[:, POOL_WIDTH:POOL_WIDTH + 4 * HGRN_WIDTH]
    zq_ref[0] = z[:, POOL_WIDTH + 4 * HGRN_WIDTH:POOL_WIDTH + 4 * HGRN_WIDTH + ATT_WIDTH]
    zkv_ref[0] = z[:, POOL_WIDTH + 4 * HGRN_WIDTH + ATT_WIDTH:]


def _in_proj(x, sc, sh, nw, w_in_b, layer):
    b, s, d = x.shape
    n = w_in_b.shape[2]
    ts = min(SEQ_TILE, s)
    tok = lambda w: pl.BlockSpec((1, ts, w), lambda bi, i: (bi, i, 0))
    vec = pl.BlockSpec((1, 1, d), lambda bi, i: (bi, 0, 0))
    widths = (POOL_WIDTH, 4 * HGRN_WIDTH, ATT_WIDTH, 2 * KV_WIDTH)
    return pl.pallas_call(
        _in_kernel,
        grid=(b, s // ts),
        in_specs=[tok(d), vec, vec,
                  pl.BlockSpec((1, d), lambda bi, i: (0, 0)),
                  pl.BlockSpec((None, d, n), lambda bi, i: (layer, 0, 0))],
        out_specs=[tok(w) for w in widths],
        out_shape=[jax.ShapeDtypeStruct((b, s, w), F32) for w in widths],
        compiler_params=_cparams(("parallel", "parallel")),
        name="in_proj",
    )(x, sc.reshape(b, 1, d), sh.reshape(b, 1, d), nw.reshape(1, d), w_in_b)


def _pool_kernel(a_ref, w_ref, scale_ref, o_ref, buf_ref):
    i = pl.program_id(1)
    ts = a_ref.shape[1]
    a = a_ref[0]

    @pl.when(i == 0)
    def _():
        buf_ref[0:POOL_HALO, :] = jnp.zeros((POOL_HALO, POOL_WIDTH), F32)

    buf_ref[POOL_HALO:POOL_HALO + ts, :] = a
    lane = lax.broadcasted_iota(jnp.int32, (1, POOL_WIDTH), 1)
    win = jnp.left_shift(2, lane // POOL_GROUP)
    acc = a
    for j in range(1, POOL_HALO):
        shifted = buf_ref[POOL_HALO - j:POOL_HALO - j + ts, :]
        acc = acc + jnp.where(win > j, shifted, 0.0)
    pos = i * ts + lax.broadcasted_iota(jnp.int32, (ts, 1), 0)
    count = jnp.minimum(pos + 1, win).astype(F32)
    pooled = acc / count - a
    y = _dot(pooled.astype(BF16), w_ref[...]) * scale_ref[...]
    o_ref[0] = y.astype(o_ref.dtype)
    buf_ref[0:POOL_HALO, :] = a[ts - POOL_HALO:, :]


def _pool_mixer(za, pool_bd, pool_scale):
    b, s, w = za.shape
    ts = min(SEQ_TILE, s)
    return pl.pallas_call(
        _pool_kernel,
        grid=(b, s // ts),
        in_specs=[pl.BlockSpec((1, ts, w), lambda bi, i: (bi, i, 0)),
                  pl.BlockSpec((w, w), lambda bi, i: (0, 0)),
                  pl.BlockSpec((1, w), lambda bi, i: (0, 0))],
        out_specs=pl.BlockSpec((1, ts, w), lambda bi, i: (bi, i, 0)),
        out_shape=jax.ShapeDtypeStruct((b, s, w), BF16),
        scratch_shapes=[pltpu.VMEM((POOL_HALO + ts, w), F32)],
        compiler_params=_cparams(("parallel", "arbitrary")),
        name="pool_mixer",
    )(za, pool_bd, pool_scale.reshape(1, w))


def _block_diag(blocks):
    g, n, _ = blocks.shape
    eye = jnp.eye(g, dtype=blocks.dtype)
    return (eye[:, None, :, None] * blocks[:, :, None, :]).reshape(g * n, g * n)


def _head_ones(width, head):
    idx = np.arange(width) // head
    return jnp.asarray((idx[:, None] == idx[None, :]).astype(np.float32), dtype=BF16)


HGRN_LEVELS = (32, 16, 8, 4, 2, 1)
N_LEVEL_MASKS = len(HGRN_LEVELS) + 1
ROW_B = len(HGRN_LEVELS) * HGRN_CHUNK
ROW_R = ROW_B + HGRN_CHUNK


def _hgrn_constants():
    c = HGRN_CHUNK
    m = np.zeros((len(HGRN_LEVELS) + 2, c, c), np.float32)
    masks = np.zeros((N_LEVEL_MASKS, c, c), np.float32)
    for lvl, n in enumerate(HGRN_LEVELS):
        for t in range(c):
            blk = t // (2 * n)
            mid = blk * 2 * n + n
            if t >= mid:
                m[lvl, t, mid:t + 1] = 1.0
                masks[lvl, t, blk * 2 * n:mid] = 1.0
            else:
                m[lvl, t, t + 1:mid] = 1.0
    for t in range(c):
        m[-2, t, :t + 1] = 1.0
        m[-1, t, t + 1:] = 1.0
        masks[-1, t, t] = 1.0
    m = m.reshape(-1, c)
    m3 = np.concatenate([m, m, m], axis=1)
    masks = np.tile(masks, (1, HGRN_HEADS, 1))
    return jnp.asarray(m3, dtype=BF16), jnp.asarray(masks, dtype=F32)


def _hgrn_kernel(zh_ref, lb_ref, gw_ref, m3_ref, lmask_ref, bd_ref, o_ref, st_ref):
    i = pl.program_id(1)
    ts = zh_ref.shape[1]
    c = HGRN_CHUNK
    w = HGRN_WIDTH

    @pl.when(i == 0)
    def _():
        st_ref[...] = jnp.zeros((w, w), F32)

    lane_head = lax.broadcasted_iota(jnp.int32, (1, w), 1) // HGRN_KDIM
    one_minus_lb = 1.0 - lb_ref[...]
    gw = gw_ref[...]
    bd = bd_ref[...]
    on_diag_block = bd > 0

    def chunk(ci, carry):
        r0 = pl.multiple_of(ci * c, c)
        q = zh_ref[0, pl.ds(r0, c), 0:w]
        f = zh_ref[0, pl.ds(r0, c), w:2 * w]
        v = zh_ref[0, pl.ds(r0, c), 2 * w:3 * w].astype(BF16)
        g = zh_ref[0, pl.ds(r0, c), 3 * w:4 * w]
        kk = one_minus_lb * _sigmoid(-f)
        lf = jnp.log(1.0 - jnp.minimum(kk, MAX_ONE_MINUS_F))
        qf = _silu(q)
        hi = lf.astype(BF16)
        r1 = lf - hi.astype(F32)
        mid = r1.astype(BF16)
        lo = (r1 - mid.astype(F32)).astype(BF16)
        sums = _dot(m3_ref[...], jnp.concatenate([hi, mid, lo], axis=0))
        dec = jnp.exp(sums)

        scores = jnp.zeros((HGRN_HEADS * c, c), F32)
        for lvl in range(N_LEVEL_MASKS):
            if lvl < len(HGRN_LEVELS):
                e = dec[lvl * c:(lvl + 1) * c]
                ql = (qf * e).astype(BF16)
                kl = (kk * e).astype(BF16)
            else:
                ql = qf.astype(BF16)
                kl = kk.astype(BF16)
            zero = jnp.zeros_like(ql)
            qs = jnp.concatenate([jnp.where(lane_head == h, ql, zero) for h in range(HGRN_HEADS)], axis=0)
            scores = scores + _dot_nt(qs, kl) * lmask_ref[lvl]
        r = _dot(scores.astype(BF16), v)
        o = jnp.zeros((c, w), F32)
        for h in range(HGRN_HEADS):
            o = o + jnp.where(lane_head == h, r[h * c:(h + 1) * c], 0.0)
        eb = dec[ROW_B:ROW_B + c]
        er = dec[ROW_R:ROW_R + c]
        st = st_ref[...]
        o = o + _dot_nt((qf * eb).astype(BF16), st.astype(BF16))
        upd = _dot_tn(v, (kk * er).astype(BF16))
        st_ref[...] = st * eb[c - 1:c, :] + jnp.where(on_diag_block, upd, 0.0)
        ss = _head_sumsq(o, bd)
        y = o * lax.rsqrt(ss * (1.0 / HGRN_KDIM) + EPS) * gw * _silu(g)
        o_ref[0, pl.ds(r0, c), :] = y.astype(o_ref.dtype)
        return carry

    lax.fori_loop(0, ts // c, chunk, 0)


def _hgrn_mixer(zh, lb, norm_w, consts):
    b, s, _ = zh.shape
    w = HGRN_WIDTH
    ts = min(SEQ_TILE, s)
    m3, lmask, bd = consts
    full = lambda a: pl.BlockSpec(a.shape, lambda bi, i: (0,) * a.ndim)
    return pl.pallas_call(
        _hgrn_kernel,
        grid=(b, s // ts),
        in_specs=[pl.BlockSpec((1, ts, 4 * w), lambda bi, i: (bi, i, 0)),
                  pl.BlockSpec((1, w), lambda bi, i: (0, 0)),
                  pl.BlockSpec((1, w), lambda bi, i: (0, 0)),
                  full(m3), full(lmask), full(bd)],
        out_specs=pl.BlockSpec((1, ts, w), lambda bi, i: (bi, i, 0)),
        out_shape=jax.ShapeDtypeStruct((b, s, w), BF16),
        scratch_shapes=[pltpu.VMEM((w, w), F32)],
        compiler_params=_cparams(("parallel", "arbitrary")),
        name="hgrn_mixer",
    )(zh, lb.reshape(1, w), norm_w.reshape(1, w), m3, lmask, bd)


def _attn_bias():
    qi = np.arange(WINDOW)[:, None]
    kj = np.arange(2 * WINDOW)[None, :]
    dist = qi + WINDOW - kj
    valid = (dist >= 0) & (dist < WINDOW)
    slopes = np.exp2(-8.0 * np.arange(1, ATT_HEADS + 1) / ATT_HEADS)
    bias = np.where(valid[None], -slopes[:, None, None] * dist[None], NEG)
    return jnp.asarray(bias, dtype=F32)


def _attn_kernel(sink_ref, zq_ref, zkv_ref, qw_ref, kw_ref, bias_ref, bdq_ref, bdk_ref, o_ref,
                 qbuf, kbuf, vbuf):
    i = pl.program_id(1)
    ts = zq_ref.shape[1]
    hw = 2 * HEAD_DIM

    @pl.when(i == 0)
    def _():
        kbuf[:, 0:WINDOW, :] = jnp.zeros((4, WINDOW, hw), BF16)
        vbuf[:, 0:WINDOW, :] = jnp.zeros((4, WINDOW, hw), BF16)

    q = zq_ref[0]
    ssq = _head_sumsq(q, bdq_ref[...])
    qbuf[...] = (q * lax.rsqrt(ssq * (1.0 / HEAD_DIM) + EPS) * (qw_ref[...] * HEAD_DIM ** -0.5)).astype(BF16)
    kv = zkv_ref[0]
    k = kv[:, :KV_WIDTH]
    v = kv[:, KV_WIDTH:]
    ssk = _head_sumsq(k, bdk_ref[...])
    kn = k * lax.rsqrt(ssk * (1.0 / HEAD_DIM) + EPS) * kw_ref[...]
    kr = pltpu.roll(kn, HEAD_DIM, 1)
    vr = pltpu.roll(v, HEAD_DIM, 1)
    low = lax.broadcasted_iota(jnp.int32, (1, hw), 1) < HEAD_DIM
    for j in range(ATT_KV_HEADS):
        for half in range(2):
            keep = low if half == 0 else jnp.logical_not(low)
            ksrc = kn if j == half else kr
            vsrc = v if j == half else vr
            kbuf[2 * j + half, WINDOW:WINDOW + ts, :] = jnp.where(keep, ksrc, 0.0).astype(BF16)
            vbuf[2 * j + half, WINDOW:WINDOW + ts, :] = jnp.where(keep, vsrc, 0.0).astype(BF16)

    prev_cols = lax.broadcasted_iota(jnp.int32, (1, 2 * WINDOW), 1) < WINDOW

    def block(n, carry):
        r0 = pl.multiple_of(n * WINDOW, WINDOW)
        no_prev = jnp.logical_and(i == 0, n == 0)
        pen = jnp.where(jnp.logical_and(prev_cols, no_prev), NEG, 0.0)
        for hp in range(ATT_HEADS // 2):
            qp = qbuf[pl.ds(r0, WINDOW), hp * hw:(hp + 1) * hw]
            acc = jnp.zeros((WINDOW, hw), F32)
            for half in range(2):
                h = 2 * hp + half
                j = h // (ATT_HEADS // ATT_KV_HEADS)
                keys = kbuf[2 * j + half, pl.ds(r0, 2 * WINDOW), :]
                vals = vbuf[2 * j + half, pl.ds(r0, 2 * WINDOW), :]
                logits = _dot_nt(qp, keys) + bias_ref[h] + pen
                sink = sink_ref[h]
                m = jnp.maximum(jnp.max(logits, axis=-1, keepdims=True), sink)
                p = jnp.exp(logits - m)
                denom = jnp.sum(p, axis=-1, keepdims=True) + jnp.exp(sink - m)
                acc = acc + _dot(p.astype(BF16), vals) * (1.0 / denom)
            o_ref[0, pl.ds(r0, WINDOW), hp * hw:(hp + 1) * hw] = acc.astype(o_ref.dtype)
        return carry

    lax.fori_loop(0, ts // WINDOW, block, 0)
    kbuf[:, 0:WINDOW, :] = kbuf[:, ts:ts + WINDOW, :]
    vbuf[:, 0:WINDOW, :] = vbuf[:, ts:ts + WINDOW, :]


def _attn_mixer(zq, zkv, q_norm_w, k_norm_w, sinks, consts):
    b, s, _ = zq.shape
    ts = min(SEQ_TILE, s)
    bias, bdq, bdk = consts
    hw = 2 * HEAD_DIM
    qw = jnp.tile(q_norm_w, ATT_HEADS).reshape(1, ATT_WIDTH)
    kw = jnp.tile(k_norm_w, ATT_KV_HEADS).reshape(1, KV_WIDTH)
    full = lambda a: pl.BlockSpec(a.shape, lambda bi, i, sk: (0,) * a.ndim)
    grid_spec = pltpu.PrefetchScalarGridSpec(
        num_scalar_prefetch=1,
        grid=(b, s // ts),
        in_specs=[pl.BlockSpec((1, ts, ATT_WIDTH), lambda bi, i, sk: (bi, i, 0)),
                  pl.BlockSpec((1, ts, 2 * KV_WIDTH), lambda bi, i, sk: (bi, i, 0)),
                  full(qw), full(kw), full(bias), full(bdq), full(bdk)],
        out_specs=pl.BlockSpec((1, ts, ATT_WIDTH), lambda bi, i, sk: (bi, i, 0)),
        scratch_shapes=[pltpu.VMEM((ts, ATT_WIDTH), BF16),
                        pltpu.VMEM((4, WINDOW + ts, hw), BF16),
                        pltpu.VMEM((4, WINDOW + ts, hw), BF16)],
    )
    return pl.pallas_call(
        _attn_kernel,
        grid_spec=grid_spec,
        out_shape=jax.ShapeDtypeStruct((b, s, ATT_WIDTH), BF16),
        compiler_params=_cparams(("parallel", "arbitrary")),
        name="swa_mixer",
    )(sinks, zq, zkv, qw, kw, bias, bdq, bdk)


def _router_constants(tm):
    t = np.arange(tm)
    before = (t[:, None] < t[None, :]).astype(np.float32)
    e = np.arange(N_EXPERTS)
    lower = (e[None, :] < e[:, None]).astype(np.float32)
    return (jnp.asarray(before, dtype=BF16), jnp.asarray(np.ones((tm, tm), np.float32), dtype=BF16),
            jnp.asarray(lower, dtype=BF16))


def _out_kernel(yp_ref, yh_ref, ya_ref, x_ref, g1_ref, sc_ref, sh_ref, nw_ref, wo_ref, rwt_ref, rb_ref,
                before_ref, ones_ref, lower_ref, x1_ref, h2_ref, ls_ref, gate_ref, cnt_ref):
    tm = x_ref.shape[0]
    p0, p1 = POOL_WIDTH, POOL_WIDTH + HGRN_WIDTH
    mix = (_dot(yp_ref[...], wo_ref[0:p0, :]) + _dot(yh_ref[...], wo_ref[p0:p1, :])
           + _dot(ya_ref[...], wo_ref[p1:, :]))
    x1 = x_ref[...] + g1_ref[0] * mix
    x1_ref[...] = x1
    ms = jnp.mean(x1 * x1, axis=-1, keepdims=True)
    h2 = ((x1 * lax.rsqrt(ms + EPS) * nw_ref[...]) * (1.0 + sc_ref[0]) + sh_ref[0]).astype(BF16)
    h2_ref[...] = h2

    logits = _dot_nt(rwt_ref[...], h2)
    ex = jnp.exp(logits - jnp.max(logits, axis=0, keepdims=True))
    scores = ex / jnp.sum(ex, axis=0, keepdims=True)
    sel = scores + rb_ref[...]
    group_scores = []
    for g in range(N_GROUPS):
        rows = [sel[g * EXPERTS_PER_GROUP + a:g * EXPERTS_PER_GROUP + a + 1] for a in range(EXPERTS_PER_GROUP)]
        best_pair = None
        for a in range(EXPERTS_PER_GROUP):
            for bb in range(a + 1, EXPERTS_PER_GROUP):
                pair = rows[a] + rows[bb]
                best_pair = pair if best_pair is None else jnp.maximum(best_pair, pair)
        group_scores.append(best_pair)
    top = functools.reduce(jnp.maximum, group_scores)
    best = jnp.full((1, tm), N_GROUPS - 1, jnp.int32)
    for g in reversed(range(N_GROUPS - 1)):
        best = jnp.where(group_scores[g] == top, g, best)
    row = lax.broadcasted_iota(jnp.int32, (N_EXPERTS, tm), 0)
    cand = jnp.where(row // EXPERTS_PER_GROUP == best, sel, NEG)
    m1 = jnp.max(cand, axis=0, keepdims=True)
    i1 = jnp.min(jnp.where(cand == m1, row, N_EXPERTS), axis=0, keepdims=True)
    oh1 = row == i1
    cand = jnp.where(oh1, NEG, cand)
    m2 = jnp.max(cand, axis=0, keepdims=True)
    i2 = jnp.min(jnp.where(cand == m2, row, N_EXPERTS), axis=0, keepdims=True)
    oh2 = row == i2
    s1 = jnp.sum(jnp.where(oh1, scores, 0.0), axis=0, keepdims=True)
    s2 = jnp.sum(jnp.where(oh2, scores, 0.0), axis=0, keepdims=True)
    gate_ref[...] = jnp.concatenate([s1 / (s1 + s2), s2 / (s1 + s2)], axis=0)

    chosen = jnp.where(jnp.logical_or(oh1, oh2), 1.0, 0.0).astype(BF16)
    rank = _dot(chosen, before_ref[...])
    count = _dot(chosen, ones_ref[...])
    aligned = jnp.floor((count + (ROW_ALIGN - 1)) * (1.0 / ROW_ALIGN)) * ROW_ALIGN
    slot = _dot(lower_ref[...], aligned.astype(BF16)) + rank
    ls_ref[...] = jnp.concatenate([jnp.sum(jnp.where(oh1, slot, 0.0), axis=0, keepdims=True),
                                   jnp.sum(jnp.where(oh2, slot, 0.0), axis=0, keepdims=True)], axis=0)
    cnt_ref[0] = count[:, 0:128]


def _out_proj_router(yp, yh, ya, x, g1, sc, sh, nw, wo_b, layer, rwt_b, rb, consts):
    b, s, d = x.shape
    t = b * s
    tm = MOE_TOK_TILE
    per_batch = s // tm
    nt = t // tm
    before, ones, lower = consts
    tok = lambda w: pl.BlockSpec((tm, w), lambda i: (i, 0))
    vec = pl.BlockSpec((1, 1, d), lambda i: (i // per_batch, 0, 0))
    full = lambda a: pl.BlockSpec(a.shape, lambda i: (0,) * a.ndim)
    lanes = pl.BlockSpec((2, tm), lambda i: (0, i))
    return pl.pallas_call(
        _out_kernel,
        grid=(nt,),
        in_specs=[tok(POOL_WIDTH), tok(HGRN_WIDTH), tok(ATT_WIDTH), tok(d), vec, vec, vec,
                  pl.BlockSpec((1, d), lambda i: (0, 0)),
                  pl.BlockSpec((None,) + wo_b.shape[1:], lambda i: (layer, 0, 0)), full(rwt_b),
                  pl.BlockSpec((N_EXPERTS, 1), lambda i: (0, 0)), full(before), full(ones), full(lower)],
        out_specs=[tok(d), tok(d), lanes, lanes, pl.BlockSpec((1, N_EXPERTS, 128), lambda i: (i, 0, 0))],
        out_shape=[jax.ShapeDtypeStruct((t, d), F32), jax.ShapeDtypeStruct((t, d), BF16),
                   jax.ShapeDtypeStruct((2, t), F32), jax.ShapeDtypeStruct((2, t), F32),
                   jax.ShapeDtypeStruct((nt, N_EXPERTS, 128), F32)],
        compiler_params=_cparams(("parallel",)),
        name="out_proj_router",
    )(yp.reshape(t, -1), yh.reshape(t, -1), ya.reshape(t, -1), x.reshape(t, d),
      g1.reshape(b, 1, d), sc.reshape(b, 1, d), sh.reshape(b, 1, d), nw.reshape(1, d), wo_b, rwt_b,
      rb.reshape(N_EXPERTS, 1), before, ones, lower)


def _moe_tables(cnt_out):
    cnt = jnp.round(cnt_out[:, :, 0]).astype(jnp.int32)
    cnt = (cnt + ROW_ALIGN - 1) // ROW_ALIGN * ROW_ALIGN
    total = jnp.sum(cnt, axis=0)
    padded = (total + EXPERT_TILE - 1) // EXPERT_TILE * EXPERT_TILE
    ends = jnp.cumsum(padded)
    first = ends - padded
    start = first[None, :] + jnp.cumsum(cnt, axis=0) - cnt
    loff = jnp.cumsum(cnt, axis=1) - cnt
    n_tiles = _moe_rows(cnt.shape[0] * MOE_TOK_TILE) // EXPERT_TILE
    n_used = ends[-1] // EXPERT_TILE
    tile_row = jnp.minimum(jnp.arange(n_tiles), n_used - 1) * EXPERT_TILE
    tile_expert = jnp.sum((ends[None, :] <= tile_row[:, None]).astype(jnp.int32), axis=1)
    pieces = jnp.sum(cnt, axis=1) // ROW_ALIGN
    row = jnp.arange(MAX_PIECES, dtype=jnp.int32) * ROW_ALIGN
    owner = jnp.sum(((loff + cnt)[:, None, :] <= row[None, :, None]).astype(jnp.int32), axis=2)
    owner = jnp.minimum(owner, N_EXPERTS - 1)
    dest = jnp.take_along_axis(start - loff, owner, axis=1) + row[None, :]
    tail = jnp.concatenate([first + total, padded - total, jnp.stack([ends[-1], n_tiles - n_used])])
    return ((dest.reshape(-1).astype(jnp.int32), pieces.astype(jnp.int32)),
            tail.astype(jnp.int32), tile_expert.astype(jnp.int32), n_used.reshape(1).astype(jnp.int32))


def _moe_rows(n_tokens):
    per_tile = 2 * MOE_TOK_TILE + N_EXPERTS * (ROW_ALIGN - 1)
    rows = (n_tokens // MOE_TOK_TILE) * per_tile + N_EXPERTS * (EXPERT_TILE - ROW_ALIGN)
    return (rows + EXPERT_TILE - 1) // EXPERT_TILE * EXPERT_TILE


def _chunk_copies(count, make_copy, wait=False):
    for size in CHUNK_SIZES:
        offset = jnp.bitwise_and(count, ~(2 * size - 1))

        @pl.when(jnp.bitwise_and(count, size) != 0)
        def _():
            copy = make_copy(pl.multiple_of(offset, ROW_ALIGN), size)
            copy.wait() if wait else copy.start()


def _wait_rows(n_pieces, make_copy):
    for p in WAIT_PIECES:
        @pl.when(jnp.bitwise_and(n_pieces, p) != 0)
        def _():
            make_copy(p * ROW_ALIGN).wait()


def _slot_rows(tm):
    return lax.broadcasted_iota(jnp.int32, (MOE_SLOTS, tm), 0).astype(F32)


def _sort_kernel(dest_ref, pieces_ref, tail_ref, h2_ref, ls_ref, xs_ref, buf_ref, sem_ref):
    i = pl.program_id(0)
    n = pl.num_programs(0)
    tm = h2_ref.shape[0]
    slot = i % 2

    def sent(tile, s):
        _wait_rows(pieces_ref[tile], lambda rows: pltpu.make_async_copy(
            buf_ref.at[s, pl.ds(0, rows), :], xs_ref.at[pl.ds(0, rows), :], sem_ref.at[s]))

    @pl.when(i >= 2)
    def _():
        sent(i - 2, slot)

    ls = ls_ref[...]
    srow = _slot_rows(tm)
    perm = jnp.where(jnp.logical_or(srow == ls[0:1], srow == ls[1:2]), 1.0, 0.0).astype(BF16)
    buf_ref[slot] = _dot(perm, h2_ref[...]).astype(BF16)

    def send_piece(j, carry):
        src = pl.multiple_of(j * ROW_ALIGN, ROW_ALIGN)
        dst = pl.multiple_of(dest_ref[i * MAX_PIECES + j], ROW_ALIGN)
        pltpu.make_async_copy(buf_ref.at[slot, pl.ds(src, ROW_ALIGN), :], xs_ref.at[pl.ds(dst, ROW_ALIGN), :],
                              sem_ref.at[slot]).start()
        return carry

    lax.fori_loop(0, pieces_ref[i], send_piece, 0)

    @pl.when(i == n - 1)
    def _():
        @pl.when(n > 1)
        def _():
            sent(i - 1, 1 - slot)
        sent(i, slot)

        buf_ref[0] = jnp.zeros(buf_ref.shape[1:], BF16)
        zeros_to = lambda row, size: pltpu.make_async_copy(
            buf_ref.at[0, pl.ds(0, size), :], xs_ref.at[pl.ds(row, size), :], sem_ref.at[0])

        for wait in (False, True):
            def expert_tail(e, carry):
                st = pl.multiple_of(tail_ref[e], ROW_ALIGN)
                _chunk_copies(tail_ref[N_EXPERTS + e], lambda off, size: zeros_to(st + off, size), wait)
                return carry

            def free_tile(j, carry):
                copy = zeros_to(pl.multiple_of(tail_ref[2 * N_EXPERTS] + j * EXPERT_TILE, ROW_ALIGN), EXPERT_TILE)
                copy.wait() if wait else copy.start()
                return carry

            lax.fori_loop(0, N_EXPERTS, expert_tail, 0)
            lax.fori_loop(0, tail_ref[2 * N_EXPERTS + 1], free_tile, 0)


def _sort_tokens(h2, ls, tables, tail, n_rows):
    t, d = h2.shape
    tm = MOE_TOK_TILE
    grid_spec = pltpu.PrefetchScalarGridSpec(
        num_scalar_prefetch=3,
        grid=(t // tm,),
        in_specs=[pl.BlockSpec((tm, d), lambda i, *_: (i, 0)),
                  pl.BlockSpec((2, tm), lambda i, *_: (0, i))],
        out_specs=pl.BlockSpec(memory_space=pl.ANY),
        scratch_shapes=[pltpu.VMEM((2, MOE_SLOTS, d), BF16), pltpu.SemaphoreType.DMA((2,))],
    )
    return pl.pallas_call(
        _sort_kernel,
        grid_spec=grid_spec,
        out_shape=jax.ShapeDtypeStruct((n_rows, d), BF16),
        compiler_params=_cparams(("arbitrary",)),
        name="moe_sort",
    )(*tables, tail, h2, ls)


def _gmm_kernel(te_ref, nu_ref, xs_ref, wg_ref, wu_ref, wd_ref, ys_ref, wgb, wub, wdb):
    i = pl.program_id(0)
    used = i < nu_ref[0]
    new_expert = jnp.logical_or(i == 0, te_ref[i] != te_ref[jnp.maximum(i - 1, 0)])

    @pl.when(jnp.logical_and(used, new_expert))
    def _():
        wgb[...] = wg_ref[...].astype(BF16)
        wub[...] = wu_ref[...].astype(BF16)
        wdb[...] = wd_ref[...].astype(BF16)

    @pl.when(used)
    def _():
        x = xs_ref[...]
        he = _silu(_dot(x, wgb[...])) * _dot(x, wub[...])
        ys_ref[...] = _dot(he.astype(BF16), wdb[...]).astype(BF16)

    @pl.when(jnp.logical_not(used))
    def _():
        ys_ref[...] = jnp.zeros(ys_ref.shape, BF16)


def _grouped_mlp(xs, tile_expert, n_used, wg, wu, wd, layer):
    n_rows, d = xs.shape
    te = EXPERT_TILE
    row_map = lambda i, tex, nu: (jnp.minimum(i, nu[0] - 1), 0)
    grid_spec = pltpu.PrefetchScalarGridSpec(
        num_scalar_prefetch=2,
        grid=(n_rows // te,),
        in_specs=[pl.BlockSpec((te, d), row_map),
                  pl.BlockSpec((None, None, d, D_EXPERT), lambda i, tex, nu: (layer, tex[i], 0, 0)),
                  pl.BlockSpec((None, None, d, D_EXPERT), lambda i, tex, nu: (layer, tex[i], 0, 0)),
                  pl.BlockSpec((None, None, D_EXPERT, d), lambda i, tex, nu: (layer, tex[i], 0, 0))],
        out_specs=pl.BlockSpec((te, d), lambda i, tex, nu: (i, 0)),
        scratch_shapes=[pltpu.VMEM((d, D_EXPERT), BF16), pltpu.VMEM((d, D_EXPERT), BF16),
                        pltpu.VMEM((D_EXPERT, d), BF16)],
    )
    return pl.pallas_call(
        _gmm_kernel,
        grid_spec=grid_spec,
        out_shape=jax.ShapeDtypeStruct((n_rows, d), BF16),
        compiler_params=_cparams(("arbitrary",)),
        name="moe_grouped_mlp",
    )(tile_expert, n_used, xs, wg, wu, wd)


def _combine_kernel(dest_ref, pieces_ref, ls_ref, gate_ref, x1_ref, g2_ref, ys_ref, o_ref, buf_ref, sem_ref):
    i = pl.program_id(0)
    n = pl.num_programs(0)
    tm = x1_ref.shape[0]
    slot = i % 2

    def fetch(tile, s):
        def fetch_piece(j, carry):
            src = pl.multiple_of(dest_ref[tile * MAX_PIECES + j], ROW_ALIGN)
            dst = pl.multiple_of(j * ROW_ALIGN, ROW_ALIGN)
            pltpu.make_async_copy(ys_ref.at[pl.ds(src, ROW_ALIGN), :], buf_ref.at[s, pl.ds(dst, ROW_ALIGN), :],
                                  sem_ref.at[s]).start()
            return carry

        lax.fori_loop(0, pieces_ref[tile], fetch_piece, 0)

    @pl.when(i == 0)
    def _():
        fetch(0, 0)

    @pl.when(i + 1 < n)
    def _():
        fetch(i + 1, 1 - slot)

    _wait_rows(pieces_ref[i], lambda rows: pltpu.make_async_copy(
        ys_ref.at[pl.ds(0, rows), :], buf_ref.at[slot, pl.ds(0, rows), :], sem_ref.at[slot]))

    srow = _slot_rows(tm)
    ls = ls_ref[...]
    gate = gate_ref[...]
    weights = (jnp.where(srow == ls[0:1], gate[0:1], 0.0)
               + jnp.where(srow == ls[1:2], gate[1:2], 0.0)).astype(BF16)
    written = lax.broadcasted_iota(jnp.int32, (MOE_SLOTS, 1), 0) < pieces_ref[i] * ROW_ALIGN
    rows = jnp.where(written, buf_ref[slot], jnp.zeros((), BF16))
    y = _dot_tn(weights, rows)
    o_ref[...] = x1_ref[...] + g2_ref[0] * y


def _combine(ys, ls, gate, x1, g2, tables, per_batch):
    t, d = x1.shape
    tm = MOE_TOK_TILE
    grid_spec = pltpu.PrefetchScalarGridSpec(
        num_scalar_prefetch=2,
        grid=(t // tm,),
        in_specs=[pl.BlockSpec((2, tm), lambda i, *_: (0, i)),
                  pl.BlockSpec((2, tm), lambda i, *_: (0, i)),
                  pl.BlockSpec((tm, d), lambda i, *_: (i, 0)),
                  pl.BlockSpec((1, 1, d), lambda i, *_: (i // per_batch, 0, 0)),
                  pl.BlockSpec(memory_space=pl.ANY)],
        out_specs=pl.BlockSpec((tm, d), lambda i, *_: (i, 0)),
        scratch_shapes=[pltpu.VMEM((2, MOE_SLOTS, d), BF16), pltpu.SemaphoreType.DMA((2,))],
    )
    return pl.pallas_call(
        _combine_kernel,
        grid_spec=grid_spec,
        out_shape=jax.ShapeDtypeStruct((t, d), F32),
        compiler_params=_cparams(("arbitrary",)),
        name="moe_combine",
    )(*tables, ls, gate, x1, g2.reshape(-1, 1, d), ys)


def kernel(x, c, ada_w, ada_b, norm1_w, norm2_w, w_in, pool_w, pool_scale, hgrn_lb_raw, hgrn_norm_w, q_norm_w,
           k_norm_w, attn_sinks, w_out, router_w, router_bias, expert_w_gate, expert_w_up, expert_w_down):
    b, s, d = x.shape
    depth = ada_w.shape[0]
    t = b * s
    per_batch = s // MOE_TOK_TILE
    n_rows = _moe_rows(t)

    p = jax.nn.softmax(hgrn_lb_raw.astype(F32), axis=0)
    lower_bounds = jnp.maximum(jnp.cumsum(p, axis=0) - p[0:1], 0.0)

    mod = _modulation(c, ada_w, ada_b)
    hgrn_consts = _hgrn_constants() + (_head_ones(HGRN_WIDTH, HGRN_KDIM),)
    attn_consts = (_attn_bias(), _head_ones(ATT_WIDTH, HEAD_DIM), _head_ones(KV_WIDTH, HEAD_DIM))
    router_consts = _router_constants(MOE_TOK_TILE)
    rwt_b = router_w.T.astype(BF16)
    w_in_b = w_in.astype(BF16)
    w_out_b = w_out.astype(BF16)

    for l in range(depth):
        sh1, sc1, g1, sh2, sc2, g2 = [mod[l, :, j * d:(j + 1) * d] for j in range(6)]
        za, zh, zq, zkv = _in_proj(x, sc1, sh1, norm1_w[l], w_in_b, l)
        yp = _pool_mixer(za, _block_diag(pool_w[l]).astype(BF16), pool_scale[l])
        yh = _hgrn_mixer(zh, lower_bounds[l], hgrn_norm_w[l], hgrn_consts)
        ya = _attn_mixer(zq, zkv, q_norm_w[l], k_norm_w[l], attn_sinks[l], attn_consts)
        x1, h2, ls, gate, cnt_out = _out_proj_router(
            yp, yh, ya, x, g1, sc2, sh2, norm2_w[l], w_out_b, l, rwt_b, router_bias, router_consts)
        tables, tail, tile_expert, n_used = _moe_tables(cnt_out)
        xs = _sort_tokens(h2, ls, tables, tail, n_rows)
        ys = _grouped_mlp(xs, tile_expert, n_used, expert_w_gate, expert_w_up, expert_w_down, l)
        x = _combine(ys, ls, gate, x1, g2, tables, per_batch).reshape(b, s, d)
    return x
```

```python
import functools

import numpy as np
import jax
import jax.numpy as jnp
from jax import lax
from jax.experimental import pallas as pl
from jax.experimental.pallas import tpu as pltpu

F32 = jnp.float32
BF16 = jnp.bfloat16

D_MODEL = 1024
POOL_WINDOWS = (2, 4, 8, 16)
POOL_WIDTH = 256
POOL_GROUP = 64
POOL_HALO = 16
HGRN_HEADS = 4
HGRN_KDIM = 64
HGRN_WIDTH = 256
HGRN_CHUNK = 64
ATT_HEADS = 8
ATT_KV_HEADS = 2
HEAD_DIM = 64
ATT_WIDTH = 512
KV_WIDTH = 128
WINDOW = 128
N_EXPERTS = 16
N_GROUPS = 4
EXPERTS_PER_GROUP = 4
D_EXPERT = 512
EPS = 1e-6
MAX_ONE_MINUS_F = 1.0 - 1e-6
NEG = -1e30

VMEM_LIMIT = 48 * 1024 * 1024

SEQ_TILE = 512
MOE_TOK_TILE = 256
ROW_ALIGN = 16
MOE_SLOTS = 768
EXPERT_TILE = 512
CHUNK_SIZES = (256, 128, 64, 32, 16)
MAX_PIECES = MOE_SLOTS // ROW_ALIGN
WAIT_PIECES = (32, 16, 8, 4, 2, 1)


def _sigmoid(x):
    return 1.0 / (1.0 + jnp.exp(-x))


def _silu(x):
    return x * _sigmoid(x)


def _cparams(sem, **kw):
    return pltpu.CompilerParams(dimension_semantics=sem, vmem_limit_bytes=VMEM_LIMIT, **kw)


def _dot(a, b):
    return jnp.dot(a, b, preferred_element_type=F32)


def _dot_nt(a, b):
    return lax.dot_general(a, b, (((1,), (1,)), ((), ())), preferred_element_type=F32)


def _dot_tn(a, b):
    return lax.dot_general(a, b, (((0,), (0,)), ((), ())), preferred_element_type=F32)


def _head_sumsq(x, bd):
    return _dot((x * x).astype(BF16), bd)


def _mod_kernel(c_ref, w_ref, b_ref, o_ref):
    cond = _silu(c_ref[...])
    o_ref[0] = _dot(cond.astype(BF16), w_ref[0].astype(BF16)) + b_ref[0]


def _modulation(c, ada_w, ada_b):
    depth, d, n = ada_w.shape
    b = c.shape[0]
    nb = n // d
    return pl.pallas_call(
        _mod_kernel,
        grid=(depth, nb),
        in_specs=[pl.BlockSpec((b, d), lambda l, j: (0, 0)),
                  pl.BlockSpec((1, d, d), lambda l, j: (l, 0, j)),
                  pl.BlockSpec((1, 1, d), lambda l, j: (l, 0, j))],
        out_specs=pl.BlockSpec((1, b, d), lambda l, j: (l, 0, j)),
        out_shape=jax.ShapeDtypeStruct((depth, b, n), F32),
        compiler_params=_cparams(("parallel", "parallel")),
        name="adaln_mod",
    )(c, ada_w, ada_b.reshape(depth, 1, n))


def _in_kernel(x_ref, sc_ref, sh_ref, nw_ref, w_ref, za_ref, zh_ref, zq_ref, zkv_ref):
    x = x_ref[0]
    ms = jnp.mean(x * x, axis=-1, keepdims=True)
    h = (x * lax.rsqrt(ms + EPS) * nw_ref[...]) * (1.0 + sc_ref[0]) + sh_ref[0]
    z = _dot(h.astype(BF16), w_ref[...])
    za_ref[0] = z[:, :POOL_WIDTH]
    zh_ref[0] = z[:, POOL_WIDTH:POOL_WIDTH + 4 * HGRN_WIDTH]
    zq_ref[0] = z[:, POOL_WIDTH + 4 * HGRN_WIDTH:POOL_WIDTH + 4 * HGRN_WIDTH + ATT_WIDTH]
    zkv_ref[0] = z[:, POOL_WIDTH + 4 * HGRN_WIDTH + ATT_WIDTH:]


def _in_proj(x, sc, sh, nw, w_in_b, layer):
    b, s, d = x.shape
    n = w_in_b.shape[2]
    ts = min(SEQ_TILE, s)
    tok = lambda w: pl.BlockSpec((1, ts, w), lambda bi, i: (bi, i, 0))
    vec = pl.BlockSpec((1, 1, d), lambda bi, i: (bi, 0, 0))
    widths = (POOL_WIDTH, 4 * HGRN_WIDTH, ATT_WIDTH, 2 * KV_WIDTH)
    return pl.pallas_call(
        _in_kernel,
        grid=(b, s // ts),
        in_specs=[tok(d), vec, vec,
                  pl.BlockSpec((1, d), lambda bi, i: (0, 0)),
                  pl.BlockSpec((None, d, n), lambda bi, i: (layer, 0, 0))],
        out_specs=[tok(w) for w in widths],
        out_shape=[jax.ShapeDtypeStruct((b, s, w), F32) for w in widths],
        compiler_params=_cparams(("parallel", "parallel")),
        name="in_proj",
    )(x, sc.reshape(b, 1, d), sh.reshape(b, 1, d), nw.reshape(1, d), w_in_b)


def _pool_kernel(a_ref, w_ref, scale_ref, o_ref, buf_ref):
    i = pl.program_id(1)
    ts = a_ref.shape[1]
    a = a_ref[0]

    @pl.when(i == 0)
    def _():
        buf_ref[0:POOL_HALO, :] = jnp.zeros((POOL_HALO, POOL_WIDTH), F32)

    buf_ref[POOL_HALO:POOL_HALO + ts, :] = a
    lane = lax.broadcasted_iota(jnp.int32, (1, POOL_WIDTH), 1)
    win = jnp.left_shift(2, lane // POOL_GROUP)
    halves = []
    for c0 in range(0, POOL_WIDTH, 2 * POOL_GROUP):
        small = POOL_WINDOWS[c0 // POOL_GROUP]
        large = POOL_WINDOWS[c0 // POOL_GROUP + 1]
        acc = a[:, c0:c0 + 2 * POOL_GROUP]
        for j in range(1, large):
            shifted = buf_ref[POOL_HALO - j:POOL_HALO - j + ts, c0:c0 + 2 * POOL_GROUP]
            acc = acc + (shifted if j < small else jnp.where(win[:, c0:c0 + 2 * POOL_GROUP] > j, shifted, 0.0))
        halves.append(acc)
    acc = jnp.concatenate(halves, axis=1)
    pos = i * ts + lax.broadcasted_iota(jnp.int32, (ts, 1), 0)
    count = jnp.minimum(pos + 1, win).astype(F32)
    pooled = acc / count - a
    y = _dot(pooled.astype(BF16), w_ref[...]) * scale_ref[...]
    o_ref[0] = y.astype(o_ref.dtype)
    buf_ref[0:POOL_HALO, :] = a[ts - POOL_HALO:, :]


def _pool_mixer(za, pool_bd, pool_scale):
    b, s, w = za.shape
    ts = min(SEQ_TILE, s)
    return pl.pallas_call(
        _pool_kernel,
        grid=(b, s // ts),
        in_specs=[pl.BlockSpec((1, ts, w), lambda bi, i: (bi, i, 0)),
                  pl.BlockSpec((w, w), lambda bi, i: (0, 0)),
                  pl.BlockSpec((1, w), lambda bi, i: (0, 0))],
        out_specs=pl.BlockSpec((1, ts, w), lambda bi, i: (bi, i, 0)),
        out_shape=jax.ShapeDtypeStruct((b, s, w), BF16),
        scratch_shapes=[pltpu.VMEM((POOL_HALO + ts, w), F32)],
        compiler_params=_cparams(("parallel", "arbitrary")),
        name="pool_mixer",
    )(za, pool_bd, pool_scale.reshape(1, w))


def _block_diag(blocks):
    g, n, _ = blocks.shape
    eye = jnp.eye(g, dtype=blocks.dtype)
    return (eye[:, None, :, None] * blocks[:, :, None, :]).reshape(g * n, g * n)


def _head_ones(width, head):
    idx = np.arange(width) // head
    return jnp.asarray((idx[:, None] == idx[None, :]).astype(np.float32), dtype=BF16)


HGRN_LEVELS = (32, 16, 8, 4, 2, 1)
HGRN_MATMUL_LEVELS = (2, 1)
N_LEVEL_MASKS = len(HGRN_LEVELS) + 1
HGRN_SEQS = 4
ATT_GROUP = 8
ROUTER_SUBTILES = 4


def _hgrn_constants():
    c = HGRN_CHUNK
    m = np.zeros((len(HGRN_MATMUL_LEVELS) + 1, c, c), np.float32)
    masks = np.zeros((N_LEVEL_MASKS, c, c), np.float32)
    for lvl, n in enumerate(HGRN_LEVELS):
        for t in range(c):
            blk = t // (2 * n)
            mid = blk * 2 * n + n
            if t >= mid:
                masks[lvl, t, blk * 2 * n:mid] = 1.0
            if n in HGRN_MATMUL_LEVELS:
                row = m[HGRN_MATMUL_LEVELS.index(n), t]
                if t >= mid:
                    row[mid:t + 1] = 1.0
                else:
                    row[t + 1:mid] = 1.0
    for t in range(c):
        m[-1, t, :t + 1] = 1.0
        masks[-1, t, t] = 1.0
    m = m.reshape(-1, c)
    m3 = np.concatenate([m, m, m], axis=1)
    masks = np.tile(masks, (1, HGRN_HEADS, 1))
    return jnp.asarray(m3, dtype=BF16), jnp.asarray(masks, dtype=F32)


def _hgrn_chunks(units, bd, m3, lmask_ref):
    c = HGRN_CHUNK
    w = HGRN_WIDTH
    n_fine = len(HGRN_MATMUL_LEVELS)
    lane_head = lax.broadcasted_iota(jnp.int32, (1, w), 1) // HGRN_KDIM
    sums = [_dot(m3, u[2]) for u in units]
    bs = [s[n_fine * c:] for s in sums]

    def level_decay(u, n):
        if n in HGRN_MATMUL_LEVELS:
            k = HGRN_MATMUL_LEVELS.index(n)
            return jnp.exp(sums[u][k * c:(k + 1) * c])
        blocks = bs[u].reshape(c // (2 * n), 2 * n, w)
        ref = blocks[:, n - 1:n, :]
        right = lax.broadcasted_iota(jnp.int32, (1, 2 * n, 1), 1) >= n
        return jnp.exp(jnp.where(right, blocks - ref, ref - blocks).reshape(c, w))

    scores = [jnp.zeros((HGRN_HEADS * c, c), F32) for _ in units]
    for lvl in range(N_LEVEL_MASKS):
        for u, (qf, kk, _, _, _, _) in enumerate(units):
            if lvl < len(HGRN_LEVELS):
                e = level_decay(u, HGRN_LEVELS[lvl])
                ql = (qf * e).astype(BF16)
                kl = (kk * e).astype(BF16)
            else:
                ql = qf.astype(BF16)
                kl = kk.astype(BF16)
            zero = jnp.zeros_like(ql)
            qs = jnp.concatenate([jnp.where(lane_head == h, ql, zero) for h in range(HGRN_HEADS)], axis=0)
            scores[u] = scores[u] + _dot_nt(qs, kl) * lmask_ref[lvl]
    rs = [_dot(scores[u].astype(BF16), unit[3]) for u, unit in enumerate(units)]
    inters = [_dot_nt((unit[0] * jnp.exp(bs[u])).astype(BF16), unit[5].astype(BF16)) for u, unit in enumerate(units)]
    upds = [_dot_tn(unit[3], (unit[1] * jnp.exp(bs[u][c - 1:c, :] - bs[u])).astype(BF16))
            for u, unit in enumerate(units)]
    outs = []
    for u in range(len(units)):
        o = inters[u]
        for h in range(HGRN_HEADS):
            o = o + jnp.where(lane_head == h, rs[u][h * c:(h + 1) * c], 0.0)
        outs.append(o)
    sss = [_head_sumsq(o, bd) for o in outs]
    res = []
    for u, unit in enumerate(units):
        st = unit[5] * jnp.exp(bs[u][c - 1:c, :]) + jnp.where(bd > 0, upds[u], 0.0)
        y = outs[u] * lax.rsqrt(sss[u] * (1.0 / HGRN_KDIM) + EPS) * unit[4]
        res.append((y, st))
    return res


def _hgrn_kernel(zh_ref, lb_ref, gw_ref, m3_ref, lmask_ref, bd_ref, o_ref, st_ref):
    i = pl.program_id(1)
    nb, ts = zh_ref.shape[0], zh_ref.shape[1]
    c = HGRN_CHUNK
    w = HGRN_WIDTH

    @pl.when(i == 0)
    def _():
        st_ref[...] = jnp.zeros(st_ref.shape, F32)

    one_minus_lb = 1.0 - lb_ref[...]
    gw = gw_ref[...]
    bd = bd_ref[...]
    m3 = m3_ref[...]

    def chunk(ci, carry):
        rows = pl.ds(pl.multiple_of(ci * c, c), c)
        units = []
        for s in range(nb):
            q, f, v, g = [zh_ref[s, rows, j * w:(j + 1) * w] for j in range(4)]
            kk = one_minus_lb * _sigmoid(-f)
            lf = jnp.log(1.0 - jnp.minimum(kk, MAX_ONE_MINUS_F))
            hi = lf.astype(BF16)
            r1 = lf - hi.astype(F32)
            mid = r1.astype(BF16)
            lo = (r1 - mid.astype(F32)).astype(BF16)
            units.append((_silu(q), kk, jnp.concatenate([hi, mid, lo], axis=0), v.astype(BF16), gw * _silu(g),
                          st_ref[s]))
        for s, (y, st) in enumerate(_hgrn_chunks(units, bd, m3, lmask_ref)):
            st_ref[s] = st
            o_ref[s, rows, :] = y.astype(o_ref.dtype)
        return carry

    lax.fori_loop(0, ts // c, chunk, 0)


def _hgrn_mixer(zh, lb, norm_w, consts):
    b, s, _ = zh.shape
    w = HGRN_WIDTH
    ts = min(SEQ_TILE, s)
    nb = HGRN_SEQS if b % HGRN_SEQS == 0 else 1
    m3, lmask, bd = consts
    full = lambda a: pl.BlockSpec(a.shape, lambda bi, i: (0,) * a.ndim)
    return pl.pallas_call(
        _hgrn_kernel,
        grid=(b // nb, s // ts),
        in_specs=[pl.BlockSpec((nb, ts, 4 * w), lambda bi, i: (bi, i, 0)),
                  pl.BlockSpec((1, w), lambda bi, i: (0, 0)),
                  pl.BlockSpec((1, w), lambda bi, i: (0, 0)),
                  full(m3), full(lmask), full(bd)],
        out_specs=pl.BlockSpec((nb, ts, w), lambda bi, i: (bi, i, 0)),
        out_shape=jax.ShapeDtypeStruct((b, s, w), BF16),
        scratch_shapes=[pltpu.VMEM((nb, w, w), F32)],
        compiler_params=_cparams(("parallel", "arbitrary")),
        name="hgrn_mixer",
    )(zh, lb.reshape(1, w), norm_w.reshape(1, w), m3, lmask, bd)


def _attn_bias():
    qi = np.arange(WINDOW)[:, None]
    kj = np.arange(2 * WINDOW)[None, :]
    dist = qi + WINDOW - kj
    valid = (dist >= 0) & (dist < WINDOW)
    slopes = np.exp2(-8.0 * np.arange(1, ATT_HEADS + 1) / ATT_HEADS)
    bias = np.where(valid[None], -slopes[:, None, None] * dist[None], NEG)
    return jnp.asarray(bias, dtype=F32)


def _attn_kernel(sink_ref, zq_ref, zkv_ref, qw_ref, kw_ref, bias_ref, bdq_ref, bdk_ref, o_ref,
                 qbuf, kbuf, vbuf):
    i = pl.program_id(1)
    ts = zq_ref.shape[1]
    hw = 2 * HEAD_DIM

    @pl.when(i == 0)
    def _():
        kbuf[:, 0:WINDOW, :] = jnp.zeros((4, WINDOW, hw), BF16)
        vbuf[:, 0:WINDOW, :] = jnp.zeros((4, WINDOW, hw), BF16)

    q = zq_ref[0]
    ssq = _head_sumsq(q, bdq_ref[...])
    qbuf[...] = (q * lax.rsqrt(ssq * (1.0 / HEAD_DIM) + EPS) * (qw_ref[...] * HEAD_DIM ** -0.5)).astype(BF16)
    kv = zkv_ref[0]
    k = kv[:, :KV_WIDTH]
    v = kv[:, KV_WIDTH:]
    ssk = _head_sumsq(k, bdk_ref[...])
    kn = k * lax.rsqrt(ssk * (1.0 / HEAD_DIM) + EPS) * kw_ref[...]
    kr = pltpu.roll(kn, HEAD_DIM, 1)
    vr = pltpu.roll(v, HEAD_DIM, 1)
    low = lax.broadcasted_iota(jnp.int32, (1, hw), 1) < HEAD_DIM
    for j in range(ATT_KV_HEADS):
        for half in range(2):
            keep = low if half == 0 else jnp.logical_not(low)
            ksrc = kn if j == half else kr
            vsrc = v if j == half else vr
            kbuf[2 * j + half, WINDOW:WINDOW + ts, :] = jnp.where(keep, ksrc, 0.0).astype(BF16)
            vbuf[2 * j + half, WINDOW:WINDOW + ts, :] = jnp.where(keep, vsrc, 0.0).astype(BF16)

    prev_cols = lax.broadcasted_iota(jnp.int32, (1, 2 * WINDOW), 1) < WINDOW

    def block(n, carry):
        r0 = pl.multiple_of(n * WINDOW, WINDOW)
        no_prev = jnp.logical_and(i == 0, n == 0)
        pen = jnp.where(jnp.logical_and(prev_cols, no_prev), NEG, 0.0)
        for g0 in range(0, ATT_HEADS, ATT_GROUP):
            heads = range(g0, g0 + ATT_GROUP)
            logits = []
            for h in heads:
                hp, half = h // 2, h % 2
                j = h // (ATT_HEADS // ATT_KV_HEADS)
                qp = qbuf[pl.ds(r0, WINDOW), hp * hw:(hp + 1) * hw]
                keys = kbuf[2 * j + half, pl.ds(r0, 2 * WINDOW), :]
                logits.append(_dot_nt(qp, keys) + bias_ref[h] + pen)
            ps, scales = [], []
            for h, lg in zip(heads, logits):
                sink = sink_ref[h]
                m = jnp.maximum(jnp.max(lg, axis=-1, keepdims=True), sink)
                p = jnp.exp(lg - m)
                scales.append(1.0 / (jnp.sum(p, axis=-1, keepdims=True) + jnp.exp(sink - m)))
                ps.append(p.astype(BF16))
            outs = []
            for h, p in zip(heads, ps):
                half = h % 2
                j = h // (ATT_HEADS // ATT_KV_HEADS)
                vals = vbuf[2 * j + half, pl.ds(r0, 2 * WINDOW), :]
                outs.append(_dot(p, vals))
            for k in range(0, ATT_GROUP, 2):
                hp = (g0 + k) // 2
                acc = outs[k] * scales[k] + outs[k + 1] * scales[k + 1]
                o_ref[0, pl.ds(r0, WINDOW), hp * hw:(hp + 1) * hw] = acc.astype(o_ref.dtype)
        return carry

    lax.fori_loop(0, ts // WINDOW, block, 0)
    kbuf[:, 0:WINDOW, :] = kbuf[:, ts:ts + WINDOW, :]
    vbuf[:, 0:WINDOW, :] = vbuf[:, ts:ts + WINDOW, :]


def _attn_mixer(zq, zkv, q_norm_w, k_norm_w, sinks, consts):
    b, s, _ = zq.shape
    ts = min(SEQ_TILE, s)
    bias, bdq, bdk = consts
    hw = 2 * HEAD_DIM
    qw = jnp.tile(q_norm_w, ATT_HEADS).reshape(1, ATT_WIDTH)
    kw = jnp.tile(k_norm_w, ATT_KV_HEADS).reshape(1, KV_WIDTH)
    full = lambda a: pl.BlockSpec(a.shape, lambda bi, i, sk: (0,) * a.ndim)
    grid_spec = pltpu.PrefetchScalarGridSpec(
        num_scalar_prefetch=1,
        grid=(b, s // ts),
        in_specs=[pl.BlockSpec((1, ts, ATT_WIDTH), lambda bi, i, sk: (bi, i, 0)),
                  pl.BlockSpec((1, ts, 2 * KV_WIDTH), lambda bi, i, sk: (bi, i, 0)),
                  full(qw), full(kw), full(bias), full(bdq), full(bdk)],
        out_specs=pl.BlockSpec((1, ts, ATT_WIDTH), lambda bi, i, sk: (bi, i, 0)),
        scratch_shapes=[pltpu.VMEM((ts, ATT_WIDTH), BF16),
                        pltpu.VMEM((4, WINDOW + ts, hw), BF16),
                        pltpu.VMEM((4, WINDOW + ts, hw), BF16)],
    )
    return pl.pallas_call(
        _attn_kernel,
        grid_spec=grid_spec,
        out_shape=jax.ShapeDtypeStruct((b, s, ATT_WIDTH), BF16),
        compiler_params=_cparams(("parallel", "arbitrary")),
        name="swa_mixer",
    )(sinks, zq, zkv, qw, kw, bias, bdq, bdk)


def _router_constants(tm):
    t = np.arange(tm)
    before = (t[:, None] < t[None, :]).astype(np.float32)
    e = np.arange(N_EXPERTS)
    lower = (e[None, :] < e[:, None]).astype(np.float32)
    return (jnp.asarray(before, dtype=BF16), jnp.asarray(np.ones((tm, tm), np.float32), dtype=BF16),
            jnp.asarray(lower, dtype=BF16))


def _route(sel, scores):
    tm = sel.shape[1]
    group_scores = []
    for g in range(N_GROUPS):
        rows = [sel[g * EXPERTS_PER_GROUP + a:g * EXPERTS_PER_GROUP + a + 1] for a in range(EXPERTS_PER_GROUP)]
        best_pair = None
        for a in range(EXPERTS_PER_GROUP):
            for bb in range(a + 1, EXPERTS_PER_GROUP):
                pair = rows[a] + rows[bb]
                best_pair = pair if best_pair is None else jnp.maximum(best_pair, pair)
        group_scores.append(best_pair)
    top = functools.reduce(jnp.maximum, group_scores)
    best = jnp.full((1, tm), N_GROUPS - 1, jnp.int32)
    for g in reversed(range(N_GROUPS - 1)):
        best = jnp.where(group_scores[g] == top, g, best)
    row = lax.broadcasted_iota(jnp.int32, (N_EXPERTS, tm), 0)
    cand = jnp.where(row // EXPERTS_PER_GROUP == best, sel, NEG)
    m1 = jnp.max(cand, axis=0, keepdims=True)
    i1 = jnp.min(jnp.where(cand == m1, row, N_EXPERTS), axis=0, keepdims=True)
    oh1 = row == i1
    cand = jnp.where(oh1, NEG, cand)
    m2 = jnp.max(cand, axis=0, keepdims=True)
    i2 = jnp.min(jnp.where(cand == m2, row, N_EXPERTS), axis=0, keepdims=True)
    oh2 = row == i2
    s1 = jnp.sum(jnp.where(oh1, scores, 0.0), axis=0, keepdims=True)
    s2 = jnp.sum(jnp.where(oh2, scores, 0.0), axis=0, keepdims=True)
    return oh1, oh2, s1 / (s1 + s2), s2 / (s1 + s2)


def _out_kernel(yp_ref, yh_ref, ya_ref, x_ref, g1_ref, sc_ref, sh_ref, nw_ref, wo_ref, rwt_ref, rb_ref,
                before_ref, ones_ref, lower_ref, x1_ref, h2_ref, ls_ref, gate_ref, cnt_ref):
    tm = MOE_TOK_TILE
    subs = [pl.ds(u * tm, tm) for u in range(x_ref.shape[0] // tm)]
    p0, p1 = POOL_WIDTH, POOL_WIDTH + HGRN_WIDTH
    h2s = []
    for rows in subs:
        mix = (_dot(yp_ref[rows, :], wo_ref[0:p0, :]) + _dot(yh_ref[rows, :], wo_ref[p0:p1, :])
               + _dot(ya_ref[rows, :], wo_ref[p1:, :]))
        x1 = x_ref[rows, :] + g1_ref[0] * mix
        x1_ref[rows, :] = x1
        ms = jnp.mean(x1 * x1, axis=-1, keepdims=True)
        h2 = ((x1 * lax.rsqrt(ms + EPS) * nw_ref[...]) * (1.0 + sc_ref[0]) + sh_ref[0]).astype(BF16)
        h2_ref[rows, :] = h2
        h2s.append(h2)
    logits = [_dot_nt(rwt_ref[...], h2) for h2 in h2s]
    picks = []
    for lg in logits:
        ex = jnp.exp(lg - jnp.max(lg, axis=0, keepdims=True))
        scores = ex / jnp.sum(ex, axis=0, keepdims=True)
        picks.append(_route(scores + rb_ref[...], scores))
    chosen = [jnp.where(jnp.logical_or(oh1, oh2), 1.0, 0.0).astype(BF16) for oh1, oh2, _, _ in picks]
    ranks = [_dot(ch, before_ref[...]) for ch in chosen]
    counts = [_dot(ch, ones_ref[...]) for ch in chosen]
    aligned = [(jnp.floor((cn + (ROW_ALIGN - 1)) * (1.0 / ROW_ALIGN)) * ROW_ALIGN).astype(BF16) for cn in counts]
    slots = [_dot(lower_ref[...], al) + rk for al, rk in zip(aligned, ranks)]
    for u, rows in enumerate(subs):
        oh1, oh2, w1, w2 = picks[u]
        gate_ref[:, rows] = jnp.concatenate([w1, w2], axis=0)
        ls_ref[:, rows] = jnp.concatenate([jnp.sum(jnp.where(oh1, slots[u], 0.0), axis=0, keepdims=True),
                                           jnp.sum(jnp.where(oh2, slots[u], 0.0), axis=0, keepdims=True)], axis=0)
        cnt_ref[u] = counts[u][:, 0:128]


def _out_proj_router(yp, yh, ya, x, g1, sc, sh, nw, wo_b, layer, rwt_b, rb, consts):
    b, s, d = x.shape
    t = b * s
    tm = MOE_TOK_TILE
    n_sub = ROUTER_SUBTILES if s % (ROUTER_SUBTILES * tm) == 0 else 1
    ts = n_sub * tm
    per_batch = s // ts
    nt = t // tm
    before, ones, lower = consts
    tok = lambda w: pl.BlockSpec((ts, w), lambda i: (i, 0))
    vec = pl.BlockSpec((1, 1, d), lambda i: (i // per_batch, 0, 0))
    full = lambda a: pl.BlockSpec(a.shape, lambda i: (0,) * a.ndim)
    lanes = pl.BlockSpec((2, ts), lambda i: (0, i))
    return pl.pallas_call(
        _out_kernel,
        grid=(t // ts,),
        in_specs=[tok(POOL_WIDTH), tok(HGRN_WIDTH), tok(ATT_WIDTH), tok(d), vec, vec, vec,
                  pl.BlockSpec((1, d), lambda i: (0, 0)),
                  pl.BlockSpec((None,) + wo_b.shape[1:], lambda i: (layer, 0, 0)), full(rwt_b),
                  pl.BlockSpec((N_EXPERTS, 1), lambda i: (0, 0)), full(before), full(ones), full(lower)],
        out_specs=[tok(d), tok(d), lanes, lanes, pl.BlockSpec((n_sub, N_EXPERTS, 128), lambda i: (i, 0, 0))],
        out_shape=[jax.ShapeDtypeStruct((t, d), F32), jax.ShapeDtypeStruct((t, d), BF16),
                   jax.ShapeDtypeStruct((2, t), F32), jax.ShapeDtypeStruct((2, t), F32),
                   jax.ShapeDtypeStruct((nt, N_EXPERTS, 128), F32)],
        compiler_params=_cparams(("parallel",)),
        name="out_proj_router",
    )(yp.reshape(t, -1), yh.reshape(t, -1), ya.reshape(t, -1), x.reshape(t, d),
      g1.reshape(b, 1, d), sc.reshape(b, 1, d), sh.reshape(b, 1, d), nw.reshape(1, d), wo_b, rwt_b,
      rb.reshape(N_EXPERTS, 1), before, ones, lower)


def _moe_tables(cnt_out):
    cnt = jnp.round(cnt_out[:, :, 0]).astype(jnp.int32)
    cnt = (cnt + ROW_ALIGN - 1) // ROW_ALIGN * ROW_ALIGN
    total = jnp.sum(cnt, axis=0)
    padded = (total + EXPERT_TILE - 1) // EXPERT_TILE * EXPERT_TILE
    ends = jnp.cumsum(padded)
    first = ends - padded
    start = first[None, :] + jnp.cumsum(cnt, axis=0) - cnt
    loff = jnp.cumsum(cnt, axis=1) - cnt
    n_tiles = _moe_rows(cnt.shape[0] * MOE_TOK_TILE) // EXPERT_TILE
    n_used = ends[-1] // EXPERT_TILE
    tile_row = jnp.minimum(jnp.arange(n_tiles), n_used - 1) * EXPERT_TILE
    tile_expert = jnp.sum((ends[None, :] <= tile_row[:, None]).astype(jnp.int32), axis=1)
    pieces = jnp.sum(cnt, axis=1) // ROW_ALIGN
    row = jnp.arange(MAX_PIECES, dtype=jnp.int32) * ROW_ALIGN
    owner = jnp.sum(((loff + cnt)[:, None, :] <= row[None, :, None]).astype(jnp.int32), axis=2)
    owner = jnp.minimum(owner, N_EXPERTS - 1)
    dest = jnp.take_along_axis(start - loff, owner, axis=1) + row[None, :]
    tail = jnp.concatenate([first + total, padded - total, jnp.stack([ends[-1], n_tiles - n_used])])
    return ((dest.reshape(-1).astype(jnp.int32), pieces.astype(jnp.int32)),
            tail.astype(jnp.int32), tile_expert.astype(jnp.int32), n_used.reshape(1).astype(jnp.int32))


def _moe_rows(n_tokens):
    per_tile = 2 * MOE_TOK_TILE + N_EXPERTS * (ROW_ALIGN - 1)
    rows = (n_tokens // MOE_TOK_TILE) * per_tile + N_EXPERTS * (EXPERT_TILE - ROW_ALIGN)
    return (rows + EXPERT_TILE - 1) // EXPERT_TILE * EXPERT_TILE


def _chunk_copies(count, make_copy, wait=False):
    for size in CHUNK_SIZES:
        offset = jnp.bitwise_and(count, ~(2 * size - 1))

        @pl.when(jnp.bitwise_and(count, size) != 0)
        def _():
            copy = make_copy(pl.multiple_of(offset, ROW_ALIGN), size)
            copy.wait() if wait else copy.start()


def _wait_rows(n_pieces, make_copy):
    for p in WAIT_PIECES:
        @pl.when(jnp.bitwise_and(n_pieces, p) != 0)
        def _():
            make_copy(p * ROW_ALIGN).wait()


def _slot_rows(tm):
    return lax.broadcasted_iota(jnp.int32, (MOE_SLOTS, tm), 0).astype(F32)


def _sort_kernel(dest_ref, pieces_ref, tail_ref, h2_ref, ls_ref, xs_ref, buf_ref, sem_ref):
    i = pl.program_id(0)
    n = pl.num_programs(0)
    tm = h2_ref.shape[0]
    slot = i % 2

    def sent(tile, s):
        _wait_rows(pieces_ref[tile], lambda rows: pltpu.make_async_copy(
            buf_ref.at[s, pl.ds(0, rows), :], xs_ref.at[pl.ds(0, rows), :], sem_ref.at[s]))

    @pl.when(i >= 2)
    def _():
        sent(i - 2, slot)

    ls = ls_ref[...]
    srow = _slot_rows(tm)
    perm = jnp.where(jnp.logical_or(srow == ls[0:1], srow == ls[1:2]), 1.0, 0.0).astype(BF16)
    buf_ref[slot] = _dot(perm, h2_ref[...]).astype(BF16)

    def send_piece(j, carry):
        src = pl.multiple_of(j * ROW_ALIGN, ROW_ALIGN)
        dst = pl.multiple_of(dest_ref[i * MAX_PIECES + j], ROW_ALIGN)
        pltpu.make_async_copy(buf_ref.at[slot, pl.ds(src, ROW_ALIGN), :], xs_ref.at[pl.ds(dst, ROW_ALIGN), :],
                              sem_ref.at[slot]).start()
        return carry

    lax.fori_loop(0, pieces_ref[i], send_piece, 0)

    @pl.when(i == n - 1)
    def _():
        @pl.when(n > 1)
        def _():
            sent(i - 1, 1 - slot)
        sent(i, slot)

        buf_ref[0] = jnp.zeros(buf_ref.shape[1:], BF16)
        zeros_to = lambda row, size: pltpu.make_async_copy(
            buf_ref.at[0, pl.ds(0, size), :], xs_ref.at[pl.ds(row, size), :], sem_ref.at[0])

        for wait in (False, True):
            def expert_tail(e, carry):
                st = pl.multiple_of(tail_ref[e], ROW_ALIGN)
                _chunk_copies(tail_ref[N_EXPERTS + e], lambda off, size: zeros_to(st + off, size), wait)
                return carry

            def free_tile(j, carry):
                copy = zeros_to(pl.multiple_of(tail_ref[2 * N_EXPERTS] + j * EXPERT_TILE, ROW_ALIGN), EXPERT_TILE)
                copy.wait() if wait else copy.start()
                return carry

            lax.fori_loop(0, N_EXPERTS, expert_tail, 0)
            lax.fori_loop(0, tail_ref[2 * N_EXPERTS + 1], free_tile, 0)


def _sort_tokens(h2, ls, tables, tail, n_rows):
    t, d = h2.shape
    tm = MOE_TOK_TILE
    grid_spec = pltpu.PrefetchScalarGridSpec(
        num_scalar_prefetch=3,
        grid=(t // tm,),
        in_specs=[pl.BlockSpec((tm, d), lambda i, *_: (i, 0)),
                  pl.BlockSpec((2, tm), lambda i, *_: (0, i))],
        out_specs=pl.BlockSpec(memory_space=pl.ANY),
        scratch_shapes=[pltpu.VMEM((2, MOE_SLOTS, d), BF16), pltpu.SemaphoreType.DMA((2,))],
    )
    return pl.pallas_call(
        _sort_kernel,
        grid_spec=grid_spec,
        out_shape=jax.ShapeDtypeStruct((n_rows, d), BF16),
        compiler_params=_cparams(("arbitrary",)),
        name="moe_sort",
    )(*tables, tail, h2, ls)


def _gmm_kernel(te_ref, nu_ref, xs_ref, wg_ref, wu_ref, wd_ref, ys_ref, wgb, wub, wdb):
    i = pl.program_id(0)
    used = i < nu_ref[0]
    new_expert = jnp.logical_or(i == 0, te_ref[i] != te_ref[jnp.maximum(i - 1, 0)])

    @pl.when(jnp.logical_and(used, new_expert))
    def _():
        wgb[...] = wg_ref[...].astype(BF16)
        wub[...] = wu_ref[...].astype(BF16)
        wdb[...] = wd_ref[...].astype(BF16)

    @pl.when(used)
    def _():
        x = xs_ref[...]
        he = _silu(_dot(x, wgb[...])) * _dot(x, wub[...])
        ys_ref[...] = _dot(he.astype(BF16), wdb[...]).astype(BF16)

    @pl.when(jnp.logical_not(used))
    def _():
        ys_ref[...] = jnp.zeros(ys_ref.shape, BF16)


def _grouped_mlp(xs, tile_expert, n_used, wg, wu, wd, layer):
    n_rows, d = xs.shape
    te = EXPERT_TILE
    row_map = lambda i, tex, nu: (jnp.minimum(i, nu[0] - 1), 0)
    grid_spec = pltpu.PrefetchScalarGridSpec(
        num_scalar_prefetch=2,
        grid=(n_rows // te,),
        in_specs=[pl.BlockSpec((te, d), row_map),
                  pl.BlockSpec((None, None, d, D_EXPERT), lambda i, tex, nu: (layer, tex[i], 0, 0)),
                  pl.BlockSpec((None, None, d, D_EXPERT), lambda i, tex, nu: (layer, tex[i], 0, 0)),
                  pl.BlockSpec((None, None, D_EXPERT, d), lambda i, tex, nu: (layer, tex[i], 0, 0))],
        out_specs=pl.BlockSpec((te, d), lambda i, tex, nu: (i, 0)),
        scratch_shapes=[pltpu.VMEM((d, D_EXPERT), BF16), pltpu.VMEM((d, D_EXPERT), BF16),
                        pltpu.VMEM((D_EXPERT, d), BF16)],
    )
    return pl.pallas_call(
        _gmm_kernel,
        grid_spec=grid_spec,
        out_shape=jax.ShapeDtypeStruct((n_rows, d), BF16),
        compiler_params=_cparams(("arbitrary",)),
        name="moe_grouped_mlp",
    )(tile_expert, n_used, xs, wg, wu, wd)


def _combine_kernel(dest_ref, pieces_ref, ls_ref, gate_ref, x1_ref, g2_ref, ys_ref, o_ref, buf_ref, sem_ref):
    i = pl.program_id(0)
    n = pl.num_programs(0)
    tm = x1_ref.shape[0]
    slot = i % 2

    def fetch(tile, s):
        def fetch_piece(j, carry):
            src = pl.multiple_of(dest_ref[tile * MAX_PIECES + j], ROW_ALIGN)
            dst = pl.multiple_of(j * ROW_ALIGN, ROW_ALIGN)
            pltpu.make_async_copy(ys_ref.at[pl.ds(src, ROW_ALIGN), :], buf_ref.at[s, pl.ds(dst, ROW_ALIGN), :],
                                  sem_ref.at[s]).start()
            return carry

        lax.fori_loop(0, pieces_ref[tile], fetch_piece, 0)

    @pl.when(i == 0)
    def _():
        fetch(0, 0)

    @pl.when(i + 1 < n)
    def _():
        fetch(i + 1, 1 - slot)

    _wait_rows(pieces_ref[i], lambda rows: pltpu.make_async_copy(
        ys_ref.at[pl.ds(0, rows), :], buf_ref.at[slot, pl.ds(0, rows), :], sem_ref.at[slot]))

    srow = _slot_rows(tm)
    ls = ls_ref[...]
    gate = gate_ref[...]
    weights = (jnp.where(srow == ls[0:1], gate[0:1], 0.0)
               + jnp.where(srow == ls[1:2], gate[1:2], 0.0)).astype(BF16)
    written = lax.broadcasted_iota(jnp.int32, (MOE_SLOTS, 1), 0) < pieces_ref[i] * ROW_ALIGN
    rows = jnp.where(written, buf_ref[slot], jnp.zeros((), BF16))
    y = _dot_tn(weights, rows)
    o_ref[...] = x1_ref[...] + g2_ref[0] * y


def _combine(ys, ls, gate, x1, g2, tables, per_batch):
    t, d = x1.shape
    tm = MOE_TOK_TILE
    grid_spec = pltpu.PrefetchScalarGridSpec(
        num_scalar_prefetch=2,
        grid=(t // tm,),
        in_specs=[pl.BlockSpec((2, tm), lambda i, *_: (0, i)),
                  pl.BlockSpec((2, tm), lambda i, *_: (0, i)),
                  pl.BlockSpec((tm, d), lambda i, *_: (i, 0)),
                  pl.BlockSpec((1, 1, d), lambda i, *_: (i // per_batch, 0, 0)),
                  pl.BlockSpec(memory_space=pl.ANY)],
        out_specs=pl.BlockSpec((tm, d), lambda i, *_: (i, 0)),
        scratch_shapes=[pltpu.VMEM((2, MOE_SLOTS, d), BF16), pltpu.SemaphoreType.DMA((2,))],
    )
    return pl.pallas_call(
        _combine_kernel,
        grid_spec=grid_spec,
        out_shape=jax.ShapeDtypeStruct((t, d), F32),
        compiler_params=_cparams(("arbitrary",)),
        name="moe_combine",
    )(*tables, ls, gate, x1, g2.reshape(-1, 1, d), ys)


def kernel(x, c, ada_w, ada_b, norm1_w, norm2_w, w_in, pool_w, pool_scale, hgrn_lb_raw, hgrn_norm_w, q_norm_w,
           k_norm_w, attn_sinks, w_out, router_w, router_bias, expert_w_gate, expert_w_up, expert_w_down):
    b, s, d = x.shape
    depth = ada_w.shape[0]
    t = b * s
    per_batch = s // MOE_TOK_TILE
    n_rows = _moe_rows(t)

    p = jax.nn.softmax(hgrn_lb_raw.astype(F32), axis=0)
    lower_bounds = jnp.maximum(jnp.cumsum(p, axis=0) - p[0:1], 0.0)

    mod = _modulation(c, ada_w, ada_b)
    hgrn_consts = _hgrn_constants() + (_head_ones(HGRN_WIDTH, HGRN_KDIM),)
    attn_consts = (_attn_bias(), _head_ones(ATT_WIDTH, HEAD_DIM), _head_ones(KV_WIDTH, HEAD_DIM))
    router_consts = _router_constants(MOE_TOK_TILE)
    rwt_b = router_w.T.astype(BF16)
    w_in_b = w_in.astype(BF16)
    w_out_b = w_out.astype(BF16)

    for l in range(depth):
        sh1, sc1, g1, sh2, sc2, g2 = [mod[l, :, j * d:(j + 1) * d] for j in range(6)]
        za, zh, zq, zkv = _in_proj(x, sc1, sh1, norm1_w[l], w_in_b, l)
        yp = _pool_mixer(za, _block_diag(pool_w[l]).astype(BF16), pool_scale[l])
        yh = _hgrn_mixer(zh, lower_bounds[l], hgrn_norm_w[l], hgrn_consts)
        ya = _attn_mixer(zq, zkv, q_norm_w[l], k_norm_w[l], attn_sinks[l], attn_consts)
        x1, h2, ls, gate, cnt_out = _out_proj_router(
            yp, yh, ya, x, g1, sc2, sh2, norm2_w[l], w_out_b, l, rwt_b, router_bias, router_consts)
        tables, tail, tile_expert, n_used = _moe_tables(cnt_out)
        xs = _sort_tokens(h2, ls, tables, tail, n_rows)
        ys = _grouped_mlp(xs, tile_expert, n_used, expert_w_gate, expert_w_up, expert_w_down, l)
        x = _combine(ys, ls, gate, x1, g2, tables, per_batch).reshape(b, s, d)
    return x
```

```python
import functools

import numpy as np
import jax
import jax.numpy as jnp
from jax import lax
from jax.experimental import pallas as pl
from jax.experimental.pallas import tpu as pltpu

F32 = jnp.float32
BF16 = jnp.bfloat16

D_MODEL = 1024
POOL_WINDOWS = (2, 4, 8, 16)
POOL_WIDTH = 256
POOL_GROUP = 64
POOL_HALO = 32
HGRN_HEADS = 4
HGRN_KDIM = 64
HGRN_WIDTH = 256
HGRN_CHUNK = 64
ATT_HEADS = 8
ATT_KV_HEADS = 2
HEAD_DIM = 64
ATT_WIDTH = 512
KV_WIDTH = 128
WINDOW = 128
N_EXPERTS = 16
N_GROUPS = 4
EXPERTS_PER_GROUP = 4
D_EXPERT = 512
EPS = 1e-6
MAX_ONE_MINUS_F = 1.0 - 1e-6
LOG2E = 1.4426950408889634
NEG = -1e30

VMEM_LIMIT = 48 * 1024 * 1024

SEQ_TILE = 512
MOE_TOK_TILE = 256
ROW_ALIGN = 16
MOE_SLOTS = 768
EXPERT_TILE = 512
CHUNK_SIZES = (256, 128, 64, 32, 16)
MAX_PIECES = MOE_SLOTS // ROW_ALIGN
WAIT_PIECES = (32, 16, 8, 4, 2, 1)


def _sigmoid(x):
    return 1.0 / (1.0 + jnp.exp(-x))


def _silu(x):
    return x * _sigmoid(x)


def _cparams(sem, **kw):
    return pltpu.CompilerParams(dimension_semantics=sem, vmem_limit_bytes=VMEM_LIMIT, **kw)


def _dot(a, b):
    return jnp.dot(a, b, preferred_element_type=F32)


def _dot_nt(a, b):
    return lax.dot_general(a, b, (((1,), (1,)), ((), ())), preferred_element_type=F32)


def _dot_tn(a, b):
    return lax.dot_general(a, b, (((0,), (0,)), ((), ())), preferred_element_type=F32)


def _head_sumsq(x, bd):
    return _dot((x * x).astype(BF16), bd)


def _mod_kernel(c_ref, w_ref, b_ref, o_ref):
    cond = _silu(c_ref[...])
    o_ref[0] = _dot(cond.astype(BF16), w_ref[0].astype(BF16)) + b_ref[0]


def _modulation(c, ada_w, ada_b):
    depth, d, n = ada_w.shape
    b = c.shape[0]
    nb = n // d
    return pl.pallas_call(
        _mod_kernel,
        grid=(depth, nb),
        in_specs=[pl.BlockSpec((b, d), lambda l, j: (0, 0)),
                  pl.BlockSpec((1, d, d), lambda l, j: (l, 0, j)),
                  pl.BlockSpec((1, 1, d), lambda l, j: (l, 0, j))],
        out_specs=pl.BlockSpec((1, b, d), lambda l, j: (l, 0, j)),
        out_shape=jax.ShapeDtypeStruct((depth, b, n), F32),
        compiler_params=_cparams(("parallel", "parallel")),
        name="adaln_mod",
    )(c, ada_w, ada_b.reshape(depth, 1, n))


def _in_kernel(x_ref, sc_ref, sh_ref, nw_ref, w_ref, za_ref, zh_ref, zq_ref, zkv_ref):
    x = x_ref[0]
    ms = jnp.mean(x * x, axis=-1, keepdims=True)
    h = (x * lax.rsqrt(ms + EPS) * nw_ref[...]) * (1.0 + sc_ref[0]) + sh_ref[0]
    z = _dot(h.astype(BF16), w_ref[...])
    za_ref[0] = z[:, :POOL_WIDTH]
    zh_ref[0] = z[:, POOL_WIDTH:POOL_WIDTH + 4 * HGRN_WIDTH]
    zq_ref[0] = z[:, POOL_WIDTH + 4 * HGRN_WIDTH:POOL_WIDTH + 4 * HGRN_WIDTH + ATT_WIDTH]
    zkv_ref[0] = z[:, POOL_WIDTH + 4 * HGRN_WIDTH + ATT_WIDTH:]


def _in_proj(x, sc, sh, nw, w_in_b, layer):
    b, s, d = x.shape
    n = w_in_b.shape[2]
    ts = min(SEQ_TILE, s)
    tok = lambda w: pl.BlockSpec((1, ts, w), lambda bi, i: (bi, i, 0))
    vec = pl.BlockSpec((1, 1, d), lambda bi, i: (bi, 0, 0))
    widths = (POOL_WIDTH, 4 * HGRN_WIDTH, ATT_WIDTH, 2 * KV_WIDTH)
    return pl.pallas_call(
        _in_kernel,
        grid=(b, s // ts),
        in_specs=[tok(d), vec, vec,
                  pl.BlockSpec((1, d), lambda bi, i: (0, 0)),
                  pl.BlockSpec((None, d, n), lambda bi, i: (layer, 0, 0))],
        out_specs=[tok(w) for w in widths],
        out_shape=[jax.ShapeDtypeStruct((b, s, w), F32) for w in widths],
        compiler_params=_cparams(("parallel", "parallel")),
        name="in_proj",
    )(x, sc.reshape(b, 1, d), sh.reshape(b, 1, d), nw.reshape(1, d), w_in_b)


def _pool_kernel(a_ref, w_ref, scale_ref, o_ref, buf_ref, sa_ref, sb_ref):
    i = pl.program_id(1)
    ts = a_ref.shape[1]
    a = a_ref[0]
    halo = POOL_HALO
    end = halo + ts
    half = 2 * POOL_GROUP

    @pl.when(i == 0)
    def _():
        buf_ref[0:halo, :] = jnp.zeros((halo, POOL_WIDTH), F32)

    buf_ref[halo:end, :] = a
    s2 = buf_ref[8:end, :] + buf_ref[7:end - 1, :]
    sa_ref[8:end, :] = s2
    s4 = sa_ref[16:end, :] + sa_ref[14:end - 2, :]
    sb_ref[16:end, :] = s4
    s8 = sb_ref[24:end, half:] + sb_ref[20:end - 4, half:]
    sa_ref[24:end, half:] = s8
    s16 = sa_ref[halo:end, half:] + sa_ref[halo - 8:end - 8, half:]
    lane = lax.broadcasted_iota(jnp.int32, (1, POOL_WIDTH), 1)
    win = jnp.left_shift(2, lane // POOL_GROUP)
    low = lane[:, 0:half] % half < POOL_GROUP
    acc = jnp.concatenate([jnp.where(low, s2[halo - 8:, 0:half], s4[halo - 16:, 0:half]),
                           jnp.where(low, s8[halo - 24:], s16)], axis=1)
    pos = i * ts + lax.broadcasted_iota(jnp.int32, (ts, 1), 0)
    count = jnp.minimum(pos + 1, win).astype(F32)
    pooled = acc / count - a
    y = _dot(pooled.astype(BF16), w_ref[...]) * scale_ref[...]
    o_ref[0] = y.astype(o_ref.dtype)
    buf_ref[0:halo, :] = a[ts - halo:, :]


def _pool_mixer(za, pool_bd, pool_scale):
    b, s, w = za.shape
    ts = min(SEQ_TILE, s)
    return pl.pallas_call(
        _pool_kernel,
        grid=(b, s // ts),
        in_specs=[pl.BlockSpec((1, ts, w), lambda bi, i: (bi, i, 0)),
                  pl.BlockSpec((w, w), lambda bi, i: (0, 0)),
                  pl.BlockSpec((1, w), lambda bi, i: (0, 0))],
        out_specs=pl.BlockSpec((1, ts, w), lambda bi, i: (bi, i, 0)),
        out_shape=jax.ShapeDtypeStruct((b, s, w), BF16),
        scratch_shapes=[pltpu.VMEM((POOL_HALO + ts, w), F32)] * 3,
        compiler_params=_cparams(("parallel", "arbitrary")),
        name="pool_mixer",
    )(za, pool_bd, pool_scale.reshape(1, w))


def _block_diag(blocks):
    g, n, _ = blocks.shape
    eye = jnp.eye(g, dtype=blocks.dtype)
    return (eye[:, None, :, None] * blocks[:, :, None, :]).reshape(g * n, g * n)


def _head_ones(width, head):
    idx = np.arange(width) // head
    return jnp.asarray((idx[:, None] == idx[None, :]).astype(np.float32), dtype=BF16)


HGRN_LEVELS = (32, 16, 8, 4, 2, 1)
HGRN_MATMUL_LEVELS = (2, 1)
N_LEVEL_MASKS = len(HGRN_LEVELS) + 1
HGRN_SEQS = 8
HGRN_TILE = 256
ATT_GROUP = 8
ROUTER_SUBTILES = 4


def _hgrn_constants():
    c = HGRN_CHUNK
    m = np.zeros((len(HGRN_MATMUL_LEVELS) + 1, c, c), np.float32)
    masks = np.zeros((N_LEVEL_MASKS, c, c), np.float32)
    for lvl, n in enumerate(HGRN_LEVELS):
        for t in range(c):
            blk = t // (2 * n)
            mid = blk * 2 * n + n
            if t >= mid:
                masks[lvl, t, blk * 2 * n:mid] = 1.0
            if n in HGRN_MATMUL_LEVELS:
                row = m[HGRN_MATMUL_LEVELS.index(n), t]
                if t >= mid:
                    row[mid:t + 1] = 1.0
                else:
                    row[t + 1:mid] = 1.0
    for t in range(c):
        m[-1, t, :t + 1] = 1.0
        masks[-1, t, t] = 1.0
    m = m.reshape(-1, c)
    m3 = np.concatenate([m, m, m], axis=1)
    masks = np.tile(masks, (1, HGRN_HEADS, 1))
    return jnp.asarray(m3, dtype=BF16), jnp.asarray(masks, dtype=F32)


def _hgrn_chunks(units, bd, m3, lmask_ref):
    c = HGRN_CHUNK
    w = HGRN_WIDTH
    n_fine = len(HGRN_MATMUL_LEVELS)
    lane_head = lax.broadcasted_iota(jnp.int32, (1, w), 1) // HGRN_KDIM
    sums = [_dot(m3, u[2]) for u in units]
    bs = [s[n_fine * c:] for s in sums]

    def level_decay(u, n):
        if n in HGRN_MATMUL_LEVELS:
            k = HGRN_MATMUL_LEVELS.index(n)
            return jnp.exp(sums[u][k * c:(k + 1) * c])
        blocks = bs[u].reshape(c // (2 * n), 2 * n, w)
        ref = blocks[:, n - 1:n, :]
        right = lax.broadcasted_iota(jnp.int32, (1, 2 * n, 1), 1) >= n
        return jnp.exp(jnp.where(right, blocks - ref, ref - blocks).reshape(c, w))

    groups = c // 8
    parts = [[[None] * groups for _ in range(HGRN_HEADS)] for _ in units]

    def add_part(u, h, g, piece):
        parts[u][h][g] = piece if parts[u][h][g] is None else parts[u][h][g] + piece

    for lvl in range(N_LEVEL_MASKS):
        n = HGRN_LEVELS[lvl] if lvl < len(HGRN_LEVELS) else 0
        wanted = [g for g in range(groups) if (8 * g) % (2 * n) >= n] if n >= 8 else list(range(groups))
        for u, (qf, kk, _, _, _, _) in enumerate(units):
            if lvl < len(HGRN_LEVELS):
                e = level_decay(u, n)
                ql = qf * e
                kl = (kk * e).astype(BF16)
            else:
                ql = qf
                kl = kk.astype(BF16)
            ql = jnp.concatenate([ql[8 * g:8 * g + 8] for g in wanted], axis=0).astype(BF16)
            zero = jnp.zeros_like(ql)
            qs = jnp.concatenate([jnp.where(lane_head == h, ql, zero) for h in range(HGRN_HEADS)], axis=0)
            res = _dot_nt(qs, kl)
            for h in range(HGRN_HEADS):
                for k, g in enumerate(wanted):
                    r0 = (h * len(wanted) + k) * 8
                    add_part(u, h, g, res[r0:r0 + 8] * lmask_ref[lvl, h * c + 8 * g:h * c + 8 * g + 8, :])
    scores = [jnp.concatenate([parts[u][h][g] for h in range(HGRN_HEADS) for g in range(groups)], axis=0)
              for u in range(len(units))]
    rs = [_dot(scores[u].astype(BF16), unit[3]) for u, unit in enumerate(units)]
    inters = [_dot_nt((unit[0] * jnp.exp(bs[u])).astype(BF16), unit[5].astype(BF16)) for u, unit in enumerate(units)]
    upds = [_dot_tn(unit[3], (unit[1] * jnp.exp(bs[u][c - 1:c, :] - bs[u])).astype(BF16))
            for u, unit in enumerate(units)]
    outs = []
    for u in range(len(units)):
        o = inters[u]
        for h in range(HGRN_HEADS):
            o = o + jnp.where(lane_head == h, rs[u][h * c:(h + 1) * c], 0.0)
        outs.append(o)
    sss = [_head_sumsq(o, bd) for o in outs]
    res = []
    for u, unit in enumerate(units):
        st = unit[5] * jnp.exp(bs[u][c - 1:c, :]) + jnp.where(bd > 0, upds[u], 0.0)
        y = outs[u] * lax.rsqrt(sss[u] * (1.0 / HGRN_KDIM) + EPS) * unit[4]
        res.append((y, st))
    return res


def _hgrn_kernel(zh_ref, lb_ref, gw_ref, m3_ref, lmask_ref, bd_ref, o_ref, st_ref):
    i = pl.program_id(1)
    nb, ts = zh_ref.shape[0], zh_ref.shape[1]
    c = HGRN_CHUNK
    w = HGRN_WIDTH

    @pl.when(i == 0)
    def _():
        st_ref[...] = jnp.zeros(st_ref.shape, F32)

    one_minus_lb = 1.0 - lb_ref[...]
    gw = gw_ref[...]
    bd = bd_ref[...]
    m3 = m3_ref[...]

    def chunk(ci, carry):
        rows = pl.ds(pl.multiple_of(ci * c, c), c)
        units = []
        for s in range(nb):
            q, f, v, g = [zh_ref[s, rows, j * w:(j + 1) * w] for j in range(4)]
            kk = one_minus_lb * _sigmoid(-f)
            lf = jnp.log(1.0 - jnp.minimum(kk, MAX_ONE_MINUS_F))
            hi = lf.astype(BF16)
            r1 = lf - hi.astype(F32)
            mid = r1.astype(BF16)
            lo = (r1 - mid.astype(F32)).astype(BF16)
            units.append((_silu(q), kk, jnp.concatenate([hi, mid, lo], axis=0), v.astype(BF16), gw * _silu(g),
                          st_ref[s]))
        for s, (y, st) in enumerate(_hgrn_chunks(units, bd, m3, lmask_ref)):
            st_ref[s] = st
            o_ref[s, rows, :] = y.astype(o_ref.dtype)
        return carry

    lax.fori_loop(0, ts // c, chunk, 0)


def _hgrn_mixer(zh, lb, norm_w, consts):
    b, s, _ = zh.shape
    w = HGRN_WIDTH
    ts = min(HGRN_TILE, s)
    nb = HGRN_SEQS if b % HGRN_SEQS == 0 else 1
    m3, lmask, bd = consts
    full = lambda a: pl.BlockSpec(a.shape, lambda bi, i: (0,) * a.ndim)
    return pl.pallas_call(
        _hgrn_kernel,
        grid=(b // nb, s // ts),
        in_specs=[pl.BlockSpec((nb, ts, 4 * w), lambda bi, i: (bi, i, 0)),
                  pl.BlockSpec((1, w), lambda bi, i: (0, 0)),
                  pl.BlockSpec((1, w), lambda bi, i: (0, 0)),
                  full(m3), full(lmask), full(bd)],
        out_specs=pl.BlockSpec((nb, ts, w), lambda bi, i: (bi, i, 0)),
        out_shape=jax.ShapeDtypeStruct((b, s, w), BF16),
        scratch_shapes=[pltpu.VMEM((nb, w, w), F32)],
        compiler_params=_cparams(("parallel", "arbitrary")),
        name="hgrn_mixer",
    )(zh, lb.reshape(1, w), norm_w.reshape(1, w), m3, lmask, bd)


def _attn_bias():
    qi = np.arange(WINDOW)[:, None]
    kj = np.arange(2 * WINDOW)[None, :]
    dist = qi + WINDOW - kj
    valid = (dist >= 0) & (dist < WINDOW)
    slopes = np.exp2(-8.0 * np.arange(1, ATT_HEADS + 1) / ATT_HEADS)
    bias = np.where(valid[None], -slopes[:, None, None] * dist[None] * LOG2E, NEG)
    first = np.where(kj[None] < WINDOW, NEG, bias)
    return jnp.asarray(np.concatenate([bias, first]), dtype=F32)


def _attn_kernel(sink_ref, zq_ref, zkv_ref, qw_ref, kw_ref, bias_ref, bdq_ref, bdk_ref, o_ref,
                 qbuf, kbuf, vbuf):
    i = pl.program_id(1)
    ts = zq_ref.shape[1]
    hw = 2 * HEAD_DIM

    @pl.when(i == 0)
    def _():
        kbuf[:, 0:WINDOW, :] = jnp.zeros((4, WINDOW, hw), BF16)
        vbuf[:, 0:WINDOW, :] = jnp.zeros((4, WINDOW, hw), BF16)

    q = zq_ref[0]
    ssq = _head_sumsq(q, bdq_ref[...])
    qbuf[...] = (q * lax.rsqrt(ssq * (1.0 / HEAD_DIM) + EPS) * (qw_ref[...] * (HEAD_DIM ** -0.5 * LOG2E))).astype(BF16)
    kv = zkv_ref[0]
    k = kv[:, :KV_WIDTH]
    v = kv[:, KV_WIDTH:]
    ssk = _head_sumsq(k, bdk_ref[...])
    kn = k * lax.rsqrt(ssk * (1.0 / HEAD_DIM) + EPS) * kw_ref[...]
    kr = pltpu.roll(kn, HEAD_DIM, 1)
    vr = pltpu.roll(v, HEAD_DIM, 1)
    low = lax.broadcasted_iota(jnp.int32, (1, hw), 1) < HEAD_DIM
    for j in range(ATT_KV_HEADS):
        for half in range(2):
            keep = low if half == 0 else jnp.logical_not(low)
            ksrc = kn if j == half else kr
            vsrc = v if j == half else vr
            kbuf[2 * j + half, WINDOW:WINDOW + ts, :] = jnp.where(keep, ksrc, 0.0).astype(BF16)
            vbuf[2 * j + half, WINDOW:WINDOW + ts, :] = jnp.where(keep, vsrc, 0.0).astype(BF16)

    def block(n, carry):
        r0 = pl.multiple_of(n * WINDOW, WINDOW)
        table = jnp.where(jnp.logical_and(i == 0, n == 0), ATT_HEADS, 0)
        for g0 in range(0, ATT_HEADS, ATT_GROUP):
            heads = range(g0, g0 + ATT_GROUP)
            logits = []
            for h in heads:
                hp, half = h // 2, h % 2
                j = h // (ATT_HEADS // ATT_KV_HEADS)
                qp = qbuf[pl.ds(r0, WINDOW), hp * hw:(hp + 1) * hw]
                keys = kbuf[2 * j + half, pl.ds(r0, 2 * WINDOW), :]
                logits.append(_dot_nt(qp, keys) + bias_ref[table + h])
            ps, scales = [], []
            for h, lg in zip(heads, logits):
                sink = sink_ref[h] * LOG2E
                m = jnp.maximum(jnp.max(lg, axis=-1, keepdims=True), sink)
                p = jnp.exp2(lg - m)
                scales.append(1.0 / (jnp.sum(p, axis=-1, keepdims=True) + jnp.exp2(sink - m)))
                ps.append(p.astype(BF16))
            outs = []
            for h, p in zip(heads, ps):
                half = h % 2
                j = h // (ATT_HEADS // ATT_KV_HEADS)
                vals = vbuf[2 * j + half, pl.ds(r0, 2 * WINDOW), :]
                outs.append(_dot(p, vals))
            for k in range(0, ATT_GROUP, 2):
                hp = (g0 + k) // 2
                acc = outs[k] * scales[k] + outs[k + 1] * scales[k + 1]
                o_ref[0, pl.ds(r0, WINDOW), hp * hw:(hp + 1) * hw] = acc.astype(o_ref.dtype)
        return carry

    lax.fori_loop(0, ts // WINDOW, block, 0)
    kbuf[:, 0:WINDOW, :] = kbuf[:, ts:ts + WINDOW, :]
    vbuf[:, 0:WINDOW, :] = vbuf[:, ts:ts + WINDOW, :]


def _attn_mixer(zq, zkv, q_norm_w, k_norm_w, sinks, consts):
    b, s, _ = zq.shape
    ts = min(SEQ_TILE, s)
    bias, bdq, bdk = consts
    hw = 2 * HEAD_DIM
    qw = jnp.tile(q_norm_w, ATT_HEADS).reshape(1, ATT_WIDTH)
    kw = jnp.tile(k_norm_w, ATT_KV_HEADS).reshape(1, KV_WIDTH)
    full = lambda a: pl.BlockSpec(a.shape, lambda bi, i, sk: (0,) * a.ndim)
    grid_spec = pltpu.PrefetchScalarGridSpec(
        num_scalar_prefetch=1,
        grid=(b, s // ts),
        in_specs=[pl.BlockSpec((1, ts, ATT_WIDTH), lambda bi, i, sk: (bi, i, 0)),
                  pl.BlockSpec((1, ts, 2 * KV_WIDTH), lambda bi, i, sk: (bi, i, 0)),
                  full(qw), full(kw), full(bias), full(bdq), full(bdk)],
        out_specs=pl.BlockSpec((1, ts, ATT_WIDTH), lambda bi, i, sk: (bi, i, 0)),
        scratch_shapes=[pltpu.VMEM((ts, ATT_WIDTH), BF16),
                        pltpu.VMEM((4, WINDOW + ts, hw), BF16),
                        pltpu.VMEM((4, WINDOW + ts, hw), BF16)],
    )
    return pl.pallas_call(
        _attn_kernel,
        grid_spec=grid_spec,
        out_shape=jax.ShapeDtypeStruct((b, s, ATT_WIDTH), BF16),
        compiler_params=_cparams(("parallel", "arbitrary")),
        name="swa_mixer",
    )(sinks, zq, zkv, qw, kw, bias, bdq, bdk)


def _router_constants(tm):
    t = np.arange(tm)
    before = (t[:, None] < t[None, :]).astype(np.float32)
    e = np.arange(N_EXPERTS)
    lower = (e[None, :] < e[:, None]).astype(np.float32)
    return (jnp.asarray(before, dtype=BF16), jnp.asarray(np.ones((tm, tm), np.float32), dtype=BF16),
            jnp.asarray(lower, dtype=BF16))


def _route(sel, scores):
    tm = sel.shape[1]
    group_scores = []
    for g in range(N_GROUPS):
        rows = [sel[g * EXPERTS_PER_GROUP + a:g * EXPERTS_PER_GROUP + a + 1] for a in range(EXPERTS_PER_GROUP)]
        best_pair = None
        for a in range(EXPERTS_PER_GROUP):
            for bb in range(a + 1, EXPERTS_PER_GROUP):
                pair = rows[a] + rows[bb]
                best_pair = pair if best_pair is None else jnp.maximum(best_pair, pair)
        group_scores.append(best_pair)
    top = functools.reduce(jnp.maximum, group_scores)
    best = jnp.full((1, tm), N_GROUPS - 1, jnp.int32)
    for g in reversed(range(N_GROUPS - 1)):
        best = jnp.where(group_scores[g] == top, g, best)
    row = lax.broadcasted_iota(jnp.int32, (N_EXPERTS, tm), 0)
    cand = jnp.where(row // EXPERTS_PER_GROUP == best, sel, NEG)
    m1 = jnp.max(cand, axis=0, keepdims=True)
    i1 = jnp.min(jnp.where(cand == m1, row, N_EXPERTS), axis=0, keepdims=True)
    oh1 = row == i1
    cand = jnp.where(oh1, NEG, cand)
    m2 = jnp.max(cand, axis=0, keepdims=True)
    i2 = jnp.min(jnp.where(cand == m2, row, N_EXPERTS), axis=0, keepdims=True)
    oh2 = row == i2
    s1 = jnp.sum(jnp.where(oh1, scores, 0.0), axis=0, keepdims=True)
    s2 = jnp.sum(jnp.where(oh2, scores, 0.0), axis=0, keepdims=True)
    return oh1, oh2, s1 / (s1 + s2), s2 / (s1 + s2)


def _out_kernel(yp_ref, yh_ref, ya_ref, x_ref, g1_ref, sc_ref, sh_ref, nw_ref, wo_ref, rwt_ref, rb_ref,
                before_ref, ones_ref, lower_ref, x1_ref, h2_ref, ls_ref, gate_ref, cnt_ref):
    tm = MOE_TOK_TILE
    subs = [pl.ds(u * tm, tm) for u in range(x_ref.shape[0] // tm)]
    p0, p1 = POOL_WIDTH, POOL_WIDTH + HGRN_WIDTH
    h2s = []
    for rows in subs:
        mix = (_dot(yp_ref[rows, :], wo_ref[0:p0, :]) + _dot(yh_ref[rows, :], wo_ref[p0:p1, :])
               + _dot(ya_ref[rows, :], wo_ref[p1:, :]))
        x1 = x_ref[rows, :] + g1_ref[0] * mix
        x1_ref[rows, :] = x1
        ms = jnp.mean(x1 * x1, axis=-1, keepdims=True)
        h2 = ((x1 * lax.rsqrt(ms + EPS) * nw_ref[...]) * (1.0 + sc_ref[0]) + sh_ref[0]).astype(BF16)
        h2_ref[rows, :] = h2
        h2s.append(h2)
    logits = [_dot_nt(rwt_ref[...], h2) for h2 in h2s]
    picks = []
    for lg in logits:
        ex = jnp.exp(lg - jnp.max(lg, axis=0, keepdims=True))
        scores = ex / jnp.sum(ex, axis=0, keepdims=True)
        picks.append(_route(scores + rb_ref[...], scores))
    chosen = [jnp.where(jnp.logical_or(oh1, oh2), 1.0, 0.0).astype(BF16) for oh1, oh2, _, _ in picks]
    ranks = [_dot(ch, before_ref[...]) for ch in chosen]
    counts = [_dot(ch, ones_ref[...]) for ch in chosen]
    aligned = [(jnp.floor((cn + (ROW_ALIGN - 1)) * (1.0 / ROW_ALIGN)) * ROW_ALIGN).astype(BF16) for cn in counts]
    slots = [_dot(lower_ref[...], al) + rk for al, rk in zip(aligned, ranks)]
    for u, rows in enumerate(subs):
        oh1, oh2, w1, w2 = picks[u]
        gate_ref[:, rows] = jnp.concatenate([w1, w2], axis=0)
        ls_ref[:, rows] = jnp.concatenate([jnp.sum(jnp.where(oh1, slots[u], 0.0), axis=0, keepdims=True),
                                           jnp.sum(jnp.where(oh2, slots[u], 0.0), axis=0, keepdims=True)], axis=0)
        cnt_ref[u] = counts[u][:, 0:128]


def _out_proj_router(yp, yh, ya, x, g1, sc, sh, nw, wo_b, layer, rwt_b, rb, consts):
    b, s, d = x.shape
    t = b * s
    tm = MOE_TOK_TILE
    n_sub = ROUTER_SUBTILES if s % (ROUTER_SUBTILES * tm) == 0 else 1
    ts = n_sub * tm
    per_batch = s // ts
    nt = t // tm
    before, ones, lower = consts
    tok = lambda w: pl.BlockSpec((ts, w), lambda i: (i, 0))
    vec = pl.BlockSpec((1, 1, d), lambda i: (i // per_batch, 0, 0))
    full = lambda a: pl.BlockSpec(a.shape, lambda i: (0,) * a.ndim)
    lanes = pl.BlockSpec((2, ts), lambda i: (0, i))
    return pl.pallas_call(
        _out_kernel,
        grid=(t // ts,),
        in_specs=[tok(POOL_WIDTH), tok(HGRN_WIDTH), tok(ATT_WIDTH), tok(d), vec, vec, vec,
                  pl.BlockSpec((1, d), lambda i: (0, 0)),
                  pl.BlockSpec((None,) + wo_b.shape[1:], lambda i: (layer, 0, 0)), full(rwt_b),
                  pl.BlockSpec((N_EXPERTS, 1), lambda i: (0, 0)), full(before), full(ones), full(lower)],
        out_specs=[tok(d), tok(d), lanes, lanes, pl.BlockSpec((n_sub, N_EXPERTS, 128), lambda i: (i, 0, 0))],
        out_shape=[jax.ShapeDtypeStruct((t, d), F32), jax.ShapeDtypeStruct((t, d), BF16),
                   jax.ShapeDtypeStruct((2, t), F32), jax.ShapeDtypeStruct((2, t), F32),
                   jax.ShapeDtypeStruct((nt, N_EXPERTS, 128), F32)],
        compiler_params=_cparams(("parallel",)),
        name="out_proj_router",
    )(yp.reshape(t, -1), yh.reshape(t, -1), ya.reshape(t, -1), x.reshape(t, d),
      g1.reshape(b, 1, d), sc.reshape(b, 1, d), sh.reshape(b, 1, d), nw.reshape(1, d), wo_b, rwt_b,
      rb.reshape(N_EXPERTS, 1), before, ones, lower)


def _moe_tables(cnt_out):
    cnt = jnp.round(cnt_out[:, :, 0]).astype(jnp.int32)
    cnt = (cnt + ROW_ALIGN - 1) // ROW_ALIGN * ROW_ALIGN
    total = jnp.sum(cnt, axis=0)
    padded = (total + EXPERT_TILE - 1) // EXPERT_TILE * EXPERT_TILE
    ends = jnp.cumsum(padded)
    first = ends - padded
    start = first[None, :] + jnp.cumsum(cnt, axis=0) - cnt
    loff = jnp.cumsum(cnt, axis=1) - cnt
    n_tiles = _moe_rows(cnt.shape[0] * MOE_TOK_TILE) // EXPERT_TILE
    n_used = ends[-1] // EXPERT_TILE
    tile_row = jnp.minimum(jnp.arange(n_tiles), n_used - 1) * EXPERT_TILE
    tile_expert = jnp.sum((ends[None, :] <= tile_row[:, None]).astype(jnp.int32), axis=1)
    pieces = jnp.sum(cnt, axis=1, keepdims=True) // ROW_ALIGN

    def copies(n_per_expert, offset_in_chunk, n_max):
        incl = jnp.cumsum(n_per_expert, axis=1)
        k = jnp.arange(n_max, dtype=jnp.int32)
        owner = jnp.sum((incl[:, None, :] <= k[None, :, None]).astype(jnp.int32), axis=2)
        owner = jnp.minimum(owner, N_EXPERTS - 1)
        pick = lambda a: jnp.take_along_axis(a, owner, axis=1)
        off = offset_in_chunk(k[None, :] - pick(incl - n_per_expert), pick(cnt))
        return pick(loff) + off, pick(start) + off, incl[:, -1:]

    big_src, big_dst, n_big = copies(cnt // (2 * ROW_ALIGN), lambda j, c: j * (2 * ROW_ALIGN), MAX_PIECES // 2)
    small_src, small_dst, n_small = copies(cnt % (2 * ROW_ALIGN) // ROW_ALIGN, lambda j, c: c - ROW_ALIGN, N_EXPERTS)
    moves = jnp.concatenate([big_src, big_dst, small_src, small_dst, n_big, n_small, pieces], axis=1)
    tail = jnp.concatenate([first + total, padded - total, jnp.stack([ends[-1], n_tiles - n_used])])
    return (moves.reshape(-1).astype(jnp.int32), tail.astype(jnp.int32), tile_expert.astype(jnp.int32),
            n_used.reshape(1).astype(jnp.int32))


MOVE_BIG_SRC = 0
MOVE_BIG_DST = MOVE_BIG_SRC + MAX_PIECES // 2
MOVE_SMALL_SRC = MOVE_BIG_DST + MAX_PIECES // 2
MOVE_SMALL_DST = MOVE_SMALL_SRC + N_EXPERTS
MOVE_N_BIG = MOVE_SMALL_DST + N_EXPERTS
MOVE_N_SMALL = MOVE_N_BIG + 1
MOVE_PIECES = MOVE_N_SMALL + 1
MOVE_WIDTH = MOVE_PIECES + 1


def _tile_copies(moves_ref, tile, make_copy):
    base = tile * MOVE_WIDTH
    for src0, dst0, n_at, size in ((MOVE_BIG_SRC, MOVE_BIG_DST, MOVE_N_BIG, 2 * ROW_ALIGN),
                                   (MOVE_SMALL_SRC, MOVE_SMALL_DST, MOVE_N_SMALL, ROW_ALIGN)):
        def body(j, carry):
            make_copy(pl.multiple_of(moves_ref[base + src0 + j], ROW_ALIGN),
                      pl.multiple_of(moves_ref[base + dst0 + j], ROW_ALIGN), size).start()
            return carry

        lax.fori_loop(0, moves_ref[base + n_at], body, 0)


def _moe_rows(n_tokens):
    per_tile = 2 * MOE_TOK_TILE + N_EXPERTS * (ROW_ALIGN - 1)
    rows = (n_tokens // MOE_TOK_TILE) * per_tile + N_EXPERTS * (EXPERT_TILE - ROW_ALIGN)
    return (rows + EXPERT_TILE - 1) // EXPERT_TILE * EXPERT_TILE


def _chunk_copies(count, make_copy, wait=False):
    for size in CHUNK_SIZES:
        offset = jnp.bitwise_and(count, ~(2 * size - 1))

        @pl.when(jnp.bitwise_and(count, size) != 0)
        def _():
            copy = make_copy(pl.multiple_of(offset, ROW_ALIGN), size)
            copy.wait() if wait else copy.start()


def _wait_rows(n_pieces, make_copy):
    for p in WAIT_PIECES:
        @pl.when(jnp.bitwise_and(n_pieces, p) != 0)
        def _():
            make_copy(p * ROW_ALIGN).wait()


def _slot_rows(tm):
    return lax.broadcasted_iota(jnp.int32, (MOE_SLOTS, tm), 0).astype(F32)


def _sort_kernel(moves_ref, tail_ref, h2_ref, ls_ref, xs_ref, buf_ref, sem_ref):
    i = pl.program_id(0)
    n = pl.num_programs(0)
    tm = h2_ref.shape[0]
    slot = i % 2

    def sent(tile, s):
        _wait_rows(moves_ref[tile * MOVE_WIDTH + MOVE_PIECES], lambda rows: pltpu.make_async_copy(
            buf_ref.at[s, pl.ds(0, rows), :], xs_ref.at[pl.ds(0, rows), :], sem_ref.at[s]))

    @pl.when(i >= 2)
    def _():
        sent(i - 2, slot)

    ls = ls_ref[...]
    srow = _slot_rows(tm)
    perm = jnp.where(jnp.logical_or(srow == ls[0:1], srow == ls[1:2]), 1.0, 0.0).astype(BF16)
    buf_ref[slot] = _dot(perm, h2_ref[...]).astype(BF16)

    _tile_copies(moves_ref, i, lambda src, dst, size: pltpu.make_async_copy(
        buf_ref.at[slot, pl.ds(src, size), :], xs_ref.at[pl.ds(dst, size), :], sem_ref.at[slot]))

    @pl.when(i == n - 1)
    def _():
        @pl.when(n > 1)
        def _():
            sent(i - 1, 1 - slot)
        sent(i, slot)

        buf_ref[0] = jnp.zeros(buf_ref.shape[1:], BF16)
        zeros_to = lambda row, size: pltpu.make_async_copy(
            buf_ref.at[0, pl.ds(0, size), :], xs_ref.at[pl.ds(row, size), :], sem_ref.at[0])

        for wait in (False, True):
            def expert_tail(e, carry):
                st = pl.multiple_of(tail_ref[e], ROW_ALIGN)
                _chunk_copies(tail_ref[N_EXPERTS + e], lambda off, size: zeros_to(st + off, size), wait)
                return carry

            def free_tile(j, carry):
                copy = zeros_to(pl.multiple_of(tail_ref[2 * N_EXPERTS] + j * EXPERT_TILE, ROW_ALIGN), EXPERT_TILE)
                copy.wait() if wait else copy.start()
                return carry

            lax.fori_loop(0, N_EXPERTS, expert_tail, 0)
            lax.fori_loop(0, tail_ref[2 * N_EXPERTS + 1], free_tile, 0)


def _sort_tokens(h2, ls, moves, tail, n_rows):
    t, d = h2.shape
    tm = MOE_TOK_TILE
    grid_spec = pltpu.PrefetchScalarGridSpec(
        num_scalar_prefetch=2,
        grid=(t // tm,),
        in_specs=[pl.BlockSpec((tm, d), lambda i, *_: (i, 0)),
                  pl.BlockSpec((2, tm), lambda i, *_: (0, i))],
        out_specs=pl.BlockSpec(memory_space=pl.ANY),
        scratch_shapes=[pltpu.VMEM((2, MOE_SLOTS, d), BF16), pltpu.SemaphoreType.DMA((2,))],
    )
    return pl.pallas_call(
        _sort_kernel,
        grid_spec=grid_spec,
        out_shape=jax.ShapeDtypeStruct((n_rows, d), BF16),
        compiler_params=_cparams(("arbitrary",)),
        name="moe_sort",
    )(moves, tail, h2, ls)


def _gmm_kernel(te_ref, nu_ref, xs_ref, wg_ref, wu_ref, wd_ref, ys_ref, wgb, wub, wdb):
    i = pl.program_id(0)
    used = i < nu_ref[0]
    new_expert = jnp.logical_or(i == 0, te_ref[i] != te_ref[jnp.maximum(i - 1, 0)])

    @pl.when(jnp.logical_and(used, new_expert))
    def _():
        wgb[...] = wg_ref[...].astype(BF16)
        wub[...] = wu_ref[...].astype(BF16)
        wdb[...] = wd_ref[...].astype(BF16)

    @pl.when(used)
    def _():
        x = xs_ref[...]
        he = _silu(_dot(x, wgb[...])) * _dot(x, wub[...])
        ys_ref[...] = _dot(he.astype(BF16), wdb[...]).astype(BF16)

    @pl.when(jnp.logical_not(used))
    def _():
        ys_ref[...] = jnp.zeros(ys_ref.shape, BF16)


def _grouped_mlp(xs, tile_expert, n_used, wg, wu, wd, layer):
    n_rows, d = xs.shape
    te = EXPERT_TILE
    row_map = lambda i, tex, nu: (jnp.minimum(i, nu[0] - 1), 0)
    grid_spec = pltpu.PrefetchScalarGridSpec(
        num_scalar_prefetch=2,
        grid=(n_rows // te,),
        in_specs=[pl.BlockSpec((te, d), row_map),
                  pl.BlockSpec((None, None, d, D_EXPERT), lambda i, tex, nu: (layer, tex[i], 0, 0)),
                  pl.BlockSpec((None, None, d, D_EXPERT), lambda i, tex, nu: (layer, tex[i], 0, 0)),
                  pl.BlockSpec((None, None, D_EXPERT, d), lambda i, tex, nu: (layer, tex[i], 0, 0))],
        out_specs=pl.BlockSpec((te, d), lambda i, tex, nu: (i, 0)),
        scratch_shapes=[pltpu.VMEM((d, D_EXPERT), BF16), pltpu.VMEM((d, D_EXPERT), BF16),
                        pltpu.VMEM((D_EXPERT, d), BF16)],
    )
    return pl.pallas_call(
        _gmm_kernel,
        grid_spec=grid_spec,
        out_shape=jax.ShapeDtypeStruct((n_rows, d), BF16),
        compiler_params=_cparams(("arbitrary",)),
        name="moe_grouped_mlp",
    )(tile_expert, n_used, xs, wg, wu, wd)


def _combine_kernel(moves_ref, ls_ref, gate_ref, x1_ref, g2_ref, ys_ref, o_ref, buf_ref, sem_ref):
    i = pl.program_id(0)
    n = pl.num_programs(0)
    tm = x1_ref.shape[0]
    slot = i % 2
    n_pieces = moves_ref[i * MOVE_WIDTH + MOVE_PIECES]

    def fetch(tile, s):
        _tile_copies(moves_ref, tile, lambda dst, src, size: pltpu.make_async_copy(
            ys_ref.at[pl.ds(src, size), :], buf_ref.at[s, pl.ds(dst, size), :], sem_ref.at[s]))

    @pl.when(i == 0)
    def _():
        fetch(0, 0)

    @pl.when(i + 1 < n)
    def _():
        fetch(i + 1, 1 - slot)

    _wait_rows(n_pieces, lambda rows: pltpu.make_async_copy(
        ys_ref.at[pl.ds(0, rows), :], buf_ref.at[slot, pl.ds(0, rows), :], sem_ref.at[slot]))

    srow = _slot_rows(tm)
    ls = ls_ref[...]
    gate = gate_ref[...]
    weights = (jnp.where(srow == ls[0:1], gate[0:1], 0.0)
               + jnp.where(srow == ls[1:2], gate[1:2], 0.0)).astype(BF16)
    written = lax.broadcasted_iota(jnp.int32, (MOE_SLOTS, 1), 0) < n_pieces * ROW_ALIGN
    rows = jnp.where(written, buf_ref[slot], jnp.zeros((), BF16))
    y = _dot_tn(weights, rows)
    o_ref[...] = x1_ref[...] + g2_ref[0] * y


def _combine(ys, ls, gate, x1, g2, moves, per_batch):
    t, d = x1.shape
    tm = MOE_TOK_TILE
    grid_spec = pltpu.PrefetchScalarGridSpec(
        num_scalar_prefetch=1,
        grid=(t // tm,),
        in_specs=[pl.BlockSpec((2, tm), lambda i, *_: (0, i)),
                  pl.BlockSpec((2, tm), lambda i, *_: (0, i)),
                  pl.BlockSpec((tm, d), lambda i, *_: (i, 0)),
                  pl.BlockSpec((1, 1, d), lambda i, *_: (i // per_batch, 0, 0)),
                  pl.BlockSpec(memory_space=pl.ANY)],
        out_specs=pl.BlockSpec((tm, d), lambda i, *_: (i, 0)),
        scratch_shapes=[pltpu.VMEM((2, MOE_SLOTS, d), BF16), pltpu.SemaphoreType.DMA((2,))],
    )
    return pl.pallas_call(
        _combine_kernel,
        grid_spec=grid_spec,
        out_shape=jax.ShapeDtypeStruct((t, d), F32),
        compiler_params=_cparams(("arbitrary",)),
        name="moe_combine",
    )(moves, ls, gate, x1, g2.reshape(-1, 1, d), ys)


def kernel(x, c, ada_w, ada_b, norm1_w, norm2_w, w_in, pool_w, pool_scale, hgrn_lb_raw, hgrn_norm_w, q_norm_w,
           k_norm_w, attn_sinks, w_out, router_w, router_bias, expert_w_gate, expert_w_up, expert_w_down):
    b, s, d = x.shape
    depth = ada_w.shape[0]
    t = b * s
    per_batch = s // MOE_TOK_TILE
    n_rows = _moe_rows(t)

    p = jax.nn.softmax(hgrn_lb_raw.astype(F32), axis=0)
    lower_bounds = jnp.maximum(jnp.cumsum(p, axis=0) - p[0:1], 0.0)

    mod = _modulation(c, ada_w, ada_b)
    hgrn_consts = _hgrn_constants() + (_head_ones(HGRN_WIDTH, HGRN_KDIM),)
    attn_consts = (_attn_bias(), _head_ones(ATT_WIDTH, HEAD_DIM), _head_ones(KV_WIDTH, HEAD_DIM))
    router_consts = _router_constants(MOE_TOK_TILE)
    rwt_b = router_w.T.astype(BF16)
    w_in_b = w_in.astype(BF16)
    w_out_b = w_out.astype(BF16)

    for l in range(depth):
        sh1, sc1, g1, sh2, sc2, g2 = [mod[l, :, j * d:(j + 1) * d] for j in range(6)]
        za, zh, zq, zkv = _in_proj(x, sc1, sh1, norm1_w[l], w_in_b, l)
        yp = _pool_mixer(za, _block_diag(pool_w[l]).astype(BF16), pool_scale[l])
        yh = _hgrn_mixer(zh, lower_bounds[l], hgrn_norm_w[l], hgrn_consts)
        ya = _attn_mixer(zq, zkv, q_norm_w[l], k_norm_w[l], attn_sinks[l], attn_consts)
        x1, h2, ls, gate, cnt_out = _out_proj_router(
            yp, yh, ya, x, g1, sc2, sh2, norm2_w[l], w_out_b, l, rwt_b, router_bias, router_consts)
        moves, tail, tile_expert, n_used = _moe_tables(cnt_out)
        xs = _sort_tokens(h2, ls, moves, tail, n_rows)
        ys = _grouped_mlp(xs, tile_expert, n_used, expert_w_gate, expert_w_up, expert_w_down, l)
        x = _combine(ys, ls, gate, x1, g2, moves, per_batch).reshape(b, s, d)
    return x
```

```python
import functools

import numpy as np
import jax
import jax.numpy as jnp
from jax import lax
from jax.experimental import pallas as pl
from jax.experimental.pallas import tpu as pltpu

F32 = jnp.float32
BF16 = jnp.bfloat16

D_MODEL = 1024
POOL_WINDOWS = (2, 4, 8, 16)
POOL_WIDTH = 256
POOL_GROUP = 64
POOL_HALO = 32
HGRN_HEADS = 4
HGRN_KDIM = 64
HGRN_WIDTH = 256
HGRN_CHUNK = 64
ATT_HEADS = 8
ATT_KV_HEADS = 2
HEAD_DIM = 64
ATT_WIDTH = 512
KV_WIDTH = 128
WINDOW = 128
N_EXPERTS = 16
N_GROUPS = 4
EXPERTS_PER_GROUP = 4
D_EXPERT = 512
EPS = 1e-6
MAX_ONE_MINUS_F = 1.0 - 1e-6
LOG2E = 1.4426950408889634
NEG = -1e30

VMEM_LIMIT = 48 * 1024 * 1024

IN_TILE = 1024
ATT_TILE = 1024
MOE_TOK_TILE = 256
ROW_ALIGN = 16
MOE_SLOTS = 768
EXPERT_TILE = 1024
ZERO_ROWS = 512
CHUNK_SIZES = (512, 256, 128, 64, 32, 16)
MAX_PIECES = MOE_SLOTS // ROW_ALIGN
WAIT_PIECES = (32, 16, 8, 4, 2, 1)


def _sigmoid(x):
    return 1.0 / (1.0 + jnp.exp(-x))


def _silu(x):
    return x * _sigmoid(x)


def _cparams(sem, **kw):
    return pltpu.CompilerParams(dimension_semantics=sem, vmem_limit_bytes=VMEM_LIMIT, **kw)


def _dot(a, b):
    return jnp.dot(a, b, preferred_element_type=F32)


def _dot_nt(a, b):
    return lax.dot_general(a, b, (((1,), (1,)), ((), ())), preferred_element_type=F32)


def _dot_tn(a, b):
    return lax.dot_general(a, b, (((0,), (0,)), ((), ())), preferred_element_type=F32)


def _head_sumsq(x, bd):
    return _dot((x * x).astype(BF16), bd)


def _mod_kernel(c_ref, w_ref, b_ref, o_ref):
    cond = _silu(c_ref[...])
    o_ref[0] = _dot(cond.astype(BF16), w_ref[0].astype(BF16)) + b_ref[0]


def _modulation(c, ada_w, ada_b):
    depth, d, n = ada_w.shape
    b = c.shape[0]
    nb = n // d
    return pl.pallas_call(
        _mod_kernel,
        grid=(depth, nb),
        in_specs=[pl.BlockSpec((b, d), lambda l, j: (0, 0)),
                  pl.BlockSpec((1, d, d), lambda l, j: (l, 0, j)),
                  pl.BlockSpec((1, 1, d), lambda l, j: (l, 0, j))],
        out_specs=pl.BlockSpec((1, b, d), lambda l, j: (l, 0, j)),
        out_shape=jax.ShapeDtypeStruct((depth, b, n), F32),
        compiler_params=_cparams(("parallel", "parallel")),
        name="adaln_mod",
    )(c, ada_w, ada_b.reshape(depth, 1, n))


def _in_kernel(x_ref, sc_ref, sh_ref, nw_ref, w_ref, pw_ref, ps_ref, yp_ref, zh_ref, zq_ref, zkv_ref,
               buf_ref, sa_ref, sb_ref):
    x = x_ref[0]
    ms = jnp.mean(x * x, axis=-1, keepdims=True)
    h = (x * lax.rsqrt(ms + EPS) * nw_ref[...]) * (1.0 + sc_ref[0]) + sh_ref[0]
    z = _dot(h.astype(BF16), w_ref[...])
    zh_ref[0] = z[:, POOL_WIDTH:POOL_WIDTH + 4 * HGRN_WIDTH]
    zq_ref[0] = z[:, POOL_WIDTH + 4 * HGRN_WIDTH:POOL_WIDTH + 4 * HGRN_WIDTH + ATT_WIDTH]
    zkv_ref[0] = z[:, POOL_WIDTH + 4 * HGRN_WIDTH + ATT_WIDTH:]
    yp_ref[0] = _pool(z[:, :POOL_WIDTH], pl.program_id(1), pw_ref, ps_ref, buf_ref, sa_ref, sb_ref).astype(yp_ref.dtype)


def _in_proj(x, sc, sh, nw, w_in_b, layer, pool_bd, pool_scale):
    b, s, d = x.shape
    n = w_in_b.shape[2]
    ts = min(IN_TILE, s)
    tok = lambda w: pl.BlockSpec((1, ts, w), lambda bi, i: (bi, i, 0))
    vec = pl.BlockSpec((1, 1, d), lambda bi, i: (bi, 0, 0))
    full = lambda a: pl.BlockSpec(a.shape, lambda bi, i: (0,) * a.ndim)
    pool_scale = pool_scale.reshape(1, POOL_WIDTH)
    outs = ((POOL_WIDTH, BF16), (4 * HGRN_WIDTH, F32), (ATT_WIDTH, F32), (2 * KV_WIDTH, F32))
    return pl.pallas_call(
        _in_kernel,
        grid=(b, s // ts),
        in_specs=[tok(d), vec, vec,
                  pl.BlockSpec((1, d), lambda bi, i: (0, 0)),
                  pl.BlockSpec((None, d, n), lambda bi, i: (layer, 0, 0)), full(pool_bd), full(pool_scale)],
        out_specs=[tok(w) for w, _ in outs],
        out_shape=[jax.ShapeDtypeStruct((b, s, w), dt) for w, dt in outs],
        scratch_shapes=[pltpu.VMEM((POOL_HALO + ts, POOL_WIDTH), F32)] * 3,
        compiler_params=_cparams(("parallel", "arbitrary")),
        name="in_proj_pool",
    )(x, sc.reshape(b, 1, d), sh.reshape(b, 1, d), nw.reshape(1, d), w_in_b, pool_bd, pool_scale)


def _pool(a, i, w_ref, scale_ref, buf_ref, sa_ref, sb_ref):
    ts = a.shape[0]
    halo = POOL_HALO
    end = halo + ts
    half = 2 * POOL_GROUP

    @pl.when(i == 0)
    def _():
        buf_ref[0:halo, :] = jnp.zeros((halo, POOL_WIDTH), F32)

    buf_ref[halo:end, :] = a
    s2 = buf_ref[8:end, :] + buf_ref[7:end - 1, :]
    sa_ref[8:end, :] = s2
    s4 = sa_ref[16:end, :] + sa_ref[14:end - 2, :]
    sb_ref[16:end, :] = s4
    s8 = sb_ref[24:end, half:] + sb_ref[20:end - 4, half:]
    sa_ref[24:end, half:] = s8
    s16 = sa_ref[halo:end, half:] + sa_ref[halo - 8:end - 8, half:]
    lane = lax.broadcasted_iota(jnp.int32, (1, POOL_WIDTH), 1)
    win = jnp.left_shift(2, lane // POOL_GROUP)
    low = lane[:, 0:half] % half < POOL_GROUP
    acc = jnp.concatenate([jnp.where(low, s2[halo - 8:, 0:half], s4[halo - 16:, 0:half]),
                           jnp.where(low, s8[halo - 24:], s16)], axis=1)
    pos = i * ts + lax.broadcasted_iota(jnp.int32, (ts, 1), 0)
    count = jnp.minimum(pos + 1, win).astype(F32)
    pooled = acc / count - a
    buf_ref[0:halo, :] = a[ts - halo:, :]
    return _dot(pooled.astype(BF16), w_ref[...]) * scale_ref[...]


def _block_diag(blocks):
    g, n, _ = blocks.shape
    eye = jnp.eye(g, dtype=blocks.dtype)
    return (eye[:, None, :, None] * blocks[:, :, None, :]).reshape(g * n, g * n)


def _head_ones(width, head):
    idx = np.arange(width) // head
    return jnp.asarray((idx[:, None] == idx[None, :]).astype(np.float32), dtype=BF16)


HGRN_LEVELS = (32, 16, 8, 4, 2, 1)
HGRN_MATMUL_LEVELS = (2, 1)
N_LEVEL_MASKS = len(HGRN_LEVELS) + 1
HGRN_SEQS = 8
HGRN_TILE = 256
ATT_GROUP = 8
ROUTER_SUBTILES = 4


def _hgrn_constants():
    c = HGRN_CHUNK
    m = np.zeros((len(HGRN_MATMUL_LEVELS) + 1, c, c), np.float32)
    masks = np.zeros((N_LEVEL_MASKS, c, c), np.float32)
    for lvl, n in enumerate(HGRN_LEVELS):
        for t in range(c):
            blk = t // (2 * n)
            mid = blk * 2 * n + n
            if t >= mid:
                masks[lvl, t, blk * 2 * n:mid] = 1.0
            if n in HGRN_MATMUL_LEVELS:
                row = m[HGRN_MATMUL_LEVELS.index(n), t]
                if t >= mid:
                    row[mid:t + 1] = 1.0
                else:
                    row[t + 1:mid] = 1.0
    for t in range(c):
        m[-1, t, :t + 1] = 1.0
        masks[-1, t, t] = 1.0
    m = m.reshape(-1, c)
    m3 = np.concatenate([m, m, m], axis=1)
    masks = np.tile(masks, (1, HGRN_HEADS, 1))
    return jnp.asarray(m3, dtype=BF16), jnp.asarray(masks, dtype=F32)


def _hgrn_chunks(units, bd, m3, lmask_ref):
    c = HGRN_CHUNK
    w = HGRN_WIDTH
    n_fine = len(HGRN_MATMUL_LEVELS)
    lane_head = lax.broadcasted_iota(jnp.int32, (1, w), 1) // HGRN_KDIM
    sums = [_dot(m3, u[2]) for u in units]
    bs = [s[n_fine * c:] for s in sums]

    def level_decay(u, n):
        if n in HGRN_MATMUL_LEVELS:
            k = HGRN_MATMUL_LEVELS.index(n)
            return jnp.exp(sums[u][k * c:(k + 1) * c])
        blocks = bs[u].reshape(c // (2 * n), 2 * n, w)
        ref = blocks[:, n - 1:n, :]
        right = lax.broadcasted_iota(jnp.int32, (1, 2 * n, 1), 1) >= n
        return jnp.exp(jnp.where(right, blocks - ref, ref - blocks).reshape(c, w))

    groups = c // 8
    parts = [[[None] * groups for _ in range(HGRN_HEADS)] for _ in units]

    def add_part(u, h, g, piece):
        parts[u][h][g] = piece if parts[u][h][g] is None else parts[u][h][g] + piece

    for lvl in range(N_LEVEL_MASKS):
        n = HGRN_LEVELS[lvl] if lvl < len(HGRN_LEVELS) else 0
        wanted = [g for g in range(groups) if (8 * g) % (2 * n) >= n] if n >= 8 else list(range(groups))
        for u, (qf, kk, _, _, _, _) in enumerate(units):
            if lvl < len(HGRN_LEVELS):
                e = level_decay(u, n)
                ql = qf * e
                kl = (kk * e).astype(BF16)
            else:
                ql = qf
                kl = kk.astype(BF16)
            ql = jnp.concatenate([ql[8 * g:8 * g + 8] for g in wanted], axis=0).astype(BF16)
            zero = jnp.zeros_like(ql)
            qs = jnp.concatenate([jnp.where(lane_head == h, ql, zero) for h in range(HGRN_HEADS)], axis=0)
            res = _dot_nt(qs, kl)
            for h in range(HGRN_HEADS):
                for k, g in enumerate(wanted):
                    r0 = (h * len(wanted) + k) * 8
                    add_part(u, h, g, res[r0:r0 + 8] * lmask_ref[lvl, h * c + 8 * g:h * c + 8 * g + 8, :])
    scores = [jnp.concatenate([parts[u][h][g] for h in range(HGRN_HEADS) for g in range(groups)], axis=0)
              for u in range(len(units))]
    rs = [_dot(scores[u].astype(BF16), unit[3]) for u, unit in enumerate(units)]
    inters = [_dot_nt((unit[0] * jnp.exp(bs[u])).astype(BF16), unit[5].astype(BF16)) for u, unit in enumerate(units)]
    upds = [_dot_tn(unit[3], (unit[1] * jnp.exp(bs[u][c - 1:c, :] - bs[u])).astype(BF16))
            for u, unit in enumerate(units)]
    outs = []
    for u in range(len(units)):
        o = inters[u]
        for h in range(HGRN_HEADS):
            o = o + jnp.where(lane_head == h, rs[u][h * c:(h + 1) * c], 0.0)
        outs.append(o)
    sss = [_head_sumsq(o, bd) for o in outs]
    res = []
    for u, unit in enumerate(units):
        st = unit[5] * jnp.exp(bs[u][c - 1:c, :]) + jnp.where(bd > 0, upds[u], 0.0)
        y = outs[u] * lax.rsqrt(sss[u] * (1.0 / HGRN_KDIM) + EPS) * unit[4]
        res.append((y, st))
    return res


def _hgrn_kernel(zh_ref, lb_ref, gw_ref, m3_ref, lmask_ref, bd_ref, o_ref, st_ref):
    i = pl.program_id(1)
    nb, ts = zh_ref.shape[0], zh_ref.shape[1]
    c = HGRN_CHUNK
    w = HGRN_WIDTH

    @pl.when(i == 0)
    def _():
        st_ref[...] = jnp.zeros(st_ref.shape, F32)

    one_minus_lb = 1.0 - lb_ref[...]
    gw = gw_ref[...]
    bd = bd_ref[...]
    m3 = m3_ref[...]

    def chunk(ci, carry):
        rows = pl.ds(pl.multiple_of(ci * c, c), c)
        units = []
        for s in range(nb):
            q, f, v, g = [zh_ref[s, rows, j * w:(j + 1) * w] for j in range(4)]
            kk = one_minus_lb * _sigmoid(-f)
            lf = jnp.log(1.0 - jnp.minimum(kk, MAX_ONE_MINUS_F))
            hi = lf.astype(BF16)
            r1 = lf - hi.astype(F32)
            mid = r1.astype(BF16)
            lo = (r1 - mid.astype(F32)).astype(BF16)
            units.append((_silu(q), kk, jnp.concatenate([hi, mid, lo], axis=0), v.astype(BF16), gw * _silu(g),
                          st_ref[s]))
        for s, (y, st) in enumerate(_hgrn_chunks(units, bd, m3, lmask_ref)):
            st_ref[s] = st
            o_ref[s, rows, :] = y.astype(o_ref.dtype)
        return carry

    lax.fori_loop(0, ts // c, chunk, 0)


def _hgrn_mixer(zh, lb, norm_w, consts):
    b, s, _ = zh.shape
    w = HGRN_WIDTH
    ts = min(HGRN_TILE, s)
    nb = HGRN_SEQS if b % HGRN_SEQS == 0 else 1
    m3, lmask, bd = consts
    full = lambda a: pl.BlockSpec(a.shape, lambda bi, i: (0,) * a.ndim)
    return pl.pallas_call(
        _hgrn_kernel,
        grid=(b // nb, s // ts),
        in_specs=[pl.BlockSpec((nb, ts, 4 * w), lambda bi, i: (bi, i, 0)),
                  pl.BlockSpec((1, w), lambda bi, i: (0, 0)),
                  pl.BlockSpec((1, w), lambda bi, i: (0, 0)),
                  full(m3), full(lmask), full(bd)],
        out_specs=pl.BlockSpec((nb, ts, w), lambda bi, i: (bi, i, 0)),
        out_shape=jax.ShapeDtypeStruct((b, s, w), BF16),
        scratch_shapes=[pltpu.VMEM((nb, w, w), F32)],
        compiler_params=_cparams(("parallel", "arbitrary")),
        name="hgrn_mixer",
    )(zh, lb.reshape(1, w), norm_w.reshape(1, w), m3, lmask, bd)


def _attn_bias():
    qi = np.arange(WINDOW)[:, None]
    kj = np.arange(2 * WINDOW)[None, :]
    dist = qi + WINDOW - kj
    valid = (dist >= 0) & (dist < WINDOW)
    slopes = np.exp2(-8.0 * np.arange(1, ATT_HEADS + 1) / ATT_HEADS)
    bias = np.where(valid[None], -slopes[:, None, None] * dist[None] * LOG2E, NEG)
    first = np.where(kj[None] < WINDOW, NEG, bias)
    return jnp.asarray(np.concatenate([bias, first]), dtype=F32)


def _attn_kernel(sink_ref, zq_ref, zkv_ref, qw_ref, kw_ref, bias_ref, bdq_ref, bdk_ref, o_ref,
                 qbuf, kbuf, vbuf):
    i = pl.program_id(1)
    ts = zq_ref.shape[1]
    hw = 2 * HEAD_DIM

    @pl.when(i == 0)
    def _():
        kbuf[:, 0:WINDOW, :] = jnp.zeros((4, WINDOW, hw), BF16)
        vbuf[:, 0:WINDOW, :] = jnp.zeros((4, WINDOW, hw), BF16)

    q = zq_ref[0]
    ssq = _head_sumsq(q, bdq_ref[...])
    qbuf[...] = (q * lax.rsqrt(ssq * (1.0 / HEAD_DIM) + EPS) * (qw_ref[...] * (HEAD_DIM ** -0.5 * LOG2E))).astype(BF16)
    kv = zkv_ref[0]
    k = kv[:, :KV_WIDTH]
    v = kv[:, KV_WIDTH:]
    ssk = _head_sumsq(k, bdk_ref[...])
    kn = k * lax.rsqrt(ssk * (1.0 / HEAD_DIM) + EPS) * kw_ref[...]
    kr = pltpu.roll(kn, HEAD_DIM, 1)
    vr = pltpu.roll(v, HEAD_DIM, 1)
    low = lax.broadcasted_iota(jnp.int32, (1, hw), 1) < HEAD_DIM
    for j in range(ATT_KV_HEADS):
        for half in range(2):
            keep = low if half == 0 else jnp.logical_not(low)
            ksrc = kn if j == half else kr
            vsrc = v if j == half else vr
            kbuf[2 * j + half, WINDOW:WINDOW + ts, :] = jnp.where(keep, ksrc, 0.0).astype(BF16)
            vbuf[2 * j + half, WINDOW:WINDOW + ts, :] = jnp.where(keep, vsrc, 0.0).astype(BF16)

    def block(n, carry):
        r0 = pl.multiple_of(n * WINDOW, WINDOW)
        table = jnp.where(jnp.logical_and(i == 0, n == 0), ATT_HEADS, 0)
        for g0 in range(0, ATT_HEADS, ATT_GROUP):
            heads = range(g0, g0 + ATT_GROUP)
            logits = []
            for h in heads:
                hp, half = h // 2, h % 2
                j = h // (ATT_HEADS // ATT_KV_HEADS)
                qp = qbuf[pl.ds(r0, WINDOW), hp * hw:(hp + 1) * hw]
                keys = kbuf[2 * j + half, pl.ds(r0, 2 * WINDOW), :]
                logits.append(_dot_nt(qp, keys) + bias_ref[table + h])
            ps, scales = [], []
            for h, lg in zip(heads, logits):
                sink = sink_ref[h] * LOG2E
                m = jnp.maximum(jnp.max(lg, axis=-1, keepdims=True), sink)
                p = jnp.exp2(lg - m)
                scales.append(1.0 / (jnp.sum(p, axis=-1, keepdims=True) + jnp.exp2(sink - m)))
                ps.append(p.astype(BF16))
            outs = []
            for h, p in zip(heads, ps):
                half = h % 2
                j = h // (ATT_HEADS // ATT_KV_HEADS)
                vals = vbuf[2 * j + half, pl.ds(r0, 2 * WINDOW), :]
                outs.append(_dot(p, vals))
            for k in range(0, ATT_GROUP, 2):
                hp = (g0 + k) // 2
                acc = outs[k] * scales[k] + outs[k + 1] * scales[k + 1]
                o_ref[0, pl.ds(r0, WINDOW), hp * hw:(hp + 1) * hw] = acc.astype(o_ref.dtype)
        return carry

    lax.fori_loop(0, ts // WINDOW, block, 0)
    kbuf[:, 0:WINDOW, :] = kbuf[:, ts:ts + WINDOW, :]
    vbuf[:, 0:WINDOW, :] = vbuf[:, ts:ts + WINDOW, :]


def _attn_mixer(zq, zkv, q_norm_w, k_norm_w, sinks, consts):
    b, s, _ = zq.shape
    ts = min(ATT_TILE, s)
    bias, bdq, bdk = consts
    hw = 2 * HEAD_DIM
    qw = jnp.tile(q_norm_w, ATT_HEADS).reshape(1, ATT_WIDTH)
    kw = jnp.tile(k_norm_w, ATT_KV_HEADS).reshape(1, KV_WIDTH)
    full = lambda a: pl.BlockSpec(a.shape, lambda bi, i, sk: (0,) * a.ndim)
    grid_spec = pltpu.PrefetchScalarGridSpec(
        num_scalar_prefetch=1,
        grid=(b, s // ts),
        in_specs=[pl.BlockSpec((1, ts, ATT_WIDTH), lambda bi, i, sk: (bi, i, 0)),
                  pl.BlockSpec((1, ts, 2 * KV_WIDTH), lambda bi, i, sk: (bi, i, 0)),
                  full(qw), full(kw), full(bias), full(bdq), full(bdk)],
        out_specs=pl.BlockSpec((1, ts, ATT_WIDTH), lambda bi, i, sk: (bi, i, 0)),
        scratch_shapes=[pltpu.VMEM((ts, ATT_WIDTH), BF16),
                        pltpu.VMEM((4, WINDOW + ts, hw), BF16),
                        pltpu.VMEM((4, WINDOW + ts, hw), BF16)],
    )
    return pl.pallas_call(
        _attn_kernel,
        grid_spec=grid_spec,
        out_shape=jax.ShapeDtypeStruct((b, s, ATT_WIDTH), BF16),
        compiler_params=_cparams(("parallel", "arbitrary")),
        name="swa_mixer",
    )(sinks, zq, zkv, qw, kw, bias, bdq, bdk)


def _router_constants(tm):
    t = np.arange(tm)
    before = (t[:, None] < t[None, :]).astype(np.float32)
    e = np.arange(N_EXPERTS)
    lower = (e[None, :] < e[:, None]).astype(np.float32)
    return (jnp.asarray(before, dtype=BF16), jnp.asarray(np.ones((tm, tm), np.float32), dtype=BF16),
            jnp.asarray(lower, dtype=BF16))


def _route(sel, scores):
    tm = sel.shape[1]
    group_scores = []
    for g in range(N_GROUPS):
        rows = [sel[g * EXPERTS_PER_GROUP + a:g * EXPERTS_PER_GROUP + a + 1] for a in range(EXPERTS_PER_GROUP)]
        best_pair = None
        for a in range(EXPERTS_PER_GROUP):
            for bb in range(a + 1, EXPERTS_PER_GROUP):
                pair = rows[a] + rows[bb]
                best_pair = pair if best_pair is None else jnp.maximum(best_pair, pair)
        group_scores.append(best_pair)
    top = functools.reduce(jnp.maximum, group_scores)
    best = jnp.full((1, tm), N_GROUPS - 1, jnp.int32)
    for g in reversed(range(N_GROUPS - 1)):
        best = jnp.where(group_scores[g] == top, g, best)
    row = lax.broadcasted_iota(jnp.int32, (N_EXPERTS, tm), 0)
    cand = jnp.where(row // EXPERTS_PER_GROUP == best, sel, NEG)
    m1 = jnp.max(cand, axis=0, keepdims=True)
    i1 = jnp.min(jnp.where(cand == m1, row, N_EXPERTS), axis=0, keepdims=True)
    oh1 = row == i1
    cand = jnp.where(oh1, NEG, cand)
    m2 = jnp.max(cand, axis=0, keepdims=True)
    i2 = jnp.min(jnp.where(cand == m2, row, N_EXPERTS), axis=0, keepdims=True)
    oh2 = row == i2
    s1 = jnp.sum(jnp.where(oh1, scores, 0.0), axis=0, keepdims=True)
    s2 = jnp.sum(jnp.where(oh2, scores, 0.0), axis=0, keepdims=True)
    return oh1, oh2, s1 / (s1 + s2), s2 / (s1 + s2)


def _out_kernel(yp_ref, yh_ref, ya_ref, x_ref, g1_ref, sc_ref, sh_ref, nw_ref, wo_ref, rwt_ref, rb_ref,
                before_ref, ones_ref, lower_ref, x1_ref, h2_ref, ls_ref, gate_ref, cnt_ref):
    tm = MOE_TOK_TILE
    subs = [pl.ds(u * tm, tm) for u in range(x_ref.shape[0] // tm)]
    p0, p1 = POOL_WIDTH, POOL_WIDTH + HGRN_WIDTH
    h2s = []
    for rows in subs:
        mix = (_dot(yp_ref[rows, :], wo_ref[0:p0, :]) + _dot(yh_ref[rows, :], wo_ref[p0:p1, :])
               + _dot(ya_ref[rows, :], wo_ref[p1:, :]))
        x1 = x_ref[rows, :] + g1_ref[0] * mix
        x1_ref[rows, :] = x1
        ms = jnp.mean(x1 * x1, axis=-1, keepdims=True)
        h2 = ((x1 * lax.rsqrt(ms + EPS) * nw_ref[...]) * (1.0 + sc_ref[0]) + sh_ref[0]).astype(BF16)
        h2_ref[rows, :] = h2
        h2s.append(h2)
    logits = [_dot_nt(rwt_ref[...], h2) for h2 in h2s]
    picks = []
    for lg in logits:
        ex = jnp.exp(lg - jnp.max(lg, axis=0, keepdims=True))
        scores = ex / jnp.sum(ex, axis=0, keepdims=True)
        picks.append(_route(scores + rb_ref[...], scores))
    chosen = [jnp.where(jnp.logical_or(oh1, oh2), 1.0, 0.0).astype(BF16) for oh1, oh2, _, _ in picks]
    ranks = [_dot(ch, before_ref[...]) for ch in chosen]
    counts = [_dot(ch, ones_ref[...]) for ch in chosen]
    aligned = [(jnp.floor((cn + (ROW_ALIGN - 1)) * (1.0 / ROW_ALIGN)) * ROW_ALIGN).astype(BF16) for cn in counts]
    slots = [_dot(lower_ref[...], al) + rk for al, rk in zip(aligned, ranks)]
    for u, rows in enumerate(subs):
        oh1, oh2, w1, w2 = picks[u]
        gate_ref[:, rows] = jnp.concatenate([w1, w2], axis=0)
        ls_ref[:, rows] = jnp.concatenate([jnp.sum(jnp.where(oh1, slots[u], 0.0), axis=0, keepdims=True),
                                           jnp.sum(jnp.where(oh2, slots[u], 0.0), axis=0, keepdims=True)], axis=0)
        cnt_ref[u] = counts[u][:, 0:128]


def _out_proj_router(yp, yh, ya, x, g1, sc, sh, nw, wo_b, layer, rwt_b, rb, consts):
    b, s, d = x.shape
    t = b * s
    tm = MOE_TOK_TILE
    n_sub = ROUTER_SUBTILES if s % (ROUTER_SUBTILES * tm) == 0 else 1
    ts = n_sub * tm
    per_batch = s // ts
    nt = t // tm
    before, ones, lower = consts
    tok = lambda w: pl.BlockSpec((ts, w), lambda i: (i, 0))
    vec = pl.BlockSpec((1, 1, d), lambda i: (i // per_batch, 0, 0))
    full = lambda a: pl.BlockSpec(a.shape, lambda i: (0,) * a.ndim)
    lanes = pl.BlockSpec((2, ts), lambda i: (0, i))
    return pl.pallas_call(
        _out_kernel,
        grid=(t // ts,),
        in_specs=[tok(POOL_WIDTH), tok(HGRN_WIDTH), tok(ATT_WIDTH), tok(d), vec, vec, vec,
                  pl.BlockSpec((1, d), lambda i: (0, 0)),
                  pl.BlockSpec((None,) + wo_b.shape[1:], lambda i: (layer, 0, 0)), full(rwt_b),
                  pl.BlockSpec((N_EXPERTS, 1), lambda i: (0, 0)), full(before), full(ones), full(lower)],
        out_specs=[tok(d), tok(d), lanes, lanes, pl.BlockSpec((n_sub, N_EXPERTS, 128), lambda i: (i, 0, 0))],
        out_shape=[jax.ShapeDtypeStruct((t, d), F32), jax.ShapeDtypeStruct((t, d), BF16),
                   jax.ShapeDtypeStruct((2, t), F32), jax.ShapeDtypeStruct((2, t), F32),
                   jax.ShapeDtypeStruct((nt, N_EXPERTS, 128), F32)],
        compiler_params=_cparams(("parallel",)),
        name="out_proj_router",
    )(yp.reshape(t, -1), yh.reshape(t, -1), ya.reshape(t, -1), x.reshape(t, d),
      g1.reshape(b, 1, d), sc.reshape(b, 1, d), sh.reshape(b, 1, d), nw.reshape(1, d), wo_b, rwt_b,
      rb.reshape(N_EXPERTS, 1), before, ones, lower)


def _moe_tables(cnt_out):
    cnt = jnp.round(cnt_out[:, :, 0]).astype(jnp.int32)
    cnt = (cnt + ROW_ALIGN - 1) // ROW_ALIGN * ROW_ALIGN
    total = jnp.sum(cnt, axis=0)
    padded = (total + EXPERT_TILE - 1) // EXPERT_TILE * EXPERT_TILE
    ends = jnp.cumsum(padded)
    first = ends - padded
    start = first[None, :] + jnp.cumsum(cnt, axis=0) - cnt
    loff = jnp.cumsum(cnt, axis=1) - cnt
    n_tiles = _moe_rows(cnt.shape[0] * MOE_TOK_TILE) // EXPERT_TILE
    n_used = ends[-1] // EXPERT_TILE
    tile_row = jnp.minimum(jnp.arange(n_tiles), n_used - 1) * EXPERT_TILE
    tile_expert = jnp.sum((ends[None, :] <= tile_row[:, None]).astype(jnp.int32), axis=1)
    pieces = jnp.sum(cnt, axis=1, keepdims=True) // ROW_ALIGN

    def copies(n_per_expert, offset_in_chunk, n_max):
        incl = jnp.cumsum(n_per_expert, axis=1)
        k = jnp.arange(n_max, dtype=jnp.int32)
        owner = jnp.sum((incl[:, None, :] <= k[None, :, None]).astype(jnp.int32), axis=2)
        owner = jnp.minimum(owner, N_EXPERTS - 1)
        pick = lambda a: jnp.take_along_axis(a, owner, axis=1)
        off = offset_in_chunk(k[None, :] - pick(incl - n_per_expert), pick(cnt))
        return pick(loff) + off, pick(start) + off, incl[:, -1:]

    big_src, big_dst, n_big = copies(cnt // (2 * ROW_ALIGN), lambda j, c: j * (2 * ROW_ALIGN), MAX_PIECES // 2)
    small_src, small_dst, n_small = copies(cnt % (2 * ROW_ALIGN) // ROW_ALIGN, lambda j, c: c - ROW_ALIGN, N_EXPERTS)
    moves = jnp.concatenate([big_src, big_dst, small_src, small_dst, n_big, n_small, pieces], axis=1)
    free_pieces = (n_tiles - n_used) * (EXPERT_TILE // ZERO_ROWS)
    tail = jnp.concatenate([first + total, padded - total, jnp.stack([ends[-1], free_pieces])])
    return (moves.reshape(-1).astype(jnp.int32), tail.astype(jnp.int32), tile_expert.astype(jnp.int32),
            n_used.reshape(1).astype(jnp.int32))


MOVE_BIG_SRC = 0
MOVE_BIG_DST = MOVE_BIG_SRC + MAX_PIECES // 2
MOVE_SMALL_SRC = MOVE_BIG_DST + MAX_PIECES // 2
MOVE_SMALL_DST = MOVE_SMALL_SRC + N_EXPERTS
MOVE_N_BIG = MOVE_SMALL_DST + N_EXPERTS
MOVE_N_SMALL = MOVE_N_BIG + 1
MOVE_PIECES = MOVE_N_SMALL + 1
MOVE_WIDTH = MOVE_PIECES + 1


def _tile_copies(moves_ref, tile, make_copy):
    base = tile * MOVE_WIDTH
    for src0, dst0, n_at, size in ((MOVE_BIG_SRC, MOVE_BIG_DST, MOVE_N_BIG, 2 * ROW_ALIGN),
                                   (MOVE_SMALL_SRC, MOVE_SMALL_DST, MOVE_N_SMALL, ROW_ALIGN)):
        def body(j, carry):
            make_copy(pl.multiple_of(moves_ref[base + src0 + j], ROW_ALIGN),
                      pl.multiple_of(moves_ref[base + dst0 + j], ROW_ALIGN), size).start()
            return carry

        lax.fori_loop(0, moves_ref[base + n_at], body, 0)


def _moe_rows(n_tokens):
    per_tile = 2 * MOE_TOK_TILE + N_EXPERTS * (ROW_ALIGN - 1)
    rows = (n_tokens // MOE_TOK_TILE) * per_tile + N_EXPERTS * (EXPERT_TILE - ROW_ALIGN)
    return (rows + EXPERT_TILE - 1) // EXPERT_TILE * EXPERT_TILE


def _chunk_copies(count, make_copy, wait=False):
    for size in CHUNK_SIZES:
        offset = jnp.bitwise_and(count, ~(2 * size - 1))

        @pl.when(jnp.bitwise_and(count, size) != 0)
        def _():
            copy = make_copy(pl.multiple_of(offset, ROW_ALIGN), size)
            copy.wait() if wait else copy.start()


def _wait_rows(n_pieces, make_copy):
    for p in WAIT_PIECES:
        @pl.when(jnp.bitwise_and(n_pieces, p) != 0)
        def _():
            make_copy(p * ROW_ALIGN).wait()


def _slot_rows(tm):
    return lax.broadcasted_iota(jnp.int32, (MOE_SLOTS, tm), 0).astype(F32)


def _sort_kernel(moves_ref, tail_ref, h2_ref, ls_ref, xs_ref, buf_ref, sem_ref):
    i = pl.program_id(0)
    n = pl.num_programs(0)
    tm = h2_ref.shape[0]
    slot = i % 2

    def sent(tile, s):
        _wait_rows(moves_ref[tile * MOVE_WIDTH + MOVE_PIECES], lambda rows: pltpu.make_async_copy(
            buf_ref.at[s, pl.ds(0, rows), :], xs_ref.at[pl.ds(0, rows), :], sem_ref.at[s]))

    @pl.when(i >= 2)
    def _():
        sent(i - 2, slot)

    ls = ls_ref[...]
    srow = _slot_rows(tm)
    perm = jnp.where(jnp.logical_or(srow == ls[0:1], srow == ls[1:2]), 1.0, 0.0).astype(BF16)
    buf_ref[slot] = _dot(perm, h2_ref[...]).astype(BF16)

    _tile_copies(moves_ref, i, lambda src, dst, size: pltpu.make_async_copy(
        buf_ref.at[slot, pl.ds(src, size), :], xs_ref.at[pl.ds(dst, size), :], sem_ref.at[slot]))

    @pl.when(i == n - 1)
    def _():
        @pl.when(n > 1)
        def _():
            sent(i - 1, 1 - slot)
        sent(i, slot)

        buf_ref[0] = jnp.zeros(buf_ref.shape[1:], BF16)
        zeros_to = lambda row, size: pltpu.make_async_copy(
            buf_ref.at[0, pl.ds(0, size), :], xs_ref.at[pl.ds(row, size), :], sem_ref.at[0])

        for wait in (False, True):
            def expert_tail(e, carry):
                st = pl.multiple_of(tail_ref[e], ROW_ALIGN)
                _chunk_copies(tail_ref[N_EXPERTS + e], lambda off, size: zeros_to(st + off, size), wait)
                return carry

            def free_tile(j, carry):
                copy = zeros_to(pl.multiple_of(tail_ref[2 * N_EXPERTS] + j * ZERO_ROWS, ROW_ALIGN), ZERO_ROWS)
                copy.wait() if wait else copy.start()
                return carry

            lax.fori_loop(0, N_EXPERTS, expert_tail, 0)
            lax.fori_loop(0, tail_ref[2 * N_EXPERTS + 1], free_tile, 0)


def _sort_tokens(h2, ls, moves, tail, n_rows):
    t, d = h2.shape
    tm = MOE_TOK_TILE
    grid_spec = pltpu.PrefetchScalarGridSpec(
        num_scalar_prefetch=2,
        grid=(t // tm,),
        in_specs=[pl.BlockSpec((tm, d), lambda i, *_: (i, 0)),
                  pl.BlockSpec((2, tm), lambda i, *_: (0, i))],
        out_specs=pl.BlockSpec(memory_space=pl.ANY),
        scratch_shapes=[pltpu.VMEM((2, MOE_SLOTS, d), BF16), pltpu.SemaphoreType.DMA((2,))],
    )
    return pl.pallas_call(
        _sort_kernel,
        grid_spec=grid_spec,
        out_shape=jax.ShapeDtypeStruct((n_rows, d), BF16),
        compiler_params=_cparams(("arbitrary",)),
        name="moe_sort",
    )(moves, tail, h2, ls)


def _gmm_kernel(te_ref, nu_ref, xs_ref, wg_ref, wu_ref, wd_ref, ys_ref, wgb, wub, wdb):
    i = pl.program_id(0)
    used = i < nu_ref[0]
    new_expert = jnp.logical_or(i == 0, te_ref[i] != te_ref[jnp.maximum(i - 1, 0)])

    @pl.when(jnp.logical_and(used, new_expert))
    def _():
        wgb[...] = wg_ref[...].astype(BF16)
        wub[...] = wu_ref[...].astype(BF16)
        wdb[...] = wd_ref[...].astype(BF16)

    @pl.when(used)
    def _():
        x = xs_ref[...]
        he = _silu(_dot(x, wgb[...])) * _dot(x, wub[...])
        ys_ref[...] = _dot(he.astype(BF16), wdb[...]).astype(BF16)

    @pl.when(jnp.logical_not(used))
    def _():
        ys_ref[...] = jnp.zeros(ys_ref.shape, BF16)


def _grouped_mlp(xs, tile_expert, n_used, wg, wu, wd, layer):
    n_rows, d = xs.shape
    te = EXPERT_TILE
    row_map = lambda i, tex, nu: (jnp.minimum(i, nu[0] - 1), 0)
    grid_spec = pltpu.PrefetchScalarGridSpec(
        num_scalar_prefetch=2,
        grid=(n_rows // te,),
        in_specs=[pl.BlockSpec((te, d), row_map),
                  pl.BlockSpec((None, None, d, D_EXPERT), lambda i, tex, nu: (layer, tex[i], 0, 0)),
                  pl.BlockSpec((None, None, d, D_EXPERT), lambda i, tex, nu: (layer, tex[i], 0, 0)),
                  pl.BlockSpec((None, None, D_EXPERT, d), lambda i, tex, nu: (layer, tex[i], 0, 0))],
        out_specs=pl.BlockSpec((te, d), lambda i, tex, nu: (i, 0)),
        scratch_shapes=[pltpu.VMEM((d, D_EXPERT), BF16), pltpu.VMEM((d, D_EXPERT), BF16),
                        pltpu.VMEM((D_EXPERT, d), BF16)],
    )
    return pl.pallas_call(
        _gmm_kernel,
        grid_spec=grid_spec,
        out_shape=jax.ShapeDtypeStruct((n_rows, d), BF16),
        compiler_params=_cparams(("arbitrary",)),
        name="moe_grouped_mlp",
    )(tile_expert, n_used, xs, wg, wu, wd)


def _combine_kernel(moves_ref, ls_ref, gate_ref, x1_ref, g2_ref, ys_ref, o_ref, buf_ref, sem_ref):
    i = pl.program_id(0)
    n = pl.num_programs(0)
    tm = x1_ref.shape[0]
    slot = i % 2
    n_pieces = moves_ref[i * MOVE_WIDTH + MOVE_PIECES]

    def fetch(tile, s):
        _tile_copies(moves_ref, tile, lambda dst, src, size: pltpu.make_async_copy(
            ys_ref.at[pl.ds(src, size), :], buf_ref.at[s, pl.ds(dst, size), :], sem_ref.at[s]))

    @pl.when(i == 0)
    def _():
        fetch(0, 0)

    @pl.when(i + 1 < n)
    def _():
        fetch(i + 1, 1 - slot)

    _wait_rows(n_pieces, lambda rows: pltpu.make_async_copy(
        ys_ref.at[pl.ds(0, rows), :], buf_ref.at[slot, pl.ds(0, rows), :], sem_ref.at[slot]))

    srow = _slot_rows(tm)
    ls = ls_ref[...]
    gate = gate_ref[...]
    weights = (jnp.where(srow == ls[0:1], gate[0:1], 0.0)
               + jnp.where(srow == ls[1:2], gate[1:2], 0.0)).astype(BF16)
    written = lax.broadcasted_iota(jnp.int32, (MOE_SLOTS, 1), 0) < n_pieces * ROW_ALIGN
    rows = jnp.where(written, buf_ref[slot], jnp.zeros((), BF16))
    y = _dot_tn(weights, rows)
    o_ref[...] = x1_ref[...] + g2_ref[0] * y


def _combine(ys, ls, gate, x1, g2, moves, per_batch):
    t, d = x1.shape
    tm = MOE_TOK_TILE
    grid_spec = pltpu.PrefetchScalarGridSpec(
        num_scalar_prefetch=1,
        grid=(t // tm,),
        in_specs=[pl.BlockSpec((2, tm), lambda i, *_: (0, i)),
                  pl.BlockSpec((2, tm), lambda i, *_: (0, i)),
                  pl.BlockSpec((tm, d), lambda i, *_: (i, 0)),
                  pl.BlockSpec((1, 1, d), lambda i, *_: (i // per_batch, 0, 0)),
                  pl.BlockSpec(memory_space=pl.ANY)],
        out_specs=pl.BlockSpec((tm, d), lambda i, *_: (i, 0)),
        scratch_shapes=[pltpu.VMEM((2, MOE_SLOTS, d), BF16), pltpu.SemaphoreType.DMA((2,))],
    )
    return pl.pallas_call(
        _combine_kernel,
        grid_spec=grid_spec,
        out_shape=jax.ShapeDtypeStruct((t, d), F32),
        compiler_params=_cparams(("arbitrary",)),
        name="moe_combine",
    )(moves, ls, gate, x1, g2.reshape(-1, 1, d), ys)


def kernel(x, c, ada_w, ada_b, norm1_w, norm2_w, w_in, pool_w, pool_scale, hgrn_lb_raw, hgrn_norm_w, q_norm_w,
           k_norm_w, attn_sinks, w_out, router_w, router_bias, expert_w_gate, expert_w_up, expert_w_down):
    b, s, d = x.shape
    depth = ada_w.shape[0]
    t = b * s
    per_batch = s // MOE_TOK_TILE
    n_rows = _moe_rows(t)

    p = jax.nn.softmax(hgrn_lb_raw.astype(F32), axis=0)
    lower_bounds = jnp.maximum(jnp.cumsum(p, axis=0) - p[0:1], 0.0)

    mod = _modulation(c, ada_w, ada_b)
    hgrn_consts = _hgrn_constants() + (_head_ones(HGRN_WIDTH, HGRN_KDIM),)
    attn_consts = (_attn_bias(), _head_ones(ATT_WIDTH, HEAD_DIM), _head_ones(KV_WIDTH, HEAD_DIM))
    router_consts = _router_constants(MOE_TOK_TILE)
    rwt_b = router_w.T.astype(BF16)
    w_in_b = w_in.astype(BF16)
    w_out_b = w_out.astype(BF16)

    for l in range(depth):
        sh1, sc1, g1, sh2, sc2, g2 = [mod[l, :, j * d:(j + 1) * d] for j in range(6)]
        yp, zh, zq, zkv = _in_proj(x, sc1, sh1, norm1_w[l], w_in_b, l, _block_diag(pool_w[l]).astype(BF16),
                                   pool_scale[l])
        yh = _hgrn_mixer(zh, lower_bounds[l], hgrn_norm_w[l], hgrn_consts)
        ya = _attn_mixer(zq, zkv, q_norm_w[l], k_norm_w[l], attn_sinks[l], attn_consts)
        x1, h2, ls, gate, cnt_out = _out_proj_router(
            yp, yh, ya, x, g1, sc2, sh2, norm2_w[l], w_out_b, l, rwt_b, router_bias, router_consts)
        moves, tail, tile_expert, n_used = _moe_tables(cnt_out)
        xs = _sort_tokens(h2, ls, moves, tail, n_rows)
        ys = _grouped_mlp(xs, tile_expert, n_used, expert_w_gate, expert_w_up, expert_w_down, l)
        x = _combine(ys, ls, gate, x1, g2, moves, per_batch).reshape(b, s, d)
    return x
```

```python
import functools

import numpy as np
import jax
import jax.numpy as jnp
from jax import lax
from jax.experimental import pallas as pl
from jax.experimental.pallas import tpu as pltpu

F32 = jnp.float32
BF16 = jnp.bfloat16

D_MODEL = 1024
POOL_WINDOWS = (2, 4, 8, 16)
POOL_WIDTH = 256
POOL_GROUP = 64
POOL_HALO = 32
HGRN_HEADS = 4
HGRN_KDIM = 64
HGRN_WIDTH = 256
HGRN_CHUNK = 64
ATT_HEADS = 8
ATT_KV_HEADS = 2
HEAD_DIM = 64
ATT_WIDTH = 512
KV_WIDTH = 128
WINDOW = 128
N_EXPERTS = 16
N_GROUPS = 4
EXPERTS_PER_GROUP = 4
D_EXPERT = 512
EPS = 1e-6
MAX_ONE_MINUS_F = 1.0 - 1e-6
LOG2E = 1.4426950408889634
NEG = -1e30

VMEM_LIMIT = 48 * 1024 * 1024

IN_TILE = 1024
ATT_TILE = 1024
MOE_TOK_TILE = 256
ROW_ALIGN = 16
MOE_SLOTS = 768
EXPERT_TILE = 1024
ZERO_ROWS = 512
CHUNK_SIZES = (512, 256, 128, 64, 32, 16)
MAX_PIECES = MOE_SLOTS // ROW_ALIGN
WAIT_PIECES = (32, 16, 8, 4, 2, 1)


def _sigmoid(x):
    return 1.0 / (1.0 + jnp.exp(-x))


def _silu(x):
    return x * _sigmoid(x)


def _cparams(sem, **kw):
    return pltpu.CompilerParams(dimension_semantics=sem, vmem_limit_bytes=VMEM_LIMIT, **kw)


def _dot(a, b):
    return jnp.dot(a, b, preferred_element_type=F32)


def _dot_nt(a, b):
    return lax.dot_general(a, b, (((1,), (1,)), ((), ())), preferred_element_type=F32)


def _dot_tn(a, b):
    return lax.dot_general(a, b, (((0,), (0,)), ((), ())), preferred_element_type=F32)


def _head_sumsq(x, bd):
    return _dot((x * x).astype(BF16), bd)


def _mod_kernel(c_ref, w_ref, b_ref, o_ref):
    cond = _silu(c_ref[...])
    o_ref[0] = _dot(cond.astype(BF16), w_ref[0].astype(BF16)) + b_ref[0]


def _modulation(c, ada_w, ada_b):
    depth, d, n = ada_w.shape
    b = c.shape[0]
    nb = n // d
    return pl.pallas_call(
        _mod_kernel,
        grid=(depth, nb),
        in_specs=[pl.BlockSpec((b, d), lambda l, j: (0, 0)),
                  pl.BlockSpec((1, d, d), lambda l, j: (l, 0, j)),
                  pl.BlockSpec((1, 1, d), lambda l, j: (l, 0, j))],
        out_specs=pl.BlockSpec((1, b, d), lambda l, j: (l, 0, j)),
        out_shape=jax.ShapeDtypeStruct((depth, b, n), F32),
        compiler_params=_cparams(("parallel", "parallel")),
        name="adaln_mod",
    )(c, ada_w, ada_b.reshape(depth, 1, n))


def _in_kernel(x_ref, sc_ref, sh_ref, nw_ref, w_ref, pw_ref, ps_ref, yp_ref, zh_ref, zq_ref, zkv_ref,
               buf_ref, sa_ref, sb_ref):
    x = x_ref[0]
    ms = jnp.mean(x * x, axis=-1, keepdims=True)
    h = (x * lax.rsqrt(ms + EPS) * nw_ref[...]) * (1.0 + sc_ref[0]) + sh_ref[0]
    z = _dot(h.astype(BF16), w_ref[...])
    zh_ref[0] = z[:, POOL_WIDTH:POOL_WIDTH + 4 * HGRN_WIDTH]
    zq_ref[0] = z[:, POOL_WIDTH + 4 * HGRN_WIDTH:POOL_WIDTH + 4 * HGRN_WIDTH + ATT_WIDTH]
    zkv_ref[0] = z[:, POOL_WIDTH + 4 * HGRN_WIDTH + ATT_WIDTH:]
    yp_ref[0] = _pool(z[:, :POOL_WIDTH], pl.program_id(1), pw_ref, ps_ref, buf_ref, sa_ref, sb_ref).astype(yp_ref.dtype)


def _in_proj(x, sc, sh, nw, w_in_b, layer, pool_bd, pool_scale):
    b, s, d = x.shape
    n = w_in_b.shape[2]
    ts = min(IN_TILE, s)
    tok = lambda w: pl.BlockSpec((1, ts, w), lambda bi, i: (bi, i, 0))
    vec = pl.BlockSpec((1, 1, d), lambda bi, i: (bi, 0, 0))
    full = lambda a: pl.BlockSpec(a.shape, lambda bi, i: (0,) * a.ndim)
    pool_scale = pool_scale.reshape(1, POOL_WIDTH)
    outs = ((POOL_WIDTH, BF16), (4 * HGRN_WIDTH, F32), (ATT_WIDTH, F32), (2 * KV_WIDTH, F32))
    return pl.pallas_call(
        _in_kernel,
        grid=(b, s // ts),
        in_specs=[tok(d), vec, vec,
                  pl.BlockSpec((1, d), lambda bi, i: (0, 0)),
                  pl.BlockSpec((None, d, n), lambda bi, i: (layer, 0, 0)), full(pool_bd), full(pool_scale)],
        out_specs=[tok(w) for w, _ in outs],
        out_shape=[jax.ShapeDtypeStruct((b, s, w), dt) for w, dt in outs],
        scratch_shapes=[pltpu.VMEM((POOL_HALO + ts, POOL_WIDTH), F32)] * 3,
        compiler_params=_cparams(("parallel", "arbitrary")),
        name="in_proj_pool",
    )(x, sc.reshape(b, 1, d), sh.reshape(b, 1, d), nw.reshape(1, d), w_in_b, pool_bd, pool_scale)


def _pool(a, i, w_ref, scale_ref, buf_ref, sa_ref, sb_ref):
    ts = a.shape[0]
    halo = POOL_HALO
    end = halo + ts
    half = 2 * POOL_GROUP

    @pl.when(i == 0)
    def _():
        buf_ref[0:halo, :] = jnp.zeros((halo, POOL_WIDTH), F32)

    buf_ref[halo:end, :] = a
    s2 = buf_ref[8:end, :] + buf_ref[7:end - 1, :]
    sa_ref[8:end, :] = s2
    s4 = sa_ref[16:end, :] + sa_ref[14:end - 2, :]
    sb_ref[16:end, :] = s4
    s8 = sb_ref[24:end, half:] + sb_ref[20:end - 4, half:]
    sa_ref[24:end, half:] = s8
    s16 = sa_ref[halo:end, half:] + sa_ref[halo - 8:end - 8, half:]
    lane = lax.broadcasted_iota(jnp.int32, (1, POOL_WIDTH), 1)
    win = jnp.left_shift(2, lane // POOL_GROUP)
    low = lane[:, 0:half] % half < POOL_GROUP
    acc = jnp.concatenate([jnp.where(low, s2[halo - 8:, 0:half], s4[halo - 16:, 0:half]),
                           jnp.where(low, s8[halo - 24:], s16)], axis=1)
    pos = i * ts + lax.broadcasted_iota(jnp.int32, (ts, 1), 0)
    count = jnp.minimum(pos + 1, win).astype(F32)
    pooled = acc / count - a
    buf_ref[0:halo, :] = a[ts - halo:, :]
    return _dot(pooled.astype(BF16), w_ref[...]) * scale_ref[...]


def _block_diag(blocks):
    g, n, _ = blocks.shape
    eye = jnp.eye(g, dtype=blocks.dtype)
    return (eye[:, None, :, None] * blocks[:, :, None, :]).reshape(g * n, g * n)


def _head_ones(width, head):
    idx = np.arange(width) // head
    return jnp.asarray((idx[:, None] == idx[None, :]).astype(np.float32), dtype=BF16)


HGRN_LEVELS = (32, 16, 8, 4, 2, 1)
HGRN_MATMUL_LEVELS = (2, 1)
N_LEVEL_MASKS = len(HGRN_LEVELS) + 1
HGRN_SEQS = 8
HGRN_TILE = 256
ATT_GROUP = 8
ROUTER_SUBTILES = 4


def _hgrn_constants():
    c = HGRN_CHUNK
    m = np.zeros((len(HGRN_MATMUL_LEVELS) + 1, c, c), np.float32)
    masks = np.zeros((N_LEVEL_MASKS, c, c), np.float32)
    for lvl, n in enumerate(HGRN_LEVELS):
        for t in range(c):
            blk = t // (2 * n)
            mid = blk * 2 * n + n
            if t >= mid:
                masks[lvl, t, blk * 2 * n:mid] = 1.0
            if n in HGRN_MATMUL_LEVELS:
                row = m[HGRN_MATMUL_LEVELS.index(n), t]
                if t >= mid:
                    row[mid:t + 1] = 1.0
                else:
                    row[t + 1:mid] = 1.0
    for t in range(c):
        m[-1, t, :t + 1] = 1.0
        masks[-1, t, t] = 1.0
    m = m.reshape(-1, c)
    m3 = np.concatenate([m, m, m], axis=1)
    masks = np.tile(masks, (1, HGRN_HEADS, 1))
    return jnp.asarray(m3, dtype=BF16), jnp.asarray(masks, dtype=F32)


def _hgrn_chunks(units, bd, m3, lmask_ref):
    c = HGRN_CHUNK
    w = HGRN_WIDTH
    n_fine = len(HGRN_MATMUL_LEVELS)
    lane_head = lax.broadcasted_iota(jnp.int32, (1, w), 1) // HGRN_KDIM
    sums = [_dot(m3, u[2]) for u in units]
    bs = [s[n_fine * c:] for s in sums]

    def level_decay(u, n):
        if n in HGRN_MATMUL_LEVELS:
            k = HGRN_MATMUL_LEVELS.index(n)
            return jnp.exp(sums[u][k * c:(k + 1) * c])
        blocks = bs[u].reshape(c // (2 * n), 2 * n, w)
        ref = blocks[:, n - 1:n, :]
        right = lax.broadcasted_iota(jnp.int32, (1, 2 * n, 1), 1) >= n
        return jnp.exp(jnp.where(right, blocks - ref, ref - blocks).reshape(c, w))

    groups = c // 8
    parts = [[[None] * groups for _ in range(HGRN_HEADS)] for _ in units]

    def add_part(u, h, g, piece):
        parts[u][h][g] = piece if parts[u][h][g] is None else parts[u][h][g] + piece

    for lvl in range(N_LEVEL_MASKS):
        n = HGRN_LEVELS[lvl] if lvl < len(HGRN_LEVELS) else 0
        wanted = [g for g in range(groups) if (8 * g) % (2 * n) >= n] if n >= 8 else list(range(groups))
        for u, (qf, kk, _, _, _, _) in enumerate(units):
            if lvl < len(HGRN_LEVELS):
                e = level_decay(u, n)
                ql = qf * e
                kl = (kk * e).astype(BF16)
            else:
                ql = qf
                kl = kk.astype(BF16)
            ql = jnp.concatenate([ql[8 * g:8 * g + 8] for g in wanted], axis=0).astype(BF16)
            zero = jnp.zeros_like(ql)
            qs = jnp.concatenate([jnp.where(lane_head == h, ql, zero) for h in range(HGRN_HEADS)], axis=0)
            res = _dot_nt(qs, kl)
            for h in range(HGRN_HEADS):
                for k, g in enumerate(wanted):
                    r0 = (h * len(wanted) + k) * 8
                    add_part(u, h, g, res[r0:r0 + 8] * lmask_ref[lvl, h * c + 8 * g:h * c + 8 * g + 8, :])
    scores = [jnp.concatenate([parts[u][h][g] for h in range(HGRN_HEADS) for g in range(groups)], axis=0)
              for u in range(len(units))]
    rs = [_dot(scores[u].astype(BF16), unit[3]) for u, unit in enumerate(units)]
    inters = [_dot_nt((unit[0] * jnp.exp(bs[u])).astype(BF16), unit[5].astype(BF16)) for u, unit in enumerate(units)]
    upds = [_dot_tn(unit[3], (unit[1] * jnp.exp(bs[u][c - 1:c, :] - bs[u])).astype(BF16))
            for u, unit in enumerate(units)]
    outs = []
    for u in range(len(units)):
        o = inters[u]
        for h in range(HGRN_HEADS):
            o = o + jnp.where(lane_head == h, rs[u][h * c:(h + 1) * c], 0.0)
        outs.append(o)
    sss = [_head_sumsq(o, bd) for o in outs]
    res = []
    for u, unit in enumerate(units):
        st = unit[5] * jnp.exp(bs[u][c - 1:c, :]) + jnp.where(bd > 0, upds[u], 0.0)
        y = outs[u] * lax.rsqrt(sss[u] * (1.0 / HGRN_KDIM) + EPS) * unit[4]
        res.append((y, st))
    return res


def _hgrn_kernel(zh_ref, lb_ref, gw_ref, m3_ref, lmask_ref, bd_ref, o_ref, st_ref):
    i = pl.program_id(1)
    nb, ts = zh_ref.shape[0], zh_ref.shape[1]
    c = HGRN_CHUNK
    w = HGRN_WIDTH

    @pl.when(i == 0)
    def _():
        st_ref[...] = jnp.zeros(st_ref.shape, F32)

    one_minus_lb = 1.0 - lb_ref[...]
    gw = gw_ref[...]
    bd = bd_ref[...]
    m3 = m3_ref[...]

    def chunk(ci, carry):
        rows = pl.ds(pl.multiple_of(ci * c, c), c)
        units = []
        for s in range(nb):
            q, f, v, g = [zh_ref[s, rows, j * w:(j + 1) * w] for j in range(4)]
            kk = one_minus_lb * _sigmoid(-f)
            lf = jnp.log(1.0 - jnp.minimum(kk, MAX_ONE_MINUS_F))
            hi = lf.astype(BF16)
            r1 = lf - hi.astype(F32)
            mid = r1.astype(BF16)
            lo = (r1 - mid.astype(F32)).astype(BF16)
            units.append((_silu(q), kk, jnp.concatenate([hi, mid, lo], axis=0), v.astype(BF16), gw * _silu(g),
                          st_ref[s]))
        for s, (y, st) in enumerate(_hgrn_chunks(units, bd, m3, lmask_ref)):
            st_ref[s] = st
            o_ref[s, rows, :] = y.astype(o_ref.dtype)
        return carry

    lax.fori_loop(0, ts // c, chunk, 0)


def _hgrn_mixer(zh, lb, norm_w, consts):
    b, s, _ = zh.shape
    w = HGRN_WIDTH
    ts = min(HGRN_TILE, s)
    nb = HGRN_SEQS if b % HGRN_SEQS == 0 else 1
    m3, lmask, bd = consts
    full = lambda a: pl.BlockSpec(a.shape, lambda bi, i: (0,) * a.ndim)
    return pl.pallas_call(
        _hgrn_kernel,
        grid=(b // nb, s // ts),
        in_specs=[pl.BlockSpec((nb, ts, 4 * w), lambda bi, i: (bi, i, 0)),
                  pl.BlockSpec((1, w), lambda bi, i: (0, 0)),
                  pl.BlockSpec((1, w), lambda bi, i: (0, 0)),
                  full(m3), full(lmask), full(bd)],
        out_specs=pl.BlockSpec((nb, ts, w), lambda bi, i: (bi, i, 0)),
        out_shape=jax.ShapeDtypeStruct((b, s, w), BF16),
        scratch_shapes=[pltpu.VMEM((nb, w, w), F32)],
        compiler_params=_cparams(("parallel", "arbitrary")),
        name="hgrn_mixer",
    )(zh, lb.reshape(1, w), norm_w.reshape(1, w), m3, lmask, bd)


def _attn_bias():
    qi = np.arange(WINDOW)[:, None]
    kj = np.arange(2 * WINDOW)[None, :]
    dist = qi + WINDOW - kj
    valid = (dist >= 0) & (dist < WINDOW)
    slopes = np.exp2(-8.0 * np.arange(1, ATT_HEADS + 1) / ATT_HEADS)
    bias = np.where(valid[None], -slopes[:, None, None] * dist[None] * LOG2E, NEG)
    first = np.where(kj[None] < WINDOW, NEG, bias)
    return jnp.asarray(np.concatenate([bias, first]), dtype=F32)


def _attn_kernel(sink_ref, zq_ref, zkv_ref, qw_ref, kw_ref, bias_ref, bdq_ref, bdk_ref, o_ref,
                 qbuf, kbuf, vbuf):
    i = pl.program_id(1)
    ts = zq_ref.shape[1]
    hw = 2 * HEAD_DIM

    @pl.when(i == 0)
    def _():
        kbuf[:, 0:WINDOW, :] = jnp.zeros((4, WINDOW, hw), BF16)
        vbuf[:, 0:WINDOW, :] = jnp.zeros((4, WINDOW, hw), BF16)

    q = zq_ref[0]
    ssq = _head_sumsq(q, bdq_ref[...])
    qbuf[...] = (q * lax.rsqrt(ssq * (1.0 / HEAD_DIM) + EPS) * (qw_ref[...] * (HEAD_DIM ** -0.5 * LOG2E))).astype(BF16)
    kv = zkv_ref[0]
    k = kv[:, :KV_WIDTH]
    v = kv[:, KV_WIDTH:]
    ssk = _head_sumsq(k, bdk_ref[...])
    kn = k * lax.rsqrt(ssk * (1.0 / HEAD_DIM) + EPS) * kw_ref[...]
    kr = pltpu.roll(kn, HEAD_DIM, 1)
    vr = pltpu.roll(v, HEAD_DIM, 1)
    low = lax.broadcasted_iota(jnp.int32, (1, hw), 1) < HEAD_DIM
    for j in range(ATT_KV_HEADS):
        for half in range(2):
            keep = low if half == 0 else jnp.logical_not(low)
            ksrc = kn if j == half else kr
            vsrc = v if j == half else vr
            kbuf[2 * j + half, WINDOW:WINDOW + ts, :] = jnp.where(keep, ksrc, 0.0).astype(BF16)
            vbuf[2 * j + half, WINDOW:WINDOW + ts, :] = jnp.where(keep, vsrc, 0.0).astype(BF16)

    def block(n, carry):
        r0 = pl.multiple_of(n * WINDOW, WINDOW)
        table = jnp.where(jnp.logical_and(i == 0, n == 0), ATT_HEADS, 0)
        for g0 in range(0, ATT_HEADS, ATT_GROUP):
            heads = range(g0, g0 + ATT_GROUP)
            logits = []
            for h in heads:
                hp, half = h // 2, h % 2
                j = h // (ATT_HEADS // ATT_KV_HEADS)
                qp = qbuf[pl.ds(r0, WINDOW), hp * hw:(hp + 1) * hw]
                keys = kbuf[2 * j + half, pl.ds(r0, 2 * WINDOW), :]
                logits.append(_dot_nt(qp, keys) + bias_ref[table + h])
            ps, scales = [], []
            for h, lg in zip(heads, logits):
                sink = sink_ref[h] * LOG2E
                m = jnp.maximum(jnp.max(lg, axis=-1, keepdims=True), sink)
                p = jnp.exp2(lg - m)
                scales.append(1.0 / (jnp.sum(p, axis=-1, keepdims=True) + jnp.exp2(sink - m)))
                ps.append(p.astype(BF16))
            outs = []
            for h, p in zip(heads, ps):
                half = h % 2
                j = h // (ATT_HEADS // ATT_KV_HEADS)
                vals = vbuf[2 * j + half, pl.ds(r0, 2 * WINDOW), :]
                outs.append(_dot(p, vals))
            for k in range(0, ATT_GROUP, 2):
                hp = (g0 + k) // 2
                acc = outs[k] * scales[k] + outs[k + 1] * scales[k + 1]
                o_ref[0, pl.ds(r0, WINDOW), hp * hw:(hp + 1) * hw] = acc.astype(o_ref.dtype)
        return carry

    lax.fori_loop(0, ts // WINDOW, block, 0)
    kbuf[:, 0:WINDOW, :] = kbuf[:, ts:ts + WINDOW, :]
    vbuf[:, 0:WINDOW, :] = vbuf[:, ts:ts + WINDOW, :]


def _attn_mixer(zq, zkv, q_norm_w, k_norm_w, sinks, consts):
    b, s, _ = zq.shape
    ts = min(ATT_TILE, s)
    bias, bdq, bdk = consts
    hw = 2 * HEAD_DIM
    qw = jnp.tile(q_norm_w, ATT_HEADS).reshape(1, ATT_WIDTH)
    kw = jnp.tile(k_norm_w, ATT_KV_HEADS).reshape(1, KV_WIDTH)
    full = lambda a: pl.BlockSpec(a.shape, lambda bi, i, sk: (0,) * a.ndim)
    grid_spec = pltpu.PrefetchScalarGridSpec(
        num_scalar_prefetch=1,
        grid=(b, s // ts),
        in_specs=[pl.BlockSpec((1, ts, ATT_WIDTH), lambda bi, i, sk: (bi, i, 0)),
                  pl.BlockSpec((1, ts, 2 * KV_WIDTH), lambda bi, i, sk: (bi, i, 0)),
                  full(qw), full(kw), full(bias), full(bdq), full(bdk)],
        out_specs=pl.BlockSpec((1, ts, ATT_WIDTH), lambda bi, i, sk: (bi, i, 0)),
        scratch_shapes=[pltpu.VMEM((ts, ATT_WIDTH), BF16),
                        pltpu.VMEM((4, WINDOW + ts, hw), BF16),
                        pltpu.VMEM((4, WINDOW + ts, hw), BF16)],
    )
    return pl.pallas_call(
        _attn_kernel,
        grid_spec=grid_spec,
        out_shape=jax.ShapeDtypeStruct((b, s, ATT_WIDTH), BF16),
        compiler_params=_cparams(("parallel", "arbitrary")),
        name="swa_mixer",
    )(sinks, zq, zkv, qw, kw, bias, bdq, bdk)


def _router_constants(tm):
    t = np.arange(tm)
    before = (t[:, None] < t[None, :]).astype(np.float32)
    e = np.arange(N_EXPERTS)
    lower = (e[None, :] < e[:, None]).astype(np.float32)
    return (jnp.asarray(before, dtype=BF16), jnp.asarray(np.ones((tm, tm), np.float32), dtype=BF16),
            jnp.asarray(lower, dtype=BF16))


def _route(sel, scores):
    tm = sel.shape[1]
    group_scores = []
    for g in range(N_GROUPS):
        rows = [sel[g * EXPERTS_PER_GROUP + a:g * EXPERTS_PER_GROUP + a + 1] for a in range(EXPERTS_PER_GROUP)]
        best_pair = None
        for a in range(EXPERTS_PER_GROUP):
            for bb in range(a + 1, EXPERTS_PER_GROUP):
                pair = rows[a] + rows[bb]
                best_pair = pair if best_pair is None else jnp.maximum(best_pair, pair)
        group_scores.append(best_pair)
    top = functools.reduce(jnp.maximum, group_scores)
    best = jnp.full((1, tm), N_GROUPS - 1, jnp.int32)
    for g in reversed(range(N_GROUPS - 1)):
        best = jnp.where(group_scores[g] == top, g, best)
    row = lax.broadcasted_iota(jnp.int32, (N_EXPERTS, tm), 0)
    cand = jnp.where(row // EXPERTS_PER_GROUP == best, sel, NEG)
    m1 = jnp.max(cand, axis=0, keepdims=True)
    i1 = jnp.min(jnp.where(cand == m1, row, N_EXPERTS), axis=0, keepdims=True)
    oh1 = row == i1
    cand = jnp.where(oh1, NEG, cand)
    m2 = jnp.max(cand, axis=0, keepdims=True)
    i2 = jnp.min(jnp.where(cand == m2, row, N_EXPERTS), axis=0, keepdims=True)
    oh2 = row == i2
    s1 = jnp.sum(jnp.where(oh1, scores, 0.0), axis=0, keepdims=True)
    s2 = jnp.sum(jnp.where(oh2, scores, 0.0), axis=0, keepdims=True)
    return oh1, oh2, s1 / (s1 + s2), s2 / (s1 + s2)


def _out_kernel(yp_ref, yh_ref, ya_ref, x_ref, g1_ref, sc_ref, sh_ref, nw_ref, wo_ref, rwt_ref, rb_ref,
                before_ref, ones_ref, lower_ref, x1_ref, h2_ref, ls_ref, gate_ref, cnt_ref):
    tm = MOE_TOK_TILE
    subs = [pl.ds(u * tm, tm) for u in range(x_ref.shape[0] // tm)]
    p0, p1 = POOL_WIDTH, POOL_WIDTH + HGRN_WIDTH
    h2s = []
    for rows in subs:
        mix = (_dot(yp_ref[rows, :], wo_ref[0:p0, :]) + _dot(yh_ref[rows, :], wo_ref[p0:p1, :])
               + _dot(ya_ref[rows, :], wo_ref[p1:, :]))
        x1 = x_ref[rows, :] + g1_ref[0] * mix
        x1_ref[rows, :] = x1
        ms = jnp.mean(x1 * x1, axis=-1, keepdims=True)
        h2 = ((x1 * lax.rsqrt(ms + EPS) * nw_ref[...]) * (1.0 + sc_ref[0]) + sh_ref[0]).astype(BF16)
        h2_ref[rows, :] = h2
        h2s.append(h2)
    logits = [_dot_nt(rwt_ref[...], h2) for h2 in h2s]
    picks = []
    for lg in logits:
        ex = jnp.exp(lg - jnp.max(lg, axis=0, keepdims=True))
        scores = ex / jnp.sum(ex, axis=0, keepdims=True)
        picks.append(_route(scores + rb_ref[...], scores))
    chosen = [jnp.where(jnp.logical_or(oh1, oh2), 1.0, 0.0).astype(BF16) for oh1, oh2, _, _ in picks]
    ranks = [_dot(ch, before_ref[...]) for ch in chosen]
    counts = [_dot(ch, ones_ref[...]) for ch in chosen]
    aligned = [(jnp.floor((cn + (ROW_ALIGN - 1)) * (1.0 / ROW_ALIGN)) * ROW_ALIGN).astype(BF16) for cn in counts]
    slots = [_dot(lower_ref[...], al) + rk for al, rk in zip(aligned, ranks)]
    for u, rows in enumerate(subs):
        oh1, oh2, w1, w2 = picks[u]
        gate_ref[:, rows] = jnp.concatenate([w1, w2], axis=0)
        ls_ref[:, rows] = jnp.concatenate([jnp.sum(jnp.where(oh1, slots[u], 0.0), axis=0, keepdims=True),
                                           jnp.sum(jnp.where(oh2, slots[u], 0.0), axis=0, keepdims=True)], axis=0)
        cnt_ref[u] = counts[u][:, 0:128]


def _out_proj_router(yp, yh, ya, x, g1, sc, sh, nw, wo_b, layer, rwt_b, rb, consts):
    b, s, d = x.shape
    t = b * s
    tm = MOE_TOK_TILE
    n_sub = ROUTER_SUBTILES if s % (ROUTER_SUBTILES * tm) == 0 else 1
    ts = n_sub * tm
    per_batch = s // ts
    nt = t // tm
    before, ones, lower = consts
    tok = lambda w: pl.BlockSpec((ts, w), lambda i: (i, 0))
    vec = pl.BlockSpec((1, 1, d), lambda i: (i // per_batch, 0, 0))
    full = lambda a: pl.BlockSpec(a.shape, lambda i: (0,) * a.ndim)
    lanes = pl.BlockSpec((2, ts), lambda i: (0, i))
    return pl.pallas_call(
        _out_kernel,
        grid=(t // ts,),
        in_specs=[tok(POOL_WIDTH), tok(HGRN_WIDTH), tok(ATT_WIDTH), tok(d), vec, vec, vec,
                  pl.BlockSpec((1, d), lambda i: (0, 0)),
                  pl.BlockSpec((None,) + wo_b.shape[1:], lambda i: (layer, 0, 0)), full(rwt_b),
                  pl.BlockSpec((N_EXPERTS, 1), lambda i: (0, 0)), full(before), full(ones), full(lower)],
        out_specs=[tok(d), tok(d), lanes, lanes, pl.BlockSpec((n_sub, N_EXPERTS, 128), lambda i: (i, 0, 0))],
        out_shape=[jax.ShapeDtypeStruct((t, d), F32), jax.ShapeDtypeStruct((t, d), BF16),
                   jax.ShapeDtypeStruct((2, t), F32), jax.ShapeDtypeStruct((2, t), F32),
                   jax.ShapeDtypeStruct((nt, N_EXPERTS, 128), F32)],
        compiler_params=_cparams(("parallel",)),
        name="out_proj_router",
    )(yp.reshape(t, -1), yh.reshape(t, -1), ya.reshape(t, -1), x.reshape(t, d),
      g1.reshape(b, 1, d), sc.reshape(b, 1, d), sh.reshape(b, 1, d), nw.reshape(1, d), wo_b, rwt_b,
      rb.reshape(N_EXPERTS, 1), before, ones, lower)


def _moe_tables(cnt_out):
    cnt = jnp.round(cnt_out[:, :, 0]).astype(jnp.int32)
    cnt = (cnt + ROW_ALIGN - 1) // ROW_ALIGN * ROW_ALIGN
    total = jnp.sum(cnt, axis=0)
    padded = (total + EXPERT_TILE - 1) // EXPERT_TILE * EXPERT_TILE
    ends = jnp.cumsum(padded)
    first = ends - padded
    start = first[None, :] + jnp.cumsum(cnt, axis=0) - cnt
    loff = jnp.cumsum(cnt, axis=1) - cnt
    n_tiles = _moe_rows(cnt.shape[0] * MOE_TOK_TILE) // EXPERT_TILE
    n_used = ends[-1] // EXPERT_TILE
    tile_row = jnp.minimum(jnp.arange(n_tiles), n_used - 1) * EXPERT_TILE
    tile_expert = jnp.sum((ends[None, :] <= tile_row[:, None]).astype(jnp.int32), axis=1)
    pieces = jnp.sum(cnt, axis=1, keepdims=True) // ROW_ALIGN

    def copies(n_per_expert, offset_in_chunk, n_max):
        incl = jnp.cumsum(n_per_expert, axis=1)
        k = jnp.arange(n_max, dtype=jnp.int32)
        owner = jnp.sum((incl[:, None, :] <= k[None, :, None]).astype(jnp.int32), axis=2)
        owner = jnp.minimum(owner, N_EXPERTS - 1)
        is_owner = owner[:, :, None] == jnp.arange(N_EXPERTS, dtype=jnp.int32)[None, None, :]
        pick = lambda a: jnp.sum(jnp.where(is_owner, a[:, None, :], 0), axis=2)
        off = offset_in_chunk(k[None, :] - pick(incl - n_per_expert), pick(cnt))
        return pick(loff) + off, pick(start) + off, incl[:, -1:]

    big_src, big_dst, n_big = copies(cnt // (2 * ROW_ALIGN), lambda j, c: j * (2 * ROW_ALIGN), MAX_PIECES // 2)
    small_src, small_dst, n_small = copies(cnt % (2 * ROW_ALIGN) // ROW_ALIGN, lambda j, c: c - ROW_ALIGN, N_EXPERTS)
    moves = jnp.concatenate([big_src, big_dst, small_src, small_dst, n_big, n_small, pieces], axis=1)
    free_pieces = (n_tiles - n_used) * (EXPERT_TILE // ZERO_ROWS)
    tail = jnp.concatenate([first + total, padded - total, jnp.stack([ends[-1], free_pieces])])
    return (moves.reshape(-1).astype(jnp.int32), tail.astype(jnp.int32), tile_expert.astype(jnp.int32),
            n_used.reshape(1).astype(jnp.int32))


MOVE_BIG_SRC = 0
MOVE_BIG_DST = MOVE_BIG_SRC + MAX_PIECES // 2
MOVE_SMALL_SRC = MOVE_BIG_DST + MAX_PIECES // 2
MOVE_SMALL_DST = MOVE_SMALL_SRC + N_EXPERTS
MOVE_N_BIG = MOVE_SMALL_DST + N_EXPERTS
MOVE_N_SMALL = MOVE_N_BIG + 1
MOVE_PIECES = MOVE_N_SMALL + 1
MOVE_WIDTH = MOVE_PIECES + 1


def _tile_copies(moves_ref, tile, make_copy):
    base = tile * MOVE_WIDTH
    for src0, dst0, n_at, size in ((MOVE_BIG_SRC, MOVE_BIG_DST, MOVE_N_BIG, 2 * ROW_ALIGN),
                                   (MOVE_SMALL_SRC, MOVE_SMALL_DST, MOVE_N_SMALL, ROW_ALIGN)):
        def body(j, carry):
            make_copy(pl.multiple_of(moves_ref[base + src0 + j], ROW_ALIGN),
                      pl.multiple_of(moves_ref[base + dst0 + j], ROW_ALIGN), size).start()
            return carry

        lax.fori_loop(0, moves_ref[base + n_at], body, 0)


def _moe_rows(n_tokens):
    per_tile = 2 * MOE_TOK_TILE + N_EXPERTS * (ROW_ALIGN - 1)
    rows = (n_tokens // MOE_TOK_TILE) * per_tile + N_EXPERTS * (EXPERT_TILE - ROW_ALIGN)
    return (rows + EXPERT_TILE - 1) // EXPERT_TILE * EXPERT_TILE


def _chunk_copies(count, make_copy, wait=False):
    for size in CHUNK_SIZES:
        offset = jnp.bitwise_and(count, ~(2 * size - 1))

        @pl.when(jnp.bitwise_and(count, size) != 0)
        def _():
            copy = make_copy(pl.multiple_of(offset, ROW_ALIGN), size)
            copy.wait() if wait else copy.start()


def _wait_rows(n_pieces, make_copy):
    for p in WAIT_PIECES:
        @pl.when(jnp.bitwise_and(n_pieces, p) != 0)
        def _():
            make_copy(p * ROW_ALIGN).wait()


def _slot_rows(tm):
    return lax.broadcasted_iota(jnp.int32, (MOE_SLOTS, tm), 0).astype(F32)


def _sort_kernel(moves_ref, tail_ref, h2_ref, ls_ref, xs_ref, buf_ref, sem_ref):
    i = pl.program_id(0)
    n = pl.num_programs(0)
    tm = MOE_TOK_TILE
    n_sub = h2_ref.shape[0] // tm
    slot_of = lambda tile, u: u if n_sub == 2 else tile % 2

    def sent(tile, s):
        _wait_rows(moves_ref[tile * MOVE_WIDTH + MOVE_PIECES], lambda rows: pltpu.make_async_copy(
            buf_ref.at[s, pl.ds(0, rows), :], xs_ref.at[pl.ds(0, rows), :], sem_ref.at[s]))

    for u in range(n_sub):
        tile = i * n_sub + u
        slot = slot_of(tile, u)

        @pl.when(tile >= 2)
        def _():
            sent(tile - 2, slot)

        ls = ls_ref[:, u * tm:(u + 1) * tm]
        srow = _slot_rows(tm)
        perm = jnp.where(jnp.logical_or(srow == ls[0:1], srow == ls[1:2]), 1.0, 0.0).astype(BF16)
        buf_ref[slot] = _dot(perm, h2_ref[u * tm:(u + 1) * tm, :]).astype(BF16)
        _tile_copies(moves_ref, tile, lambda src, dst, size: pltpu.make_async_copy(
            buf_ref.at[slot, pl.ds(src, size), :], xs_ref.at[pl.ds(dst, size), :], sem_ref.at[slot]))

    @pl.when(i == n - 1)
    def _():
        last = n * n_sub - 1

        @pl.when(last >= 1)
        def _():
            sent(last - 1, slot_of(last - 1, n_sub - 2))
        sent(last, slot_of(last, n_sub - 1))

        buf_ref[0] = jnp.zeros(buf_ref.shape[1:], BF16)
        zeros_to = lambda row, size: pltpu.make_async_copy(
            buf_ref.at[0, pl.ds(0, size), :], xs_ref.at[pl.ds(row, size), :], sem_ref.at[0])

        for wait in (False, True):
            def expert_tail(e, carry):
                st = pl.multiple_of(tail_ref[e], ROW_ALIGN)
                _chunk_copies(tail_ref[N_EXPERTS + e], lambda off, size: zeros_to(st + off, size), wait)
                return carry

            def free_tile(j, carry):
                copy = zeros_to(pl.multiple_of(tail_ref[2 * N_EXPERTS] + j * ZERO_ROWS, ROW_ALIGN), ZERO_ROWS)
                copy.wait() if wait else copy.start()
                return carry

            lax.fori_loop(0, N_EXPERTS, expert_tail, 0)
            lax.fori_loop(0, tail_ref[2 * N_EXPERTS + 1], free_tile, 0)


def _sort_tokens(h2, ls, moves, tail, n_rows):
    t, d = h2.shape
    tm = MOE_TOK_TILE * (2 if (t // MOE_TOK_TILE) % 2 == 0 else 1)
    grid_spec = pltpu.PrefetchScalarGridSpec(
        num_scalar_prefetch=2,
        grid=(t // tm,),
        in_specs=[pl.BlockSpec((tm, d), lambda i, *_: (i, 0)),
                  pl.BlockSpec((2, tm), lambda i, *_: (0, i))],
        out_specs=pl.BlockSpec(memory_space=pl.ANY),
        scratch_shapes=[pltpu.VMEM((2, MOE_SLOTS, d), BF16), pltpu.SemaphoreType.DMA((2,))],
    )
    return pl.pallas_call(
        _sort_kernel,
        grid_spec=grid_spec,
        out_shape=jax.ShapeDtypeStruct((n_rows, d), BF16),
        compiler_params=_cparams(("arbitrary",)),
        name="moe_sort",
    )(moves, tail, h2, ls)


def _gmm_kernel(te_ref, nu_ref, xs_ref, wg_ref, wu_ref, wd_ref, ys_ref, wgb, wub, wdb):
    i = pl.program_id(0)
    used = i < nu_ref[0]
    new_expert = jnp.logical_or(i == 0, te_ref[i] != te_ref[jnp.maximum(i - 1, 0)])

    @pl.when(jnp.logical_and(used, new_expert))
    def _():
        wgb[...] = wg_ref[...].astype(BF16)
        wub[...] = wu_ref[...].astype(BF16)
        wdb[...] = wd_ref[...].astype(BF16)

    @pl.when(used)
    def _():
        x = xs_ref[...]
        he = _silu(_dot(x, wgb[...])) * _dot(x, wub[...])
        ys_ref[...] = _dot(he.astype(BF16), wdb[...]).astype(BF16)

    @pl.when(jnp.logical_not(used))
    def _():
        ys_ref[...] = jnp.zeros(ys_ref.shape, BF16)


def _grouped_mlp(xs, tile_expert, n_used, wg, wu, wd, layer):
    n_rows, d = xs.shape
    te = EXPERT_TILE
    row_map = lambda i, tex, nu: (jnp.minimum(i, nu[0] - 1), 0)
    grid_spec = pltpu.PrefetchScalarGridSpec(
        num_scalar_prefetch=2,
        grid=(n_rows // te,),
        in_specs=[pl.BlockSpec((te, d), row_map),
                  pl.BlockSpec((None, None, d, D_EXPERT), lambda i, tex, nu: (layer, tex[i], 0, 0)),
                  pl.BlockSpec((None, None, d, D_EXPERT), lambda i, tex, nu: (layer, tex[i], 0, 0)),
                  pl.BlockSpec((None, None, D_EXPERT, d), lambda i, tex, nu: (layer, tex[i], 0, 0))],
        out_specs=pl.BlockSpec((te, d), lambda i, tex, nu: (i, 0)),
        scratch_shapes=[pltpu.VMEM((d, D_EXPERT), BF16), pltpu.VMEM((d, D_EXPERT), BF16),
                        pltpu.VMEM((D_EXPERT, d), BF16)],
    )
    return pl.pallas_call(
        _gmm_kernel,
        grid_spec=grid_spec,
        out_shape=jax.ShapeDtypeStruct((n_rows, d), BF16),
        compiler_params=_cparams(("arbitrary",)),
        name="moe_grouped_mlp",
    )(tile_expert, n_used, xs, wg, wu, wd)


def _combine_kernel(moves_ref, ls_ref, gate_ref, x1_ref, g2_ref, ys_ref, o_ref, buf_ref, sem_ref):
    i = pl.program_id(0)
    n = pl.num_programs(0)
    tm = MOE_TOK_TILE
    n_sub = x1_ref.shape[0] // tm
    n_tiles = n * n_sub
    slot_of = lambda tile, u: u if n_sub == 2 else tile % 2

    def fetch(tile, s):
        _tile_copies(moves_ref, tile, lambda dst, src, size: pltpu.make_async_copy(
            ys_ref.at[pl.ds(src, size), :], buf_ref.at[s, pl.ds(dst, size), :], sem_ref.at[s]))

    @pl.when(i == 0)
    def _():
        fetch(0, 0)

    for u in range(n_sub):
        tile = i * n_sub + u
        slot = slot_of(tile, u)

        @pl.when(tile + 1 < n_tiles)
        def _():
            fetch(tile + 1, slot_of(tile + 1, 1 - u))

        n_pieces = moves_ref[tile * MOVE_WIDTH + MOVE_PIECES]
        _wait_rows(n_pieces, lambda rows: pltpu.make_async_copy(
            ys_ref.at[pl.ds(0, rows), :], buf_ref.at[slot, pl.ds(0, rows), :], sem_ref.at[slot]))

        rows_u = pl.ds(u * tm, tm)
        srow = _slot_rows(tm)
        ls = ls_ref[:, u * tm:(u + 1) * tm]
        gate = gate_ref[:, u * tm:(u + 1) * tm]
        weights = (jnp.where(srow == ls[0:1], gate[0:1], 0.0)
                   + jnp.where(srow == ls[1:2], gate[1:2], 0.0)).astype(BF16)
        written = lax.broadcasted_iota(jnp.int32, (MOE_SLOTS, 1), 0) < n_pieces * ROW_ALIGN
        rows = jnp.where(written, buf_ref[slot], jnp.zeros((), BF16))
        y = _dot_tn(weights, rows)
        o_ref[rows_u, :] = x1_ref[rows_u, :] + g2_ref[0] * y


def _combine(ys, ls, gate, x1, g2, moves, seq_len):
    t, d = x1.shape
    tm = MOE_TOK_TILE * (2 if (seq_len // MOE_TOK_TILE) % 2 == 0 else 1)
    per_batch = seq_len // tm
    grid_spec = pltpu.PrefetchScalarGridSpec(
        num_scalar_prefetch=1,
        grid=(t // tm,),
        in_specs=[pl.BlockSpec((2, tm), lambda i, *_: (0, i)),
                  pl.BlockSpec((2, tm), lambda i, *_: (0, i)),
                  pl.BlockSpec((tm, d), lambda i, *_: (i, 0)),
                  pl.BlockSpec((1, 1, d), lambda i, *_: (i // per_batch, 0, 0)),
                  pl.BlockSpec(memory_space=pl.ANY)],
        out_specs=pl.BlockSpec((tm, d), lambda i, *_: (i, 0)),
        scratch_shapes=[pltpu.VMEM((2, MOE_SLOTS, d), BF16), pltpu.SemaphoreType.DMA((2,))],
    )
    return pl.pallas_call(
        _combine_kernel,
        grid_spec=grid_spec,
        out_shape=jax.ShapeDtypeStruct((t, d), F32),
        compiler_params=_cparams(("arbitrary",)),
        name="moe_combine",
    )(moves, ls, gate, x1, g2.reshape(-1, 1, d), ys)


def kernel(x, c, ada_w, ada_b, norm1_w, norm2_w, w_in, pool_w, pool_scale, hgrn_lb_raw, hgrn_norm_w, q_norm_w,
           k_norm_w, attn_sinks, w_out, router_w, router_bias, expert_w_gate, expert_w_up, expert_w_down):
    b, s, d = x.shape
    depth = ada_w.shape[0]
    t = b * s
    n_rows = _moe_rows(t)

    p = jax.nn.softmax(hgrn_lb_raw.astype(F32), axis=0)
    lower_bounds = jnp.maximum(jnp.cumsum(p, axis=0) - p[0:1], 0.0)

    mod = _modulation(c, ada_w, ada_b)
    hgrn_consts = _hgrn_constants() + (_head_ones(HGRN_WIDTH, HGRN_KDIM),)
    attn_consts = (_attn_bias(), _head_ones(ATT_WIDTH, HEAD_DIM), _head_ones(KV_WIDTH, HEAD_DIM))
    router_consts = _router_constants(MOE_TOK_TILE)
    rwt_b = router_w.T.astype(BF16)
    w_in_b = w_in.astype(BF16)
    w_out_b = w_out.astype(BF16)

    for l in range(depth):
        sh1, sc1, g1, sh2, sc2, g2 = [mod[l, :, j * d:(j + 1) * d] for j in range(6)]
        yp, zh, zq, zkv = _in_proj(x, sc1, sh1, norm1_w[l], w_in_b, l, _block_diag(pool_w[l]).astype(BF16),
                                   pool_scale[l])
        yh = _hgrn_mixer(zh, lower_bounds[l], hgrn_norm_w[l], hgrn_consts)
        ya = _attn_mixer(zq, zkv, q_norm_w[l], k_norm_w[l], attn_sinks[l], attn_consts)
        x1, h2, ls, gate, cnt_out = _out_proj_router(
            yp, yh, ya, x, g1, sc2, sh2, norm2_w[l], w_out_b, l, rwt_b, router_bias, router_consts)
        moves, tail, tile_expert, n_used = _moe_tables(cnt_out)
        xs = _sort_tokens(h2, ls, moves, tail, n_rows)
        ys = _grouped_mlp(xs, tile_expert, n_used, expert_w_gate, expert_w_up, expert_w_down, l)
        x = _combine(ys, ls, gate, x1, g2, moves, s).reshape(b, s, d)
    return x
```

```python
import functools

import numpy as np
import jax
import jax.numpy as jnp
from jax import lax
from jax.experimental import pallas as pl
from jax.experimental.pallas import tpu as pltpu

F32 = jnp.float32
BF16 = jnp.bfloat16

D_MODEL = 1024
POOL_WINDOWS = (2, 4, 8, 16)
POOL_WIDTH = 256
POOL_GROUP = 64
POOL_HALO = 32
HGRN_HEADS = 4
HGRN_KDIM = 64
HGRN_WIDTH = 256
HGRN_CHUNK = 64
ATT_HEADS = 8
ATT_KV_HEADS = 2
HEAD_DIM = 64
ATT_WIDTH = 512
KV_WIDTH = 128
WINDOW = 128
N_EXPERTS = 16
N_GROUPS = 4
EXPERTS_PER_GROUP = 4
D_EXPERT = 512
EPS = 1e-6
MAX_ONE_MINUS_F = 1.0 - 1e-6
LOG2E = 1.4426950408889634
NEG = -1e30

VMEM_LIMIT = 48 * 1024 * 1024

IN_TILE = 1024
ATT_TILE = 1024
MOE_TOK_TILE = 256
ROW_ALIGN = 16
MOE_SLOTS = 768
EXPERT_TILE = 1024
ZERO_ROWS = 512
CHUNK_SIZES = (512, 256, 128, 64, 32, 16)
COMBINE_AHEAD = 3
MAX_PIECES = MOE_SLOTS // ROW_ALIGN
WAIT_PIECES = (32, 16, 8, 4, 2, 1)


def _sigmoid(x):
    return 1.0 / (1.0 + jnp.exp(-x))


def _silu(x):
    return x * _sigmoid(x)


def _cparams(sem, **kw):
    return pltpu.CompilerParams(dimension_semantics=sem, vmem_limit_bytes=VMEM_LIMIT, **kw)


def _dot(a, b):
    return jnp.dot(a, b, preferred_element_type=F32)


def _dot_nt(a, b):
    return lax.dot_general(a, b, (((1,), (1,)), ((), ())), preferred_element_type=F32)


def _dot_tn(a, b):
    return lax.dot_general(a, b, (((0,), (0,)), ((), ())), preferred_element_type=F32)


def _head_sumsq(x, bd):
    return _dot((x * x).astype(BF16), bd)


def _mod_kernel(c_ref, w_ref, b_ref, o_ref):
    cond = _silu(c_ref[...])
    o_ref[0] = _dot(cond.astype(BF16), w_ref[0].astype(BF16)) + b_ref[0]


def _modulation(c, ada_w, ada_b):
    depth, d, n = ada_w.shape
    b = c.shape[0]
    nb = n // d
    return pl.pallas_call(
        _mod_kernel,
        grid=(depth, nb),
        in_specs=[pl.BlockSpec((b, d), lambda l, j: (0, 0)),
                  pl.BlockSpec((1, d, d), lambda l, j: (l, 0, j)),
                  pl.BlockSpec((1, 1, d), lambda l, j: (l, 0, j))],
        out_specs=pl.BlockSpec((1, b, d), lambda l, j: (l, 0, j)),
        out_shape=jax.ShapeDtypeStruct((depth, b, n), F32),
        compiler_params=_cparams(("parallel", "parallel")),
        name="adaln_mod",
    )(c, ada_w, ada_b.reshape(depth, 1, n))


def _in_kernel(x_ref, sc_ref, sh_ref, nw_ref, w_ref, pw_ref, ps_ref, yp_ref, zh_ref, zq_ref, zkv_ref,
               buf_ref, sa_ref, sb_ref):
    x = x_ref[0]
    ms = jnp.mean(x * x, axis=-1, keepdims=True)
    h = (x * lax.rsqrt(ms + EPS) * nw_ref[...]) * (1.0 + sc_ref[0]) + sh_ref[0]
    z = _dot(h.astype(BF16), w_ref[...])
    zh_ref[0] = z[:, POOL_WIDTH:POOL_WIDTH + 4 * HGRN_WIDTH]
    zq_ref[0] = z[:, POOL_WIDTH + 4 * HGRN_WIDTH:POOL_WIDTH + 4 * HGRN_WIDTH + ATT_WIDTH]
    zkv_ref[0] = z[:, POOL_WIDTH + 4 * HGRN_WIDTH + ATT_WIDTH:]
    yp_ref[0] = _pool(z[:, :POOL_WIDTH], pl.program_id(1), pw_ref, ps_ref, buf_ref, sa_ref, sb_ref).astype(yp_ref.dtype)


def _in_proj(x, sc, sh, nw, w_in_b, layer, pool_bd, pool_scale):
    b, s, d = x.shape
    n = w_in_b.shape[2]
    ts = min(IN_TILE, s)
    tok = lambda w: pl.BlockSpec((1, ts, w), lambda bi, i: (bi, i, 0))
    vec = pl.BlockSpec((1, 1, d), lambda bi, i: (bi, 0, 0))
    full = lambda a: pl.BlockSpec(a.shape, lambda bi, i: (0,) * a.ndim)
    pool_scale = pool_scale.reshape(1, POOL_WIDTH)
    outs = ((POOL_WIDTH, BF16), (4 * HGRN_WIDTH, F32), (ATT_WIDTH, F32), (2 * KV_WIDTH, F32))
    return pl.pallas_call(
        _in_kernel,
        grid=(b, s // ts),
        in_specs=[tok(d), vec, vec,
                  pl.BlockSpec((1, d), lambda bi, i: (0, 0)),
                  pl.BlockSpec((None, d, n), lambda bi, i: (layer, 0, 0)), full(pool_bd), full(pool_scale)],
        out_specs=[tok(w) for w, _ in outs],
        out_shape=[jax.ShapeDtypeStruct((b, s, w), dt) for w, dt in outs],
        scratch_shapes=[pltpu.VMEM((POOL_HALO + ts, POOL_WIDTH), F32)] * 3,
        compiler_params=_cparams(("parallel", "arbitrary")),
        name="in_proj_pool",
    )(x, sc.reshape(b, 1, d), sh.reshape(b, 1, d), nw.reshape(1, d), w_in_b, pool_bd, pool_scale)


def _pool(a, i, w_ref, scale_ref, buf_ref, sa_ref, sb_ref):
    ts = a.shape[0]
    halo = POOL_HALO
    end = halo + ts
    half = 2 * POOL_GROUP

    @pl.when(i == 0)
    def _():
        buf_ref[0:halo, :] = jnp.zeros((halo, POOL_WIDTH), F32)

    buf_ref[halo:end, :] = a
    s2 = buf_ref[8:end, :] + buf_ref[7:end - 1, :]
    sa_ref[8:end, :] = s2
    s4 = sa_ref[16:end, :] + sa_ref[14:end - 2, :]
    sb_ref[16:end, :] = s4
    s8 = sb_ref[24:end, half:] + sb_ref[20:end - 4, half:]
    sa_ref[24:end, half:] = s8
    s16 = sa_ref[halo:end, half:] + sa_ref[halo - 8:end - 8, half:]
    lane = lax.broadcasted_iota(jnp.int32, (1, POOL_WIDTH), 1)
    win = jnp.left_shift(2, lane // POOL_GROUP)
    low = lane[:, 0:half] % half < POOL_GROUP
    acc = jnp.concatenate([jnp.where(low, s2[halo - 8:, 0:half], s4[halo - 16:, 0:half]),
                           jnp.where(low, s8[halo - 24:], s16)], axis=1)
    pos = i * ts + lax.broadcasted_iota(jnp.int32, (ts, 1), 0)
    count = jnp.minimum(pos + 1, win).astype(F32)
    pooled = acc / count - a
    buf_ref[0:halo, :] = a[ts - halo:, :]
    return _dot(pooled.astype(BF16), w_ref[...]) * scale_ref[...]


def _block_diag(blocks):
    g, n, _ = blocks.shape
    eye = jnp.eye(g, dtype=blocks.dtype)
    return (eye[:, None, :, None] * blocks[:, :, None, :]).reshape(g * n, g * n)


def _head_ones(width, head):
    idx = np.arange(width) // head
    return jnp.asarray((idx[:, None] == idx[None, :]).astype(np.float32), dtype=BF16)


HGRN_LEVELS = (32, 16, 8, 4, 2, 1)
HGRN_MATMUL_LEVELS = (2, 1)
N_LEVEL_MASKS = len(HGRN_LEVELS) + 1
HGRN_SEQS = 8
HGRN_TILE = 256
ATT_GROUP = 8
ROUTER_SUBTILES = 4


def _hgrn_constants():
    c = HGRN_CHUNK
    m = np.zeros((len(HGRN_MATMUL_LEVELS) + 1, c, c), np.float32)
    masks = np.zeros((N_LEVEL_MASKS, c, c), np.float32)
    for lvl, n in enumerate(HGRN_LEVELS):
        for t in range(c):
            blk = t // (2 * n)
            mid = blk * 2 * n + n
            if t >= mid:
                masks[lvl, t, blk * 2 * n:mid] = 1.0
            if n in HGRN_MATMUL_LEVELS:
                row = m[HGRN_MATMUL_LEVELS.index(n), t]
                if t >= mid:
                    row[mid:t + 1] = 1.0
                else:
                    row[t + 1:mid] = 1.0
    for t in range(c):
        m[-1, t, :t + 1] = 1.0
        masks[-1, t, t] = 1.0
    m = m.reshape(-1, c)
    m3 = np.concatenate([m, m, m], axis=1)
    masks = np.tile(masks, (1, HGRN_HEADS, 1))
    return jnp.asarray(m3, dtype=BF16), jnp.asarray(masks, dtype=F32)


def _hgrn_chunks(units, bd, m3, lmask_ref):
    c = HGRN_CHUNK
    w = HGRN_WIDTH
    n_fine = len(HGRN_MATMUL_LEVELS)
    lane_head = lax.broadcasted_iota(jnp.int32, (1, w), 1) // HGRN_KDIM
    sums = [_dot(m3, u[2]) for u in units]
    bs = [s[n_fine * c:] for s in sums]

    def level_decay(u, n):
        if n in HGRN_MATMUL_LEVELS:
            k = HGRN_MATMUL_LEVELS.index(n)
            return jnp.exp(sums[u][k * c:(k + 1) * c])
        blocks = bs[u].reshape(c // (2 * n), 2 * n, w)
        ref = blocks[:, n - 1:n, :]
        right = lax.broadcasted_iota(jnp.int32, (1, 2 * n, 1), 1) >= n
        return jnp.exp(jnp.where(right, blocks - ref, ref - blocks).reshape(c, w))

    groups = c // 8
    parts = [[[None] * groups for _ in range(HGRN_HEADS)] for _ in units]

    def add_part(u, h, g, piece):
        parts[u][h][g] = piece if parts[u][h][g] is None else parts[u][h][g] + piece

    for lvl in range(N_LEVEL_MASKS):
        n = HGRN_LEVELS[lvl] if lvl < len(HGRN_LEVELS) else 0
        wanted = [g for g in range(groups) if (8 * g) % (2 * n) >= n] if n >= 8 else list(range(groups))
        for u, (qf, kk, _, _, _, _) in enumerate(units):
            if lvl < len(HGRN_LEVELS):
                e = level_decay(u, n)
                ql = qf * e
                kl = (kk * e).astype(BF16)
            else:
                ql = qf
                kl = kk.astype(BF16)
            ql = jnp.concatenate([ql[8 * g:8 * g + 8] for g in wanted], axis=0).astype(BF16)
            zero = jnp.zeros_like(ql)
            qs = jnp.concatenate([jnp.where(lane_head == h, ql, zero) for h in range(HGRN_HEADS)], axis=0)
            res = _dot_nt(qs, kl)
            for h in range(HGRN_HEADS):
                for k, g in enumerate(wanted):
                    r0 = (h * len(wanted) + k) * 8
                    add_part(u, h, g, res[r0:r0 + 8] * lmask_ref[lvl, h * c + 8 * g:h * c + 8 * g + 8, :])
    scores = [jnp.concatenate([parts[u][h][g] for h in range(HGRN_HEADS) for g in range(groups)], axis=0)
              for u in range(len(units))]
    rs = [_dot(scores[u].astype(BF16), unit[3]) for u, unit in enumerate(units)]
    inters = [_dot_nt((unit[0] * jnp.exp(bs[u])).astype(BF16), unit[5].astype(BF16)) for u, unit in enumerate(units)]
    upds = [_dot_tn(unit[3], (unit[1] * jnp.exp(bs[u][c - 1:c, :] - bs[u])).astype(BF16))
            for u, unit in enumerate(units)]
    outs = []
    for u in range(len(units)):
        o = inters[u]
        for h in range(HGRN_HEADS):
            o = o + jnp.where(lane_head == h, rs[u][h * c:(h + 1) * c], 0.0)
        outs.append(o)
    sss = [_head_sumsq(o, bd) for o in outs]
    res = []
    for u, unit in enumerate(units):
        st = unit[5] * jnp.exp(bs[u][c - 1:c, :]) + jnp.where(bd > 0, upds[u], 0.0)
        y = outs[u] * lax.rsqrt(sss[u] * (1.0 / HGRN_KDIM) + EPS) * unit[4]
        res.append((y, st))
    return res


def _hgrn_kernel(zh_ref, lb_ref, gw_ref, m3_ref, lmask_ref, bd_ref, o_ref, st_ref):
    i = pl.program_id(1)
    nb, ts = zh_ref.shape[0], zh_ref.shape[1]
    c = HGRN_CHUNK
    w = HGRN_WIDTH

    @pl.when(i == 0)
    def _():
        st_ref[...] = jnp.zeros(st_ref.shape, F32)

    one_minus_lb = 1.0 - lb_ref[...]
    gw = gw_ref[...]
    bd = bd_ref[...]
    m3 = m3_ref[...]

    def chunk(ci, carry):
        rows = pl.ds(pl.multiple_of(ci * c, c), c)
        units = []
        for s in range(nb):
            q, f, v, g = [zh_ref[s, rows, j * w:(j + 1) * w] for j in range(4)]
            kk = one_minus_lb * _sigmoid(-f)
            lf = jnp.log(1.0 - jnp.minimum(kk, MAX_ONE_MINUS_F))
            hi = lf.astype(BF16)
            r1 = lf - hi.astype(F32)
            mid = r1.astype(BF16)
            lo = (r1 - mid.astype(F32)).astype(BF16)
            units.append((_silu(q), kk, jnp.concatenate([hi, mid, lo], axis=0), v.astype(BF16), gw * _silu(g),
                          st_ref[s]))
        for s, (y, st) in enumerate(_hgrn_chunks(units, bd, m3, lmask_ref)):
            st_ref[s] = st
            o_ref[s, rows, :] = y.astype(o_ref.dtype)
        return carry

    lax.fori_loop(0, ts // c, chunk, 0)


def _hgrn_mixer(zh, lb, norm_w, consts):
    b, s, _ = zh.shape
    w = HGRN_WIDTH
    ts = min(HGRN_TILE, s)
    nb = HGRN_SEQS if b % HGRN_SEQS == 0 else 1
    m3, lmask, bd = consts
    full = lambda a: pl.BlockSpec(a.shape, lambda bi, i: (0,) * a.ndim)
    return pl.pallas_call(
        _hgrn_kernel,
        grid=(b // nb, s // ts),
        in_specs=[pl.BlockSpec((nb, ts, 4 * w), lambda bi, i: (bi, i, 0)),
                  pl.BlockSpec((1, w), lambda bi, i: (0, 0)),
                  pl.BlockSpec((1, w), lambda bi, i: (0, 0)),
                  full(m3), full(lmask), full(bd)],
        out_specs=pl.BlockSpec((nb, ts, w), lambda bi, i: (bi, i, 0)),
        out_shape=jax.ShapeDtypeStruct((b, s, w), BF16),
        scratch_shapes=[pltpu.VMEM((nb, w, w), F32)],
        compiler_params=_cparams(("parallel", "arbitrary")),
        name="hgrn_mixer",
    )(zh, lb.reshape(1, w), norm_w.reshape(1, w), m3, lmask, bd)


def _attn_bias():
    qi = np.arange(WINDOW)[:, None]
    kj = np.arange(2 * WINDOW)[None, :]
    dist = qi + WINDOW - kj
    valid = (dist >= 0) & (dist < WINDOW)
    slopes = np.exp2(-8.0 * np.arange(1, ATT_HEADS + 1) / ATT_HEADS)
    bias = np.where(valid[None], -slopes[:, None, None] * dist[None] * LOG2E, NEG)
    first = np.where(kj[None] < WINDOW, NEG, bias)
    return jnp.asarray(np.concatenate([bias, first]), dtype=F32)


def _attn_kernel(sink_ref, zq_ref, zkv_ref, qw_ref, kw_ref, bias_ref, bdq_ref, bdk_ref, o_ref,
                 qbuf, kbuf, vbuf):
    i = pl.program_id(1)
    ts = zq_ref.shape[1]
    hw = 2 * HEAD_DIM

    @pl.when(i == 0)
    def _():
        kbuf[:, 0:WINDOW, :] = jnp.zeros((4, WINDOW, hw), BF16)
        vbuf[:, 0:WINDOW, :] = jnp.zeros((4, WINDOW, hw), BF16)

    q = zq_ref[0]
    ssq = _head_sumsq(q, bdq_ref[...])
    qbuf[...] = (q * lax.rsqrt(ssq * (1.0 / HEAD_DIM) + EPS) * (qw_ref[...] * (HEAD_DIM ** -0.5 * LOG2E))).astype(BF16)
    kv = zkv_ref[0]
    k = kv[:, :KV_WIDTH]
    v = kv[:, KV_WIDTH:]
    ssk = _head_sumsq(k, bdk_ref[...])
    kn = k * lax.rsqrt(ssk * (1.0 / HEAD_DIM) + EPS) * kw_ref[...]
    kr = pltpu.roll(kn, HEAD_DIM, 1)
    vr = pltpu.roll(v, HEAD_DIM, 1)
    low = lax.broadcasted_iota(jnp.int32, (1, hw), 1) < HEAD_DIM
    for j in range(ATT_KV_HEADS):
        for half in range(2):
            keep = low if half == 0 else jnp.logical_not(low)
            ksrc = kn if j == half else kr
            vsrc = v if j == half else vr
            kbuf[2 * j + half, WINDOW:WINDOW + ts, :] = jnp.where(keep, ksrc, 0.0).astype(BF16)
            vbuf[2 * j + half, WINDOW:WINDOW + ts, :] = jnp.where(keep, vsrc, 0.0).astype(BF16)

    def block(n, carry):
        r0 = pl.multiple_of(n * WINDOW, WINDOW)
        table = jnp.where(jnp.logical_and(i == 0, n == 0), ATT_HEADS, 0)
        for g0 in range(0, ATT_HEADS, ATT_GROUP):
            heads = range(g0, g0 + ATT_GROUP)
            logits = []
            for h in heads:
                hp, half = h // 2, h % 2
                j = h // (ATT_HEADS // ATT_KV_HEADS)
                qp = qbuf[pl.ds(r0, WINDOW), hp * hw:(hp + 1) * hw]
                keys = kbuf[2 * j + half, pl.ds(r0, 2 * WINDOW), :]
                logits.append(_dot_nt(qp, keys) + bias_ref[table + h])
            ps, scales = [], []
            for h, lg in zip(heads, logits):
                sink = sink_ref[h] * LOG2E
                m = jnp.maximum(jnp.max(lg, axis=-1, keepdims=True), sink)
                p = jnp.exp2(lg - m)
                scales.append(1.0 / (jnp.sum(p, axis=-1, keepdims=True) + jnp.exp2(sink - m)))
                ps.append(p.astype(BF16))
            outs = []
            for h, p in zip(heads, ps):
                half = h % 2
                j = h // (ATT_HEADS // ATT_KV_HEADS)
                vals = vbuf[2 * j + half, pl.ds(r0, 2 * WINDOW), :]
                outs.append(_dot(p, vals))
            for k in range(0, ATT_GROUP, 2):
                hp = (g0 + k) // 2
                acc = outs[k] * scales[k] + outs[k + 1] * scales[k + 1]
                o_ref[0, pl.ds(r0, WINDOW), hp * hw:(hp + 1) * hw] = acc.astype(o_ref.dtype)
        return carry

    lax.fori_loop(0, ts // WINDOW, block, 0)
    kbuf[:, 0:WINDOW, :] = kbuf[:, ts:ts + WINDOW, :]
    vbuf[:, 0:WINDOW, :] = vbuf[:, ts:ts + WINDOW, :]


def _attn_mixer(zq, zkv, q_norm_w, k_norm_w, sinks, consts):
    b, s, _ = zq.shape
    ts = min(ATT_TILE, s)
    bias, bdq, bdk = consts
    hw = 2 * HEAD_DIM
    qw = jnp.tile(q_norm_w, ATT_HEADS).reshape(1, ATT_WIDTH)
    kw = jnp.tile(k_norm_w, ATT_KV_HEADS).reshape(1, KV_WIDTH)
    full = lambda a: pl.BlockSpec(a.shape, lambda bi, i, sk: (0,) * a.ndim)
    grid_spec = pltpu.PrefetchScalarGridSpec(
        num_scalar_prefetch=1,
        grid=(b, s // ts),
        in_specs=[pl.BlockSpec((1, ts, ATT_WIDTH), lambda bi, i, sk: (bi, i, 0)),
                  pl.BlockSpec((1, ts, 2 * KV_WIDTH), lambda bi, i, sk: (bi, i, 0)),
                  full(qw), full(kw), full(bias), full(bdq), full(bdk)],
        out_specs=pl.BlockSpec((1, ts, ATT_WIDTH), lambda bi, i, sk: (bi, i, 0)),
        scratch_shapes=[pltpu.VMEM((ts, ATT_WIDTH), BF16),
                        pltpu.VMEM((4, WINDOW + ts, hw), BF16),
                        pltpu.VMEM((4, WINDOW + ts, hw), BF16)],
    )
    return pl.pallas_call(
        _attn_kernel,
        grid_spec=grid_spec,
        out_shape=jax.ShapeDtypeStruct((b, s, ATT_WIDTH), BF16),
        compiler_params=_cparams(("parallel", "arbitrary")),
        name="swa_mixer",
    )(sinks, zq, zkv, qw, kw, bias, bdq, bdk)


def _router_constants(tm):
    t = np.arange(tm)
    before = (t[:, None] < t[None, :]).astype(np.float32)
    e = np.arange(N_EXPERTS)
    lower = (e[None, :] < e[:, None]).astype(np.float32)
    return (jnp.asarray(before, dtype=BF16), jnp.asarray(np.ones((tm, tm), np.float32), dtype=BF16),
            jnp.asarray(lower, dtype=BF16))


def _route(sel, scores):
    tm = sel.shape[1]
    group_scores = []
    for g in range(N_GROUPS):
        rows = [sel[g * EXPERTS_PER_GROUP + a:g * EXPERTS_PER_GROUP + a + 1] for a in range(EXPERTS_PER_GROUP)]
        best_pair = None
        for a in range(EXPERTS_PER_GROUP):
            for bb in range(a + 1, EXPERTS_PER_GROUP):
                pair = rows[a] + rows[bb]
                best_pair = pair if best_pair is None else jnp.maximum(best_pair, pair)
        group_scores.append(best_pair)
    top = functools.reduce(jnp.maximum, group_scores)
    best = jnp.full((1, tm), N_GROUPS - 1, jnp.int32)
    for g in reversed(range(N_GROUPS - 1)):
        best = jnp.where(group_scores[g] == top, g, best)
    row = lax.broadcasted_iota(jnp.int32, (N_EXPERTS, tm), 0)
    cand = jnp.where(row // EXPERTS_PER_GROUP == best, sel, NEG)
    m1 = jnp.max(cand, axis=0, keepdims=True)
    i1 = jnp.min(jnp.where(cand == m1, row, N_EXPERTS), axis=0, keepdims=True)
    oh1 = row == i1
    cand = jnp.where(oh1, NEG, cand)
    m2 = jnp.max(cand, axis=0, keepdims=True)
    i2 = jnp.min(jnp.where(cand == m2, row, N_EXPERTS), axis=0, keepdims=True)
    oh2 = row == i2
    s1 = jnp.sum(jnp.where(oh1, scores, 0.0), axis=0, keepdims=True)
    s2 = jnp.sum(jnp.where(oh2, scores, 0.0), axis=0, keepdims=True)
    return oh1, oh2, s1 / (s1 + s2), s2 / (s1 + s2)


def _out_kernel(yp_ref, yh_ref, ya_ref, x_ref, g1_ref, sc_ref, sh_ref, nw_ref, wo_ref, rwt_ref, rb_ref,
                before_ref, ones_ref, lower_ref, x1_ref, h2_ref, ls_ref, gate_ref, cnt_ref):
    tm = MOE_TOK_TILE
    subs = [pl.ds(u * tm, tm) for u in range(x_ref.shape[0] // tm)]
    p0, p1 = POOL_WIDTH, POOL_WIDTH + HGRN_WIDTH
    h2s = []
    for rows in subs:
        mix = (_dot(yp_ref[rows, :], wo_ref[0:p0, :]) + _dot(yh_ref[rows, :], wo_ref[p0:p1, :])
               + _dot(ya_ref[rows, :], wo_ref[p1:, :]))
        x1 = x_ref[rows, :] + g1_ref[0] * mix
        x1_ref[rows, :] = x1
        ms = jnp.mean(x1 * x1, axis=-1, keepdims=True)
        h2 = ((x1 * lax.rsqrt(ms + EPS) * nw_ref[...]) * (1.0 + sc_ref[0]) + sh_ref[0]).astype(BF16)
        h2_ref[rows, :] = h2
        h2s.append(h2)
    logits = [_dot_nt(rwt_ref[...], h2) for h2 in h2s]
    picks = []
    for lg in logits:
        ex = jnp.exp(lg - jnp.max(lg, axis=0, keepdims=True))
        scores = ex / jnp.sum(ex, axis=0, keepdims=True)
        picks.append(_route(scores + rb_ref[...], scores))
    chosen = [jnp.where(jnp.logical_or(oh1, oh2), 1.0, 0.0).astype(BF16) for oh1, oh2, _, _ in picks]
    ranks = [_dot(ch, before_ref[...]) for ch in chosen]
    counts = [_dot(ch, ones_ref[...]) for ch in chosen]
    aligned = [(jnp.floor((cn + (ROW_ALIGN - 1)) * (1.0 / ROW_ALIGN)) * ROW_ALIGN).astype(BF16) for cn in counts]
    slots = [_dot(lower_ref[...], al) + rk for al, rk in zip(aligned, ranks)]
    for u, rows in enumerate(subs):
        oh1, oh2, w1, w2 = picks[u]
        gate_ref[:, rows] = jnp.concatenate([w1, w2], axis=0)
        ls_ref[:, rows] = jnp.concatenate([jnp.sum(jnp.where(oh1, slots[u], 0.0), axis=0, keepdims=True),
                                           jnp.sum(jnp.where(oh2, slots[u], 0.0), axis=0, keepdims=True)], axis=0)
        cnt_ref[u] = counts[u][:, 0:128]


def _out_proj_router(yp, yh, ya, x, g1, sc, sh, nw, wo_b, layer, rwt_b, rb, consts):
    b, s, d = x.shape
    t = b * s
    tm = MOE_TOK_TILE
    n_sub = ROUTER_SUBTILES if s % (ROUTER_SUBTILES * tm) == 0 else 1
    ts = n_sub * tm
    per_batch = s // ts
    nt = t // tm
    before, ones, lower = consts
    tok = lambda w: pl.BlockSpec((ts, w), lambda i: (i, 0))
    vec = pl.BlockSpec((1, 1, d), lambda i: (i // per_batch, 0, 0))
    full = lambda a: pl.BlockSpec(a.shape, lambda i: (0,) * a.ndim)
    lanes = pl.BlockSpec((2, ts), lambda i: (0, i))
    return pl.pallas_call(
        _out_kernel,
        grid=(t // ts,),
        in_specs=[tok(POOL_WIDTH), tok(HGRN_WIDTH), tok(ATT_WIDTH), tok(d), vec, vec, vec,
                  pl.BlockSpec((1, d), lambda i: (0, 0)),
                  pl.BlockSpec((None,) + wo_b.shape[1:], lambda i: (layer, 0, 0)), full(rwt_b),
                  pl.BlockSpec((N_EXPERTS, 1), lambda i: (0, 0)), full(before), full(ones), full(lower)],
        out_specs=[tok(d), tok(d), lanes, lanes, pl.BlockSpec((n_sub, N_EXPERTS, 128), lambda i: (i, 0, 0))],
        out_shape=[jax.ShapeDtypeStruct((t, d), F32), jax.ShapeDtypeStruct((t, d), BF16),
                   jax.ShapeDtypeStruct((2, t), F32), jax.ShapeDtypeStruct((2, t), F32),
                   jax.ShapeDtypeStruct((nt, N_EXPERTS, 128), F32)],
        compiler_params=_cparams(("parallel",)),
        name="out_proj_router",
    )(yp.reshape(t, -1), yh.reshape(t, -1), ya.reshape(t, -1), x.reshape(t, d),
      g1.reshape(b, 1, d), sc.reshape(b, 1, d), sh.reshape(b, 1, d), nw.reshape(1, d), wo_b, rwt_b,
      rb.reshape(N_EXPERTS, 1), before, ones, lower)


def _moe_tables(cnt_out):
    cnt = jnp.round(cnt_out[:, :, 0]).astype(jnp.int32)
    cnt = (cnt + ROW_ALIGN - 1) // ROW_ALIGN * ROW_ALIGN
    total = jnp.sum(cnt, axis=0)
    padded = (total + EXPERT_TILE - 1) // EXPERT_TILE * EXPERT_TILE
    ends = jnp.cumsum(padded)
    first = ends - padded
    start = first[None, :] + jnp.cumsum(cnt, axis=0) - cnt
    loff = jnp.cumsum(cnt, axis=1) - cnt
    n_tiles = _moe_rows(cnt.shape[0] * MOE_TOK_TILE) // EXPERT_TILE
    n_used = ends[-1] // EXPERT_TILE
    tile_row = jnp.minimum(jnp.arange(n_tiles), n_used - 1) * EXPERT_TILE
    tile_expert = jnp.sum((ends[None, :] <= tile_row[:, None]).astype(jnp.int32), axis=1)
    pieces = jnp.sum(cnt, axis=1, keepdims=True) // ROW_ALIGN

    def copies(n_per_expert, offset_in_chunk, n_max):
        incl = jnp.cumsum(n_per_expert, axis=1)
        k = jnp.arange(n_max, dtype=jnp.int32)
        owner = jnp.sum((incl[:, None, :] <= k[None, :, None]).astype(jnp.int32), axis=2)
        owner = jnp.minimum(owner, N_EXPERTS - 1)
        is_owner = owner[:, :, None] == jnp.arange(N_EXPERTS, dtype=jnp.int32)[None, None, :]
        pick = lambda a: jnp.sum(jnp.where(is_owner, a[:, None, :], 0), axis=2)
        off = offset_in_chunk(k[None, :] - pick(incl - n_per_expert), pick(cnt))
        return pick(loff) + off, pick(start) + off, incl[:, -1:]

    big_src, big_dst, n_big = copies(cnt // (2 * ROW_ALIGN), lambda j, c: j * (2 * ROW_ALIGN), MAX_PIECES // 2)
    small_src, small_dst, n_small = copies(cnt % (2 * ROW_ALIGN) // ROW_ALIGN, lambda j, c: c - ROW_ALIGN, N_EXPERTS)
    moves = jnp.concatenate([big_src, big_dst, small_src, small_dst, n_big, n_small, pieces], axis=1)
    free_pieces = (n_tiles - n_used) * (EXPERT_TILE // ZERO_ROWS)
    tail = jnp.concatenate([first + total, padded - total, jnp.stack([ends[-1], free_pieces])])
    return (moves.reshape(-1).astype(jnp.int32), tail.astype(jnp.int32), tile_expert.astype(jnp.int32),
            n_used.reshape(1).astype(jnp.int32))


MOVE_BIG_SRC = 0
MOVE_BIG_DST = MOVE_BIG_SRC + MAX_PIECES // 2
MOVE_SMALL_SRC = MOVE_BIG_DST + MAX_PIECES // 2
MOVE_SMALL_DST = MOVE_SMALL_SRC + N_EXPERTS
MOVE_N_BIG = MOVE_SMALL_DST + N_EXPERTS
MOVE_N_SMALL = MOVE_N_BIG + 1
MOVE_PIECES = MOVE_N_SMALL + 1
MOVE_WIDTH = MOVE_PIECES + 1


def _tile_copies(moves_ref, tile, make_copy):
    base = tile * MOVE_WIDTH
    for src0, dst0, n_at, size in ((MOVE_BIG_SRC, MOVE_BIG_DST, MOVE_N_BIG, 2 * ROW_ALIGN),
                                   (MOVE_SMALL_SRC, MOVE_SMALL_DST, MOVE_N_SMALL, ROW_ALIGN)):
        def body(j, carry):
            make_copy(pl.multiple_of(moves_ref[base + src0 + j], ROW_ALIGN),
                      pl.multiple_of(moves_ref[base + dst0 + j], ROW_ALIGN), size).start()
            return carry

        lax.fori_loop(0, moves_ref[base + n_at], body, 0)


def _moe_rows(n_tokens):
    per_tile = 2 * MOE_TOK_TILE + N_EXPERTS * (ROW_ALIGN - 1)
    rows = (n_tokens // MOE_TOK_TILE) * per_tile + N_EXPERTS * (EXPERT_TILE - ROW_ALIGN)
    return (rows + EXPERT_TILE - 1) // EXPERT_TILE * EXPERT_TILE


def _chunk_copies(count, make_copy, wait=False):
    for size in CHUNK_SIZES:
        offset = jnp.bitwise_and(count, ~(2 * size - 1))

        @pl.when(jnp.bitwise_and(count, size) != 0)
        def _():
            copy = make_copy(pl.multiple_of(offset, ROW_ALIGN), size)
            copy.wait() if wait else copy.start()


def _wait_rows(n_pieces, make_copy):
    for p in WAIT_PIECES:
        @pl.when(jnp.bitwise_and(n_pieces, p) != 0)
        def _():
            make_copy(p * ROW_ALIGN).wait()


def _slot_rows(tm):
    return lax.broadcasted_iota(jnp.int32, (MOE_SLOTS, tm), 0).astype(F32)


def _sort_kernel(moves_ref, tail_ref, h2_ref, ls_ref, xs_ref, buf_ref, sem_ref):
    i = pl.program_id(0)
    n = pl.num_programs(0)
    tm = MOE_TOK_TILE
    n_sub = h2_ref.shape[0] // tm
    slot_of = lambda tile, u: u if n_sub == 2 else tile % 2

    def sent(tile, s):
        _wait_rows(moves_ref[tile * MOVE_WIDTH + MOVE_PIECES], lambda rows: pltpu.make_async_copy(
            buf_ref.at[s, pl.ds(0, rows), :], xs_ref.at[pl.ds(0, rows), :], sem_ref.at[s]))

    for u in range(n_sub):
        tile = i * n_sub + u
        slot = slot_of(tile, u)

        @pl.when(tile >= 2)
        def _():
            sent(tile - 2, slot)

        ls = ls_ref[:, u * tm:(u + 1) * tm]
        srow = _slot_rows(tm)
        perm = jnp.where(jnp.logical_or(srow == ls[0:1], srow == ls[1:2]), 1.0, 0.0).astype(BF16)
        buf_ref[slot] = _dot(perm, h2_ref[u * tm:(u + 1) * tm, :]).astype(BF16)
        _tile_copies(moves_ref, tile, lambda src, dst, size: pltpu.make_async_copy(
            buf_ref.at[slot, pl.ds(src, size), :], xs_ref.at[pl.ds(dst, size), :], sem_ref.at[slot]))

    @pl.when(i == n - 1)
    def _():
        last = n * n_sub - 1

        @pl.when(last >= 1)
        def _():
            sent(last - 1, slot_of(last - 1, n_sub - 2))
        sent(last, slot_of(last, n_sub - 1))

        buf_ref[0] = jnp.zeros(buf_ref.shape[1:], BF16)
        zeros_to = lambda row, size: pltpu.make_async_copy(
            buf_ref.at[0, pl.ds(0, size), :], xs_ref.at[pl.ds(row, size), :], sem_ref.at[0])

        for wait in (False, True):
            def expert_tail(e, carry):
                st = pl.multiple_of(tail_ref[e], ROW_ALIGN)
                _chunk_copies(tail_ref[N_EXPERTS + e], lambda off, size: zeros_to(st + off, size), wait)
                return carry

            def free_tile(j, carry):
                copy = zeros_to(pl.multiple_of(tail_ref[2 * N_EXPERTS] + j * ZERO_ROWS, ROW_ALIGN), ZERO_ROWS)
                copy.wait() if wait else copy.start()
                return carry

            lax.fori_loop(0, N_EXPERTS, expert_tail, 0)
            lax.fori_loop(0, tail_ref[2 * N_EXPERTS + 1], free_tile, 0)


def _sort_tokens(h2, ls, moves, tail, n_rows):
    t, d = h2.shape
    tm = MOE_TOK_TILE * (2 if (t // MOE_TOK_TILE) % 2 == 0 else 1)
    grid_spec = pltpu.PrefetchScalarGridSpec(
        num_scalar_prefetch=2,
        grid=(t // tm,),
        in_specs=[pl.BlockSpec((tm, d), lambda i, *_: (i, 0)),
                  pl.BlockSpec((2, tm), lambda i, *_: (0, i))],
        out_specs=pl.BlockSpec(memory_space=pl.ANY),
        scratch_shapes=[pltpu.VMEM((2, MOE_SLOTS, d), BF16), pltpu.SemaphoreType.DMA((2,))],
    )
    return pl.pallas_call(
        _sort_kernel,
        grid_spec=grid_spec,
        out_shape=jax.ShapeDtypeStruct((n_rows, d), BF16),
        compiler_params=_cparams(("arbitrary",)),
        name="moe_sort",
    )(moves, tail, h2, ls)


def _gmm_kernel(te_ref, nu_ref, xs_ref, wg_ref, wu_ref, wd_ref, ys_ref, wgb, wub, wdb):
    i = pl.program_id(0)
    used = i < nu_ref[0]
    new_expert = jnp.logical_or(i == 0, te_ref[i] != te_ref[jnp.maximum(i - 1, 0)])

    @pl.when(jnp.logical_and(used, new_expert))
    def _():
        wgb[...] = wg_ref[...].astype(BF16)
        wub[...] = wu_ref[...].astype(BF16)
        wdb[...] = wd_ref[...].astype(BF16)

    @pl.when(used)
    def _():
        x = xs_ref[...]
        he = _silu(_dot(x, wgb[...])) * _dot(x, wub[...])
        ys_ref[...] = _dot(he.astype(BF16), wdb[...]).astype(BF16)

    @pl.when(jnp.logical_not(used))
    def _():
        ys_ref[...] = jnp.zeros(ys_ref.shape, BF16)


def _grouped_mlp(xs, tile_expert, n_used, wg, wu, wd, layer):
    n_rows, d = xs.shape
    te = EXPERT_TILE
    row_map = lambda i, tex, nu: (jnp.minimum(i, nu[0] - 1), 0)
    grid_spec = pltpu.PrefetchScalarGridSpec(
        num_scalar_prefetch=2,
        grid=(n_rows // te,),
        in_specs=[pl.BlockSpec((te, d), row_map),
                  pl.BlockSpec((None, None, d, D_EXPERT), lambda i, tex, nu: (layer, tex[i], 0, 0)),
                  pl.BlockSpec((None, None, d, D_EXPERT), lambda i, tex, nu: (layer, tex[i], 0, 0)),
                  pl.BlockSpec((None, None, D_EXPERT, d), lambda i, tex, nu: (layer, tex[i], 0, 0))],
        out_specs=pl.BlockSpec((te, d), lambda i, tex, nu: (i, 0)),
        scratch_shapes=[pltpu.VMEM((d, D_EXPERT), BF16), pltpu.VMEM((d, D_EXPERT), BF16),
                        pltpu.VMEM((D_EXPERT, d), BF16)],
    )
    return pl.pallas_call(
        _gmm_kernel,
        grid_spec=grid_spec,
        out_shape=jax.ShapeDtypeStruct((n_rows, d), BF16),
        compiler_params=_cparams(("arbitrary",)),
        name="moe_grouped_mlp",
    )(tile_expert, n_used, xs, wg, wu, wd)


def _combine_kernel(moves_ref, ls_ref, gate_ref, x1_ref, g2_ref, ys_ref, o_ref, buf_ref, sem_ref):
    i = pl.program_id(0)
    n = pl.num_programs(0)
    tm = MOE_TOK_TILE
    n_sub = x1_ref.shape[0] // tm
    n_tiles = n * n_sub
    n_buf = buf_ref.shape[0]

    def fetch(tile):
        s = tile % n_buf
        _tile_copies(moves_ref, tile, lambda dst, src, size: pltpu.make_async_copy(
            ys_ref.at[pl.ds(src, size), :], buf_ref.at[s, pl.ds(dst, size), :], sem_ref.at[s]))

    @pl.when(i == 0)
    def _():
        for first in range(COMBINE_AHEAD):
            @pl.when(first < n_tiles)
            def _():
                fetch(first)

    for u in range(n_sub):
        tile = i * n_sub + u
        slot = tile % n_buf

        @pl.when(tile + COMBINE_AHEAD < n_tiles)
        def _():
            fetch(tile + COMBINE_AHEAD)

        n_pieces = moves_ref[tile * MOVE_WIDTH + MOVE_PIECES]
        _wait_rows(n_pieces, lambda rows: pltpu.make_async_copy(
            ys_ref.at[pl.ds(0, rows), :], buf_ref.at[slot, pl.ds(0, rows), :], sem_ref.at[slot]))

        rows_u = pl.ds(u * tm, tm)
        srow = _slot_rows(tm)
        ls = ls_ref[:, u * tm:(u + 1) * tm]
        gate = gate_ref[:, u * tm:(u + 1) * tm]
        weights = (jnp.where(srow == ls[0:1], gate[0:1], 0.0)
                   + jnp.where(srow == ls[1:2], gate[1:2], 0.0)).astype(BF16)
        written = lax.broadcasted_iota(jnp.int32, (MOE_SLOTS, 1), 0) < n_pieces * ROW_ALIGN
        rows = jnp.where(written, buf_ref[slot], jnp.zeros((), BF16))
        y = _dot_tn(weights, rows)
        o_ref[rows_u, :] = x1_ref[rows_u, :] + g2_ref[0] * y


def _combine(ys, ls, gate, x1, g2, moves, seq_len):
    t, d = x1.shape
    tm = MOE_TOK_TILE * (2 if (seq_len // MOE_TOK_TILE) % 2 == 0 else 1)
    per_batch = seq_len // tm
    grid_spec = pltpu.PrefetchScalarGridSpec(
        num_scalar_prefetch=1,
        grid=(t // tm,),
        in_specs=[pl.BlockSpec((2, tm), lambda i, *_: (0, i)),
                  pl.BlockSpec((2, tm), lambda i, *_: (0, i)),
                  pl.BlockSpec((tm, d), lambda i, *_: (i, 0)),
                  pl.BlockSpec((1, 1, d), lambda i, *_: (i // per_batch, 0, 0)),
                  pl.BlockSpec(memory_space=pl.ANY)],
        out_specs=pl.BlockSpec((tm, d), lambda i, *_: (i, 0)),
        scratch_shapes=[pltpu.VMEM((COMBINE_AHEAD + 1, MOE_SLOTS, d), BF16),
                        pltpu.SemaphoreType.DMA((COMBINE_AHEAD + 1,))],
    )
    return pl.pallas_call(
        _combine_kernel,
        grid_spec=grid_spec,
        out_shape=jax.ShapeDtypeStruct((t, d), F32),
        compiler_params=_cparams(("arbitrary",)),
        name="moe_combine",
    )(moves, ls, gate, x1, g2.reshape(-1, 1, d), ys)


def kernel(x, c, ada_w, ada_b, norm1_w, norm2_w, w_in, pool_w, pool_scale, hgrn_lb_raw, hgrn_norm_w, q_norm_w,
           k_norm_w, attn_sinks, w_out, router_w, router_bias, expert_w_gate, expert_w_up, expert_w_down):
    b, s, d = x.shape
    depth = ada_w.shape[0]
    t = b * s
    n_rows = _moe_rows(t)

    p = jax.nn.softmax(hgrn_lb_raw.astype(F32), axis=0)
    lower_bounds = jnp.maximum(jnp.cumsum(p, axis=0) - p[0:1], 0.0)

    mod = _modulation(c, ada_w, ada_b)
    hgrn_consts = _hgrn_constants() + (_head_ones(HGRN_WIDTH, HGRN_KDIM),)
    attn_consts = (_attn_bias(), _head_ones(ATT_WIDTH, HEAD_DIM), _head_ones(KV_WIDTH, HEAD_DIM))
    router_consts = _router_constants(MOE_TOK_TILE)
    rwt_b = router_w.T.astype(BF16)
    w_in_b = w_in.astype(BF16)
    w_out_b = w_out.astype(BF16)

    for l in range(depth):
        sh1, sc1, g1, sh2, sc2, g2 = [mod[l, :, j * d:(j + 1) * d] for j in range(6)]
        yp, zh, zq, zkv = _in_proj(x, sc1, sh1, norm1_w[l], w_in_b, l, _block_diag(pool_w[l]).astype(BF16),
                                   pool_scale[l])
        yh = _hgrn_mixer(zh, lower_bounds[l], hgrn_norm_w[l], hgrn_consts)
        ya = _attn_mixer(zq, zkv, q_norm_w[l], k_norm_w[l], attn_sinks[l], attn_consts)
        x1, h2, ls, gate, cnt_out = _out_proj_router(
            yp, yh, ya, x, g1, sc2, sh2, norm2_w[l], w_out_b, l, rwt_b, router_bias, router_consts)
        moves, tail, tile_expert, n_used = _moe_tables(cnt_out)
        xs = _sort_tokens(h2, ls, moves, tail, n_rows)
        ys = _grouped_mlp(xs, tile_expert, n_used, expert_w_gate, expert_w_up, expert_w_down, l)
        x = _combine(ys, ls, gate, x1, g2, moves, s).reshape(b, s, d)
    return x
```

```python
import functools

import numpy as np
import jax
import jax.numpy as jnp
from jax import lax
from jax.experimental import pallas as pl
from jax.experimental.pallas import tpu as pltpu

F32 = jnp.float32
BF16 = jnp.bfloat16

D_MODEL = 1024
POOL_WINDOWS = (2, 4, 8, 16)
POOL_WIDTH = 256
POOL_GROUP = 64
POOL_HALO = 32
HGRN_HEADS = 4
HGRN_KDIM = 64
HGRN_WIDTH = 256
HGRN_CHUNK = 64
ATT_HEADS = 8
ATT_KV_HEADS = 2
HEAD_DIM = 64
ATT_WIDTH = 512
KV_WIDTH = 128
WINDOW = 128
N_EXPERTS = 16
N_GROUPS = 4
EXPERTS_PER_GROUP = 4
D_EXPERT = 512
EPS = 1e-6
MAX_ONE_MINUS_F = 1.0 - 1e-6
LOG2E = 1.4426950408889634
NEG = -1e30

VMEM_LIMIT = 48 * 1024 * 1024

IN_TILE = 1024
ATT_TILE = 1024
MOE_TOK_TILE = 256
ROW_ALIGN = 16
MOE_SLOTS = 768
EXPERT_TILE = 1024
ZERO_ROWS = 512
CHUNK_SIZES = (512, 256, 128, 64, 32, 16)
COMBINE_AHEAD = 3
SORT_BUFFERS = 4
MAX_PIECES = MOE_SLOTS // ROW_ALIGN
WAIT_PIECES = (32, 16, 8, 4, 2, 1)


def _sigmoid(x):
    return 1.0 / (1.0 + jnp.exp(-x))


def _silu(x):
    return x * _sigmoid(x)


def _cparams(sem, **kw):
    return pltpu.CompilerParams(dimension_semantics=sem, vmem_limit_bytes=VMEM_LIMIT, **kw)


def _dot(a, b):
    return jnp.dot(a, b, preferred_element_type=F32)


def _dot_nt(a, b):
    return lax.dot_general(a, b, (((1,), (1,)), ((), ())), preferred_element_type=F32)


def _dot_tn(a, b):
    return lax.dot_general(a, b, (((0,), (0,)), ((), ())), preferred_element_type=F32)


def _head_sumsq(x, bd):
    return _dot((x * x).astype(BF16), bd)


def _mod_kernel(c_ref, w_ref, b_ref, o_ref):
    cond = _silu(c_ref[...])
    o_ref[0] = _dot(cond.astype(BF16), w_ref[0].astype(BF16)) + b_ref[0]


def _modulation(c, ada_w, ada_b):
    depth, d, n = ada_w.shape
    b = c.shape[0]
    nb = n // d
    return pl.pallas_call(
        _mod_kernel,
        grid=(depth, nb),
        in_specs=[pl.BlockSpec((b, d), lambda l, j: (0, 0)),
                  pl.BlockSpec((1, d, d), lambda l, j: (l, 0, j)),
                  pl.BlockSpec((1, 1, d), lambda l, j: (l, 0, j))],
        out_specs=pl.BlockSpec((1, b, d), lambda l, j: (l, 0, j)),
        out_shape=jax.ShapeDtypeStruct((depth, b, n), F32),
        compiler_params=_cparams(("parallel", "parallel")),
        name="adaln_mod",
    )(c, ada_w, ada_b.reshape(depth, 1, n))


def _in_kernel(x_ref, sc_ref, sh_ref, nw_ref, w_ref, pw_ref, ps_ref, yp_ref, zh_ref, zq_ref, zkv_ref,
               buf_ref, sa_ref, sb_ref):
    x = x_ref[0]
    ms = jnp.mean(x * x, axis=-1, keepdims=True)
    h = (x * lax.rsqrt(ms + EPS) * nw_ref[...]) * (1.0 + sc_ref[0]) + sh_ref[0]
    z = _dot(h.astype(BF16), w_ref[...])
    zh_ref[0] = z[:, POOL_WIDTH:POOL_WIDTH + 4 * HGRN_WIDTH]
    zq_ref[0] = z[:, POOL_WIDTH + 4 * HGRN_WIDTH:POOL_WIDTH + 4 * HGRN_WIDTH + ATT_WIDTH]
    zkv_ref[0] = z[:, POOL_WIDTH + 4 * HGRN_WIDTH + ATT_WIDTH:]
    yp_ref[0] = _pool(z[:, :POOL_WIDTH], pl.program_id(1), pw_ref, ps_ref, buf_ref, sa_ref, sb_ref).astype(yp_ref.dtype)


def _in_proj(x, sc, sh, nw, w_in_b, layer, pool_bd, pool_scale):
    b, s, d = x.shape
    n = w_in_b.shape[2]
    ts = min(IN_TILE, s)
    tok = lambda w: pl.BlockSpec((1, ts, w), lambda bi, i: (bi, i, 0))
    vec = pl.BlockSpec((1, 1, d), lambda bi, i: (bi, 0, 0))
    full = lambda a: pl.BlockSpec(a.shape, lambda bi, i: (0,) * a.ndim)
    pool_scale = pool_scale.reshape(1, POOL_WIDTH)
    outs = ((POOL_WIDTH, BF16), (4 * HGRN_WIDTH, F32), (ATT_WIDTH, F32), (2 * KV_WIDTH, F32))
    return pl.pallas_call(
        _in_kernel,
        grid=(b, s // ts),
        in_specs=[tok(d), vec, vec,
                  pl.BlockSpec((1, d), lambda bi, i: (0, 0)),
                  pl.BlockSpec((None, d, n), lambda bi, i: (layer, 0, 0)), full(pool_bd), full(pool_scale)],
        out_specs=[tok(w) for w, _ in outs],
        out_shape=[jax.ShapeDtypeStruct((b, s, w), dt) for w, dt in outs],
        scratch_shapes=[pltpu.VMEM((POOL_HALO + ts, POOL_WIDTH), F32)] * 3,
        compiler_params=_cparams(("parallel", "arbitrary")),
        name="in_proj_pool",
    )(x, sc.reshape(b, 1, d), sh.reshape(b, 1, d), nw.reshape(1, d), w_in_b, pool_bd, pool_scale)


def _pool(a, i, w_ref, scale_ref, buf_ref, sa_ref, sb_ref):
    ts = a.shape[0]
    halo = POOL_HALO
    end = halo + ts
    half = 2 * POOL_GROUP

    @pl.when(i == 0)
    def _():
        buf_ref[0:halo, :] = jnp.zeros((halo, POOL_WIDTH), F32)

    buf_ref[halo:end, :] = a
    s2 = buf_ref[8:end, :] + buf_ref[7:end - 1, :]
    sa_ref[8:end, :] = s2
    s4 = sa_ref[16:end, :] + sa_ref[14:end - 2, :]
    sb_ref[16:end, :] = s4
    s8 = sb_ref[24:end, half:] + sb_ref[20:end - 4, half:]
    sa_ref[24:end, half:] = s8
    s16 = sa_ref[halo:end, half:] + sa_ref[halo - 8:end - 8, half:]
    lane = lax.broadcasted_iota(jnp.int32, (1, POOL_WIDTH), 1)
    win = jnp.left_shift(2, lane // POOL_GROUP)
    low = lane[:, 0:half] % half < POOL_GROUP
    acc = jnp.concatenate([jnp.where(low, s2[halo - 8:, 0:half], s4[halo - 16:, 0:half]),
                           jnp.where(low, s8[halo - 24:], s16)], axis=1)
    pos = i * ts + lax.broadcasted_iota(jnp.int32, (ts, 1), 0)
    count = jnp.minimum(pos + 1, win).astype(F32)
    pooled = acc / count - a
    buf_ref[0:halo, :] = a[ts - halo:, :]
    return _dot(pooled.astype(BF16), w_ref[...]) * scale_ref[...]


def _block_diag(blocks):
    g, n, _ = blocks.shape
    eye = jnp.eye(g, dtype=blocks.dtype)
    return (eye[:, None, :, None] * blocks[:, :, None, :]).reshape(g * n, g * n)


def _head_ones(width, head):
    idx = np.arange(width) // head
    return jnp.asarray((idx[:, None] == idx[None, :]).astype(np.float32), dtype=BF16)


HGRN_LEVELS = (32, 16, 8, 4, 2, 1)
HGRN_MATMUL_LEVELS = (2, 1)
N_LEVEL_MASKS = len(HGRN_LEVELS) + 1
HGRN_SEQS = 8
HGRN_TILE = 256
ATT_GROUP = 8
ROUTER_SUBTILES = 4


def _hgrn_constants():
    c = HGRN_CHUNK
    m = np.zeros((len(HGRN_MATMUL_LEVELS) + 1, c, c), np.float32)
    masks = np.zeros((N_LEVEL_MASKS, c, c), np.float32)
    for lvl, n in enumerate(HGRN_LEVELS):
        for t in range(c):
            blk = t // (2 * n)
            mid = blk * 2 * n + n
            if t >= mid:
                masks[lvl, t, blk * 2 * n:mid] = 1.0
            if n in HGRN_MATMUL_LEVELS:
                row = m[HGRN_MATMUL_LEVELS.index(n), t]
                if t >= mid:
                    row[mid:t + 1] = 1.0
                else:
                    row[t + 1:mid] = 1.0
    for t in range(c):
        m[-1, t, :t + 1] = 1.0
        masks[-1, t, t] = 1.0
    m = m.reshape(-1, c)
    m3 = np.concatenate([m, m, m], axis=1)
    masks = np.tile(masks, (1, HGRN_HEADS, 1))
    return jnp.asarray(m3, dtype=BF16), jnp.asarray(masks, dtype=F32)


def _hgrn_chunks(units, bd, m3, lmask_ref):
    c = HGRN_CHUNK
    w = HGRN_WIDTH
    n_fine = len(HGRN_MATMUL_LEVELS)
    lane_head = lax.broadcasted_iota(jnp.int32, (1, w), 1) // HGRN_KDIM
    sums = [_dot(m3, u[2]) for u in units]
    bs = [s[n_fine * c:] for s in sums]

    def level_decay(u, n):
        if n in HGRN_MATMUL_LEVELS:
            k = HGRN_MATMUL_LEVELS.index(n)
            return jnp.exp(sums[u][k * c:(k + 1) * c])
        blocks = bs[u].reshape(c // (2 * n), 2 * n, w)
        ref = blocks[:, n - 1:n, :]
        right = lax.broadcasted_iota(jnp.int32, (1, 2 * n, 1), 1) >= n
        return jnp.exp(jnp.where(right, blocks - ref, ref - blocks).reshape(c, w))

    groups = c // 8
    parts = [[[None] * groups for _ in range(HGRN_HEADS)] for _ in units]

    def add_part(u, h, g, piece):
        parts[u][h][g] = piece if parts[u][h][g] is None else parts[u][h][g] + piece

    for lvl in range(N_LEVEL_MASKS):
        n = HGRN_LEVELS[lvl] if lvl < len(HGRN_LEVELS) else 0
        wanted = [g for g in range(groups) if (8 * g) % (2 * n) >= n] if n >= 8 else list(range(groups))
        for u, (qf, kk, _, _, _, _) in enumerate(units):
            if lvl < len(HGRN_LEVELS):
                e = level_decay(u, n)
                ql = qf * e
                kl = (kk * e).astype(BF16)
            else:
                ql = qf
                kl = kk.astype(BF16)
            ql = jnp.concatenate([ql[8 * g:8 * g + 8] for g in wanted], axis=0).astype(BF16)
            zero = jnp.zeros_like(ql)
            qs = jnp.concatenate([jnp.where(lane_head == h, ql, zero) for h in range(HGRN_HEADS)], axis=0)
            res = _dot_nt(qs, kl)
            for h in range(HGRN_HEADS):
                for k, g in enumerate(wanted):
                    r0 = (h * len(wanted) + k) * 8
                    add_part(u, h, g, res[r0:r0 + 8] * lmask_ref[lvl, h * c + 8 * g:h * c + 8 * g + 8, :])
    scores = [jnp.concatenate([parts[u][h][g] for h in range(HGRN_HEADS) for g in range(groups)], axis=0)
              for u in range(len(units))]
    rs = [_dot(scores[u].astype(BF16), unit[3]) for u, unit in enumerate(units)]
    inters = [_dot_nt((unit[0] * jnp.exp(bs[u])).astype(BF16), unit[5].astype(BF16)) for u, unit in enumerate(units)]
    upds = [_dot_tn(unit[3], (unit[1] * jnp.exp(bs[u][c - 1:c, :] - bs[u])).astype(BF16))
            for u, unit in enumerate(units)]
    outs = []
    for u in range(len(units)):
        o = inters[u]
        for h in range(HGRN_HEADS):
            o = o + jnp.where(lane_head == h, rs[u][h * c:(h + 1) * c], 0.0)
        outs.append(o)
    sss = [_head_sumsq(o, bd) for o in outs]
    res = []
    for u, unit in enumerate(units):
        st = unit[5] * jnp.exp(bs[u][c - 1:c, :]) + jnp.where(bd > 0, upds[u], 0.0)
        y = outs[u] * lax.rsqrt(sss[u] * (1.0 / HGRN_KDIM) + EPS) * unit[4]
        res.append((y, st))
    return res


def _hgrn_kernel(zh_ref, lb_ref, gw_ref, m3_ref, lmask_ref, bd_ref, o_ref, st_ref):
    i = pl.program_id(1)
    nb, ts = zh_ref.shape[0], zh_ref.shape[1]
    c = HGRN_CHUNK
    w = HGRN_WIDTH

    @pl.when(i == 0)
    def _():
        st_ref[...] = jnp.zeros(st_ref.shape, F32)

    one_minus_lb = 1.0 - lb_ref[...]
    gw = gw_ref[...]
    bd = bd_ref[...]
    m3 = m3_ref[...]

    def chunk(ci, carry):
        rows = pl.ds(pl.multiple_of(ci * c, c), c)
        units = []
        for s in range(nb):
            q, f, v, g = [zh_ref[s, rows, j * w:(j + 1) * w] for j in range(4)]
            kk = one_minus_lb * _sigmoid(-f)
            lf = jnp.log(1.0 - jnp.minimum(kk, MAX_ONE_MINUS_F))
            hi = lf.astype(BF16)
            r1 = lf - hi.astype(F32)
            mid = r1.astype(BF16)
            lo = (r1 - mid.astype(F32)).astype(BF16)
            units.append((_silu(q), kk, jnp.concatenate([hi, mid, lo], axis=0), v.astype(BF16), gw * _silu(g),
                          st_ref[s]))
        for s, (y, st) in enumerate(_hgrn_chunks(units, bd, m3, lmask_ref)):
            st_ref[s] = st
            o_ref[s, rows, :] = y.astype(o_ref.dtype)
        return carry

    lax.fori_loop(0, ts // c, chunk, 0)


def _hgrn_mixer(zh, lb, norm_w, consts):
    b, s, _ = zh.shape
    w = HGRN_WIDTH
    ts = min(HGRN_TILE, s)
    nb = HGRN_SEQS if b % HGRN_SEQS == 0 else 1
    m3, lmask, bd = consts
    full = lambda a: pl.BlockSpec(a.shape, lambda bi, i: (0,) * a.ndim)
    return pl.pallas_call(
        _hgrn_kernel,
        grid=(b // nb, s // ts),
        in_specs=[pl.BlockSpec((nb, ts, 4 * w), lambda bi, i: (bi, i, 0)),
                  pl.BlockSpec((1, w), lambda bi, i: (0, 0)),
                  pl.BlockSpec((1, w), lambda bi, i: (0, 0)),
                  full(m3), full(lmask), full(bd)],
        out_specs=pl.BlockSpec((nb, ts, w), lambda bi, i: (bi, i, 0)),
        out_shape=jax.ShapeDtypeStruct((b, s, w), BF16),
        scratch_shapes=[pltpu.VMEM((nb, w, w), F32)],
        compiler_params=_cparams(("parallel", "arbitrary")),
        name="hgrn_mixer",
    )(zh, lb.reshape(1, w), norm_w.reshape(1, w), m3, lmask, bd)


def _attn_bias():
    qi = np.arange(WINDOW)[:, None]
    kj = np.arange(2 * WINDOW)[None, :]
    dist = qi + WINDOW - kj
    valid = (dist >= 0) & (dist < WINDOW)
    slopes = np.exp2(-8.0 * np.arange(1, ATT_HEADS + 1) / ATT_HEADS)
    bias = np.where(valid[None], -slopes[:, None, None] * dist[None] * LOG2E, NEG)
    first = np.where(kj[None] < WINDOW, NEG, bias)
    return jnp.asarray(np.concatenate([bias, first]), dtype=F32)


def _attn_kernel(sink_ref, zq_ref, zkv_ref, qw_ref, kw_ref, bias_ref, bdq_ref, bdk_ref, o_ref,
                 qbuf, kbuf, vbuf):
    i = pl.program_id(1)
    ts = zq_ref.shape[1]
    hw = 2 * HEAD_DIM

    @pl.when(i == 0)
    def _():
        kbuf[:, 0:WINDOW, :] = jnp.zeros((4, WINDOW, hw), BF16)
        vbuf[:, 0:WINDOW, :] = jnp.zeros((4, WINDOW, hw), BF16)

    q = zq_ref[0]
    ssq = _head_sumsq(q, bdq_ref[...])
    qbuf[...] = (q * lax.rsqrt(ssq * (1.0 / HEAD_DIM) + EPS) * (qw_ref[...] * (HEAD_DIM ** -0.5 * LOG2E))).astype(BF16)
    kv = zkv_ref[0]
    k = kv[:, :KV_WIDTH]
    v = kv[:, KV_WIDTH:]
    ssk = _head_sumsq(k, bdk_ref[...])
    kn = k * lax.rsqrt(ssk * (1.0 / HEAD_DIM) + EPS) * kw_ref[...]
    kr = pltpu.roll(kn, HEAD_DIM, 1)
    vr = pltpu.roll(v, HEAD_DIM, 1)
    low = lax.broadcasted_iota(jnp.int32, (1, hw), 1) < HEAD_DIM
    for j in range(ATT_KV_HEADS):
        for half in range(2):
            keep = low if half == 0 else jnp.logical_not(low)
            ksrc = kn if j == half else kr
            vsrc = v if j == half else vr
            kbuf[2 * j + half, WINDOW:WINDOW + ts, :] = jnp.where(keep, ksrc, 0.0).astype(BF16)
            vbuf[2 * j + half, WINDOW:WINDOW + ts, :] = jnp.where(keep, vsrc, 0.0).astype(BF16)

    def block(n, carry):
        r0 = pl.multiple_of(n * WINDOW, WINDOW)
        table = jnp.where(jnp.logical_and(i == 0, n == 0), ATT_HEADS, 0)
        for g0 in range(0, ATT_HEADS, ATT_GROUP):
            heads = range(g0, g0 + ATT_GROUP)
            logits = []
            for h in heads:
                hp, half = h // 2, h % 2
                j = h // (ATT_HEADS // ATT_KV_HEADS)
                qp = qbuf[pl.ds(r0, WINDOW), hp * hw:(hp + 1) * hw]
                keys = kbuf[2 * j + half, pl.ds(r0, 2 * WINDOW), :]
                logits.append(_dot_nt(qp, keys) + bias_ref[table + h])
            ps, scales = [], []
            for h, lg in zip(heads, logits):
                sink = sink_ref[h] * LOG2E
                m = jnp.maximum(jnp.max(lg, axis=-1, keepdims=True), sink)
                p = jnp.exp2(lg - m)
                scales.append(1.0 / (jnp.sum(p, axis=-1, keepdims=True) + jnp.exp2(sink - m)))
                ps.append(p.astype(BF16))
            outs = []
            for h, p in zip(heads, ps):
                half = h % 2
                j = h // (ATT_HEADS // ATT_KV_HEADS)
                vals = vbuf[2 * j + half, pl.ds(r0, 2 * WINDOW), :]
                outs.append(_dot(p, vals))
            for k in range(0, ATT_GROUP, 2):
                hp = (g0 + k) // 2
                acc = outs[k] * scales[k] + outs[k + 1] * scales[k + 1]
                o_ref[0, pl.ds(r0, WINDOW), hp * hw:(hp + 1) * hw] = acc.astype(o_ref.dtype)
        return carry

    lax.fori_loop(0, ts // WINDOW, block, 0)
    kbuf[:, 0:WINDOW, :] = kbuf[:, ts:ts + WINDOW, :]
    vbuf[:, 0:WINDOW, :] = vbuf[:, ts:ts + WINDOW, :]


def _attn_mixer(zq, zkv, q_norm_w, k_norm_w, sinks, consts):
    b, s, _ = zq.shape
    ts = min(ATT_TILE, s)
    bias, bdq, bdk = consts
    hw = 2 * HEAD_DIM
    qw = jnp.tile(q_norm_w, ATT_HEADS).reshape(1, ATT_WIDTH)
    kw = jnp.tile(k_norm_w, ATT_KV_HEADS).reshape(1, KV_WIDTH)
    full = lambda a: pl.BlockSpec(a.shape, lambda bi, i, sk: (0,) * a.ndim)
    grid_spec = pltpu.PrefetchScalarGridSpec(
        num_scalar_prefetch=1,
        grid=(b, s // ts),
        in_specs=[pl.BlockSpec((1, ts, ATT_WIDTH), lambda bi, i, sk: (bi, i, 0)),
                  pl.BlockSpec((1, ts, 2 * KV_WIDTH), lambda bi, i, sk: (bi, i, 0)),
                  full(qw), full(kw), full(bias), full(bdq), full(bdk)],
        out_specs=pl.BlockSpec((1, ts, ATT_WIDTH), lambda bi, i, sk: (bi, i, 0)),
        scratch_shapes=[pltpu.VMEM((ts, ATT_WIDTH), BF16),
                        pltpu.VMEM((4, WINDOW + ts, hw), BF16),
                        pltpu.VMEM((4, WINDOW + ts, hw), BF16)],
    )
    return pl.pallas_call(
        _attn_kernel,
        grid_spec=grid_spec,
        out_shape=jax.ShapeDtypeStruct((b, s, ATT_WIDTH), BF16),
        compiler_params=_cparams(("parallel", "arbitrary")),
        name="swa_mixer",
    )(sinks, zq, zkv, qw, kw, bias, bdq, bdk)


def _router_constants(tm):
    t = np.arange(tm)
    before = (t[:, None] < t[None, :]).astype(np.float32)
    e = np.arange(N_EXPERTS)
    lower = (e[None, :] < e[:, None]).astype(np.float32)
    return (jnp.asarray(before, dtype=BF16), jnp.asarray(np.ones((tm, tm), np.float32), dtype=BF16),
            jnp.asarray(lower, dtype=BF16))


def _route(sel, scores):
    tm = sel.shape[1]
    group_scores = []
    for g in range(N_GROUPS):
        rows = [sel[g * EXPERTS_PER_GROUP + a:g * EXPERTS_PER_GROUP + a + 1] for a in range(EXPERTS_PER_GROUP)]
        best_pair = None
        for a in range(EXPERTS_PER_GROUP):
            for bb in range(a + 1, EXPERTS_PER_GROUP):
                pair = rows[a] + rows[bb]
                best_pair = pair if best_pair is None else jnp.maximum(best_pair, pair)
        group_scores.append(best_pair)
    top = functools.reduce(jnp.maximum, group_scores)
    best = jnp.full((1, tm), N_GROUPS - 1, jnp.int32)
    for g in reversed(range(N_GROUPS - 1)):
        best = jnp.where(group_scores[g] == top, g, best)
    row = lax.broadcasted_iota(jnp.int32, (N_EXPERTS, tm), 0)
    cand = jnp.where(row // EXPERTS_PER_GROUP == best, sel, NEG)
    m1 = jnp.max(cand, axis=0, keepdims=True)
    i1 = jnp.min(jnp.where(cand == m1, row, N_EXPERTS), axis=0, keepdims=True)
    oh1 = row == i1
    cand = jnp.where(oh1, NEG, cand)
    m2 = jnp.max(cand, axis=0, keepdims=True)
    i2 = jnp.min(jnp.where(cand == m2, row, N_EXPERTS), axis=0, keepdims=True)
    oh2 = row == i2
    s1 = jnp.sum(jnp.where(oh1, scores, 0.0), axis=0, keepdims=True)
    s2 = jnp.sum(jnp.where(oh2, scores, 0.0), axis=0, keepdims=True)
    return oh1, oh2, s1 / (s1 + s2), s2 / (s1 + s2)


def _out_kernel(yp_ref, yh_ref, ya_ref, x_ref, g1_ref, sc_ref, sh_ref, nw_ref, wo_ref, rwt_ref, rb_ref,
                before_ref, ones_ref, lower_ref, x1_ref, h2_ref, ls_ref, gate_ref, cnt_ref):
    tm = MOE_TOK_TILE
    subs = [pl.ds(u * tm, tm) for u in range(x_ref.shape[0] // tm)]
    p0, p1 = POOL_WIDTH, POOL_WIDTH + HGRN_WIDTH
    h2s = []
    for rows in subs:
        mix = (_dot(yp_ref[rows, :], wo_ref[0:p0, :]) + _dot(yh_ref[rows, :], wo_ref[p0:p1, :])
               + _dot(ya_ref[rows, :], wo_ref[p1:, :]))
        x1 = x_ref[rows, :] + g1_ref[0] * mix
        x1_ref[rows, :] = x1
        ms = jnp.mean(x1 * x1, axis=-1, keepdims=True)
        h2 = ((x1 * lax.rsqrt(ms + EPS) * nw_ref[...]) * (1.0 + sc_ref[0]) + sh_ref[0]).astype(BF16)
        h2_ref[rows, :] = h2
        h2s.append(h2)
    logits = [_dot_nt(rwt_ref[...], h2) for h2 in h2s]
    picks = []
    for lg in logits:
        ex = jnp.exp(lg - jnp.max(lg, axis=0, keepdims=True))
        scores = ex / jnp.sum(ex, axis=0, keepdims=True)
        picks.append(_route(scores + rb_ref[...], scores))
    chosen = [jnp.where(jnp.logical_or(oh1, oh2), 1.0, 0.0).astype(BF16) for oh1, oh2, _, _ in picks]
    ranks = [_dot(ch, before_ref[...]) for ch in chosen]
    counts = [_dot(ch, ones_ref[...]) for ch in chosen]
    aligned = [(jnp.floor((cn + (ROW_ALIGN - 1)) * (1.0 / ROW_ALIGN)) * ROW_ALIGN).astype(BF16) for cn in counts]
    slots = [_dot(lower_ref[...], al) + rk for al, rk in zip(aligned, ranks)]
    for u, rows in enumerate(subs):
        oh1, oh2, w1, w2 = picks[u]
        gate_ref[:, rows] = jnp.concatenate([w1, w2], axis=0)
        ls_ref[:, rows] = jnp.concatenate([jnp.sum(jnp.where(oh1, slots[u], 0.0), axis=0, keepdims=True),
                                           jnp.sum(jnp.where(oh2, slots[u], 0.0), axis=0, keepdims=True)], axis=0)
        cnt_ref[u] = counts[u][:, 0:128]


def _out_proj_router(yp, yh, ya, x, g1, sc, sh, nw, wo_b, layer, rwt_b, rb, consts):
    b, s, d = x.shape
    t = b * s
    tm = MOE_TOK_TILE
    n_sub = ROUTER_SUBTILES if s % (ROUTER_SUBTILES * tm) == 0 else 1
    ts = n_sub * tm
    per_batch = s // ts
    nt = t // tm
    before, ones, lower = consts
    tok = lambda w: pl.BlockSpec((ts, w), lambda i: (i, 0))
    vec = pl.BlockSpec((1, 1, d), lambda i: (i // per_batch, 0, 0))
    full = lambda a: pl.BlockSpec(a.shape, lambda i: (0,) * a.ndim)
    lanes = pl.BlockSpec((2, ts), lambda i: (0, i))
    return pl.pallas_call(
        _out_kernel,
        grid=(t // ts,),
        in_specs=[tok(POOL_WIDTH), tok(HGRN_WIDTH), tok(ATT_WIDTH), tok(d), vec, vec, vec,
                  pl.BlockSpec((1, d), lambda i: (0, 0)),
                  pl.BlockSpec((None,) + wo_b.shape[1:], lambda i: (layer, 0, 0)), full(rwt_b),
                  pl.BlockSpec((N_EXPERTS, 1), lambda i: (0, 0)), full(before), full(ones), full(lower)],
        out_specs=[tok(d), tok(d), lanes, lanes, pl.BlockSpec((n_sub, N_EXPERTS, 128), lambda i: (i, 0, 0))],
        out_shape=[jax.ShapeDtypeStruct((t, d), F32), jax.ShapeDtypeStruct((t, d), BF16),
                   jax.ShapeDtypeStruct((2, t), F32), jax.ShapeDtypeStruct((2, t), F32),
                   jax.ShapeDtypeStruct((nt, N_EXPERTS, 128), F32)],
        compiler_params=_cparams(("parallel",)),
        name="out_proj_router",
    )(yp.reshape(t, -1), yh.reshape(t, -1), ya.reshape(t, -1), x.reshape(t, d),
      g1.reshape(b, 1, d), sc.reshape(b, 1, d), sh.reshape(b, 1, d), nw.reshape(1, d), wo_b, rwt_b,
      rb.reshape(N_EXPERTS, 1), before, ones, lower)


def _moe_tables(cnt_out):
    cnt = jnp.round(cnt_out[:, :, 0]).astype(jnp.int32)
    cnt = (cnt + ROW_ALIGN - 1) // ROW_ALIGN * ROW_ALIGN
    total = jnp.sum(cnt, axis=0)
    padded = (total + EXPERT_TILE - 1) // EXPERT_TILE * EXPERT_TILE
    ends = jnp.cumsum(padded)
    first = ends - padded
    start = first[None, :] + jnp.cumsum(cnt, axis=0) - cnt
    loff = jnp.cumsum(cnt, axis=1) - cnt
    n_tiles = _moe_rows(cnt.shape[0] * MOE_TOK_TILE) // EXPERT_TILE
    n_used = ends[-1] // EXPERT_TILE
    tile_row = jnp.minimum(jnp.arange(n_tiles), n_used - 1) * EXPERT_TILE
    tile_expert = jnp.sum((ends[None, :] <= tile_row[:, None]).astype(jnp.int32), axis=1)
    pieces = jnp.sum(cnt, axis=1, keepdims=True) // ROW_ALIGN

    def copies(n_per_expert, offset_in_chunk, n_max):
        incl = jnp.cumsum(n_per_expert, axis=1)
        k = jnp.arange(n_max, dtype=jnp.int32)
        owner = jnp.sum((incl[:, None, :] <= k[None, :, None]).astype(jnp.int32), axis=2)
        owner = jnp.minimum(owner, N_EXPERTS - 1)
        is_owner = owner[:, :, None] == jnp.arange(N_EXPERTS, dtype=jnp.int32)[None, None, :]
        pick = lambda a: jnp.sum(jnp.where(is_owner, a[:, None, :], 0), axis=2)
        off = offset_in_chunk(k[None, :] - pick(incl - n_per_expert), pick(cnt))
        return pick(loff) + off, pick(start) + off, incl[:, -1:]

    big_src, big_dst, n_big = copies(cnt // (2 * ROW_ALIGN), lambda j, c: j * (2 * ROW_ALIGN), MAX_PIECES // 2)
    small_src, small_dst, n_small = copies(cnt % (2 * ROW_ALIGN) // ROW_ALIGN, lambda j, c: c - ROW_ALIGN, N_EXPERTS)
    moves = jnp.concatenate([big_src, big_dst, small_src, small_dst, n_big, n_small, pieces], axis=1)
    free_pieces = (n_tiles - n_used) * (EXPERT_TILE // ZERO_ROWS)
    tail = jnp.concatenate([first + total, padded - total, jnp.stack([ends[-1], free_pieces])])
    return (moves.reshape(-1).astype(jnp.int32), tail.astype(jnp.int32), tile_expert.astype(jnp.int32),
            n_used.reshape(1).astype(jnp.int32))


MOVE_BIG_SRC = 0
MOVE_BIG_DST = MOVE_BIG_SRC + MAX_PIECES // 2
MOVE_SMALL_SRC = MOVE_BIG_DST + MAX_PIECES // 2
MOVE_SMALL_DST = MOVE_SMALL_SRC + N_EXPERTS
MOVE_N_BIG = MOVE_SMALL_DST + N_EXPERTS
MOVE_N_SMALL = MOVE_N_BIG + 1
MOVE_PIECES = MOVE_N_SMALL + 1
MOVE_WIDTH = MOVE_PIECES + 1


def _tile_copies(moves_ref, tile, make_copy):
    base = tile * MOVE_WIDTH
    for src0, dst0, n_at, size in ((MOVE_BIG_SRC, MOVE_BIG_DST, MOVE_N_BIG, 2 * ROW_ALIGN),
                                   (MOVE_SMALL_SRC, MOVE_SMALL_DST, MOVE_N_SMALL, ROW_ALIGN)):
        def body(j, carry):
            make_copy(pl.multiple_of(moves_ref[base + src0 + j], ROW_ALIGN),
                      pl.multiple_of(moves_ref[base + dst0 + j], ROW_ALIGN), size).start()
            return carry

        lax.fori_loop(0, moves_ref[base + n_at], body, 0)


def _moe_rows(n_tokens):
    per_tile = 2 * MOE_TOK_TILE + N_EXPERTS * (ROW_ALIGN - 1)
    rows = (n_tokens // MOE_TOK_TILE) * per_tile + N_EXPERTS * (EXPERT_TILE - ROW_ALIGN)
    return (rows + EXPERT_TILE - 1) // EXPERT_TILE * EXPERT_TILE


def _chunk_copies(count, make_copy, wait=False):
    for size in CHUNK_SIZES:
        offset = jnp.bitwise_and(count, ~(2 * size - 1))

        @pl.when(jnp.bitwise_and(count, size) != 0)
        def _():
            copy = make_copy(pl.multiple_of(offset, ROW_ALIGN), size)
            copy.wait() if wait else copy.start()


def _wait_rows(n_pieces, make_copy):
    for p in WAIT_PIECES:
        @pl.when(jnp.bitwise_and(n_pieces, p) != 0)
        def _():
            make_copy(p * ROW_ALIGN).wait()


def _slot_rows(tm):
    return lax.broadcasted_iota(jnp.int32, (MOE_SLOTS, tm), 0).astype(F32)


def _sort_kernel(moves_ref, tail_ref, h2_ref, ls_ref, xs_ref, buf_ref, sem_ref):
    i = pl.program_id(0)
    n = pl.num_programs(0)
    tm = MOE_TOK_TILE
    n_sub = h2_ref.shape[0] // tm
    n_buf = buf_ref.shape[0]

    def sent(tile):
        s = tile % n_buf
        _wait_rows(moves_ref[tile * MOVE_WIDTH + MOVE_PIECES], lambda rows: pltpu.make_async_copy(
            buf_ref.at[s, pl.ds(0, rows), :], xs_ref.at[pl.ds(0, rows), :], sem_ref.at[s]))

    for u in range(n_sub):
        tile = i * n_sub + u
        slot = tile % n_buf

        @pl.when(tile >= n_buf)
        def _():
            sent(tile - n_buf)

        ls = ls_ref[:, u * tm:(u + 1) * tm]
        srow = _slot_rows(tm)
        perm = jnp.where(jnp.logical_or(srow == ls[0:1], srow == ls[1:2]), 1.0, 0.0).astype(BF16)
        buf_ref[slot] = _dot(perm, h2_ref[u * tm:(u + 1) * tm, :]).astype(BF16)
        _tile_copies(moves_ref, tile, lambda src, dst, size: pltpu.make_async_copy(
            buf_ref.at[slot, pl.ds(src, size), :], xs_ref.at[pl.ds(dst, size), :], sem_ref.at[slot]))

    @pl.when(i == n - 1)
    def _():
        last = n * n_sub - 1
        for back in reversed(range(n_buf)):
            @pl.when(last >= back)
            def _():
                sent(last - back)

        buf_ref[0] = jnp.zeros(buf_ref.shape[1:], BF16)
        zeros_to = lambda row, size: pltpu.make_async_copy(
            buf_ref.at[0, pl.ds(0, size), :], xs_ref.at[pl.ds(row, size), :], sem_ref.at[0])

        for wait in (False, True):
            def expert_tail(e, carry):
                st = pl.multiple_of(tail_ref[e], ROW_ALIGN)
                _chunk_copies(tail_ref[N_EXPERTS + e], lambda off, size: zeros_to(st + off, size), wait)
                return carry

            def free_tile(j, carry):
                copy = zeros_to(pl.multiple_of(tail_ref[2 * N_EXPERTS] + j * ZERO_ROWS, ROW_ALIGN), ZERO_ROWS)
                copy.wait() if wait else copy.start()
                return carry

            lax.fori_loop(0, N_EXPERTS, expert_tail, 0)
            lax.fori_loop(0, tail_ref[2 * N_EXPERTS + 1], free_tile, 0)


def _sort_tokens(h2, ls, moves, tail, n_rows):
    t, d = h2.shape
    tm = MOE_TOK_TILE * (2 if (t // MOE_TOK_TILE) % 2 == 0 else 1)
    grid_spec = pltpu.PrefetchScalarGridSpec(
        num_scalar_prefetch=2,
        grid=(t // tm,),
        in_specs=[pl.BlockSpec((tm, d), lambda i, *_: (i, 0)),
                  pl.BlockSpec((2, tm), lambda i, *_: (0, i))],
        out_specs=pl.BlockSpec(memory_space=pl.ANY),
        scratch_shapes=[pltpu.VMEM((SORT_BUFFERS, MOE_SLOTS, d), BF16), pltpu.SemaphoreType.DMA((SORT_BUFFERS,))],
    )
    return pl.pallas_call(
        _sort_kernel,
        grid_spec=grid_spec,
        out_shape=jax.ShapeDtypeStruct((n_rows, d), BF16),
        compiler_params=_cparams(("arbitrary",)),
        name="moe_sort",
    )(moves, tail, h2, ls)


def _gmm_kernel(te_ref, nu_ref, xs_ref, wg_ref, wu_ref, wd_ref, ys_ref, wgb, wub, wdb):
    i = pl.program_id(0)
    used = i < nu_ref[0]
    new_expert = jnp.logical_or(i == 0, te_ref[i] != te_ref[jnp.maximum(i - 1, 0)])

    @pl.when(jnp.logical_and(used, new_expert))
    def _():
        wgb[...] = wg_ref[...].astype(BF16)
        wub[...] = wu_ref[...].astype(BF16)
        wdb[...] = wd_ref[...].astype(BF16)

    @pl.when(used)
    def _():
        x = xs_ref[...]
        he = _silu(_dot(x, wgb[...])) * _dot(x, wub[...])
        ys_ref[...] = _dot(he.astype(BF16), wdb[...]).astype(BF16)

    @pl.when(jnp.logical_not(used))
    def _():
        ys_ref[...] = jnp.zeros(ys_ref.shape, BF16)


def _grouped_mlp(xs, tile_expert, n_used, wg, wu, wd, layer):
    n_rows, d = xs.shape
    te = EXPERT_TILE
    row_map = lambda i, tex, nu: (jnp.minimum(i, nu[0] - 1), 0)
    grid_spec = pltpu.PrefetchScalarGridSpec(
        num_scalar_prefetch=2,
        grid=(n_rows // te,),
        in_specs=[pl.BlockSpec((te, d), row_map),
                  pl.BlockSpec((None, None, d, D_EXPERT), lambda i, tex, nu: (layer, tex[i], 0, 0)),
                  pl.BlockSpec((None, None, d, D_EXPERT), lambda i, tex, nu: (layer, tex[i], 0, 0)),
                  pl.BlockSpec((None, None, D_EXPERT, d), lambda i, tex, nu: (layer, tex[i], 0, 0))],
        out_specs=pl.BlockSpec((te, d), lambda i, tex, nu: (i, 0)),
        scratch_shapes=[pltpu.VMEM((d, D_EXPERT), BF16), pltpu.VMEM((d, D_EXPERT), BF16),
                        pltpu.VMEM((D_EXPERT, d), BF16)],
    )
    return pl.pallas_call(
        _gmm_kernel,
        grid_spec=grid_spec,
        out_shape=jax.ShapeDtypeStruct((n_rows, d), BF16),
        compiler_params=_cparams(("arbitrary",)),
        name="moe_grouped_mlp",
    )(tile_expert, n_used, xs, wg, wu, wd)


def _combine_kernel(moves_ref, ls_ref, gate_ref, x1_ref, g2_ref, ys_ref, o_ref, buf_ref, sem_ref):
    i = pl.program_id(0)
    n = pl.num_programs(0)
    tm = MOE_TOK_TILE
    n_sub = x1_ref.shape[0] // tm
    n_tiles = n * n_sub
    n_buf = buf_ref.shape[0]

    def fetch(tile):
        s = tile % n_buf
        _tile_copies(moves_ref, tile, lambda dst, src, size: pltpu.make_async_copy(
            ys_ref.at[pl.ds(src, size), :], buf_ref.at[s, pl.ds(dst, size), :], sem_ref.at[s]))

    @pl.when(i == 0)
    def _():
        buf_ref[...] = jnp.zeros(buf_ref.shape, BF16)
        for first in range(COMBINE_AHEAD):
            @pl.when(first < n_tiles)
            def _():
                fetch(first)

    for u in range(n_sub):
        tile = i * n_sub + u
        slot = tile % n_buf

        @pl.when(tile + COMBINE_AHEAD < n_tiles)
        def _():
            fetch(tile + COMBINE_AHEAD)

        n_pieces = moves_ref[tile * MOVE_WIDTH + MOVE_PIECES]
        _wait_rows(n_pieces, lambda rows: pltpu.make_async_copy(
            ys_ref.at[pl.ds(0, rows), :], buf_ref.at[slot, pl.ds(0, rows), :], sem_ref.at[slot]))

        rows_u = pl.ds(u * tm, tm)
        srow = _slot_rows(tm)
        ls = ls_ref[:, u * tm:(u + 1) * tm]
        gate = gate_ref[:, u * tm:(u + 1) * tm]
        weights = (jnp.where(srow == ls[0:1], gate[0:1], 0.0)
                   + jnp.where(srow == ls[1:2], gate[1:2], 0.0)).astype(BF16)
        y = _dot_tn(weights, buf_ref[slot])
        o_ref[rows_u, :] = x1_ref[rows_u, :] + g2_ref[0] * y


def _combine(ys, ls, gate, x1, g2, moves, seq_len):
    t, d = x1.shape
    tm = MOE_TOK_TILE * (2 if (seq_len // MOE_TOK_TILE) % 2 == 0 else 1)
    per_batch = seq_len // tm
    grid_spec = pltpu.PrefetchScalarGridSpec(
        num_scalar_prefetch=1,
        grid=(t // tm,),
        in_specs=[pl.BlockSpec((2, tm), lambda i, *_: (0, i)),
                  pl.BlockSpec((2, tm), lambda i, *_: (0, i)),
                  pl.BlockSpec((tm, d), lambda i, *_: (i, 0)),
                  pl.BlockSpec((1, 1, d), lambda i, *_: (i // per_batch, 0, 0)),
                  pl.BlockSpec(memory_space=pl.ANY)],
        out_specs=pl.BlockSpec((tm, d), lambda i, *_: (i, 0)),
        scratch_shapes=[pltpu.VMEM((COMBINE_AHEAD + 1, MOE_SLOTS, d), BF16),
                        pltpu.SemaphoreType.DMA((COMBINE_AHEAD + 1,))],
    )
    return pl.pallas_call(
        _combine_kernel,
        grid_spec=grid_spec,
        out_shape=jax.ShapeDtypeStruct((t, d), F32),
        compiler_params=_cparams(("arbitrary",)),
        name="moe_combine",
    )(moves, ls, gate, x1, g2.reshape(-1, 1, d), ys)


def kernel(x, c, ada_w, ada_b, norm1_w, norm2_w, w_in, pool_w, pool_scale, hgrn_lb_raw, hgrn_norm_w, q_norm_w,
           k_norm_w, attn_sinks, w_out, router_w, router_bias, expert_w_gate, expert_w_up, expert_w_down):
    b, s, d = x.shape
    depth = ada_w.shape[0]
    t = b * s
    n_rows = _moe_rows(t)

    p = jax.nn.softmax(hgrn_lb_raw.astype(F32), axis=0)
    lower_bounds = jnp.maximum(jnp.cumsum(p, axis=0) - p[0:1], 0.0)

    mod = _modulation(c, ada_w, ada_b)
    hgrn_consts = _hgrn_constants() + (_head_ones(HGRN_WIDTH, HGRN_KDIM),)
    attn_consts = (_attn_bias(), _head_ones(ATT_WIDTH, HEAD_DIM), _head_ones(KV_WIDTH, HEAD_DIM))
    router_consts = _router_constants(MOE_TOK_TILE)
    rwt_b = router_w.T.astype(BF16)
    w_in_b = w_in.astype(BF16)
    w_out_b = w_out.astype(BF16)

    for l in range(depth):
        sh1, sc1, g1, sh2, sc2, g2 = [mod[l, :, j * d:(j + 1) * d] for j in range(6)]
        yp, zh, zq, zkv = _in_proj(x, sc1, sh1, norm1_w[l], w_in_b, l, _block_diag(pool_w[l]).astype(BF16),
                                   pool_scale[l])
        yh = _hgrn_mixer(zh, lower_bounds[l], hgrn_norm_w[l], hgrn_consts)
        ya = _attn_mixer(zq, zkv, q_norm_w[l], k_norm_w[l], attn_sinks[l], attn_consts)
        x1, h2, ls, gate, cnt_out = _out_proj_router(
            yp, yh, ya, x, g1, sc2, sh2, norm2_w[l], w_out_b, l, rwt_b, router_bias, router_consts)
        moves, tail, tile_expert, n_used = _moe_tables(cnt_out)
        xs = _sort_tokens(h2, ls, moves, tail, n_rows)
        ys = _grouped_mlp(xs, tile_expert, n_used, expert_w_gate, expert_w_up, expert_w_down, l)
        x = _combine(ys, ls, gate, x1, g2, moves, s).reshape(b, s, d)
    return x
```

```python
import functools

import numpy as np
import jax
import jax.numpy as jnp
from jax import lax
from jax.experimental import pallas as pl
from jax.experimental.pallas import tpu as pltpu

F32 = jnp.float32
BF16 = jnp.bfloat16

D_MODEL = 1024
POOL_WINDOWS = (2, 4, 8, 16)
POOL_WIDTH = 256
POOL_GROUP = 64
POOL_HALO = 32
HGRN_HEADS = 4
HGRN_KDIM = 64
HGRN_WIDTH = 256
HGRN_CHUNK = 64
ATT_HEADS = 8
ATT_KV_HEADS = 2
HEAD_DIM = 64
ATT_WIDTH = 512
KV_WIDTH = 128
WINDOW = 128
N_EXPERTS = 16
N_GROUPS = 4
EXPERTS_PER_GROUP = 4
D_EXPERT = 512
EPS = 1e-6
MAX_ONE_MINUS_F = 1.0 - 1e-6
LOG2E = 1.4426950408889634
NEG = -1e30

VMEM_LIMIT = 48 * 1024 * 1024

IN_TILE = 1024
ATT_TILE = 1024
MOE_TOK_TILE = 256
ROW_ALIGN = 16
MOE_SLOTS = 768
EXPERT_TILE = 1024
ZERO_ROWS = 512
CHUNK_SIZES = (512, 256, 128, 64, 32, 16)
SORT_TILES_PER_STEP = 4
COMBINE_TILES_PER_STEP = 2
COMBINE_AHEAD = 3
SORT_BUFFERS = 4
MAX_PIECES = MOE_SLOTS // ROW_ALIGN
WAIT_PIECES = (32, 16, 8, 4, 2, 1)


def _sigmoid(x):
    return 1.0 / (1.0 + jnp.exp(-x))


def _silu(x):
    return x * _sigmoid(x)


def _cparams(sem, **kw):
    return pltpu.CompilerParams(dimension_semantics=sem, vmem_limit_bytes=VMEM_LIMIT, **kw)


def _dot(a, b):
    return jnp.dot(a, b, preferred_element_type=F32)


def _dot_nt(a, b):
    return lax.dot_general(a, b, (((1,), (1,)), ((), ())), preferred_element_type=F32)


def _dot_tn(a, b):
    return lax.dot_general(a, b, (((0,), (0,)), ((), ())), preferred_element_type=F32)


def _head_sumsq(x, bd):
    return _dot((x * x).astype(BF16), bd)


def _mod_kernel(c_ref, w_ref, b_ref, o_ref):
    cond = _silu(c_ref[...])
    o_ref[0] = _dot(cond.astype(BF16), w_ref[0].astype(BF16)) + b_ref[0]


def _modulation(c, ada_w, ada_b):
    depth, d, n = ada_w.shape
    b = c.shape[0]
    nb = n // d
    return pl.pallas_call(
        _mod_kernel,
        grid=(depth, nb),
        in_specs=[pl.BlockSpec((b, d), lambda l, j: (0, 0)),
                  pl.BlockSpec((1, d, d), lambda l, j: (l, 0, j)),
                  pl.BlockSpec((1, 1, d), lambda l, j: (l, 0, j))],
        out_specs=pl.BlockSpec((1, b, d), lambda l, j: (l, 0, j)),
        out_shape=jax.ShapeDtypeStruct((depth, b, n), F32),
        compiler_params=_cparams(("parallel", "parallel")),
        name="adaln_mod",
    )(c, ada_w, ada_b.reshape(depth, 1, n))


def _in_kernel(x_ref, sc_ref, sh_ref, nw_ref, w_ref, pw_ref, ps_ref, yp_ref, zh_ref, zq_ref, zkv_ref,
               buf_ref, sa_ref, sb_ref):
    x = x_ref[0]
    ms = jnp.mean(x * x, axis=-1, keepdims=True)
    h = (x * lax.rsqrt(ms + EPS) * nw_ref[...]) * (1.0 + sc_ref[0]) + sh_ref[0]
    z = _dot(h.astype(BF16), w_ref[...])
    zh_ref[0] = z[:, POOL_WIDTH:POOL_WIDTH + 4 * HGRN_WIDTH]
    zq_ref[0] = z[:, POOL_WIDTH + 4 * HGRN_WIDTH:POOL_WIDTH + 4 * HGRN_WIDTH + ATT_WIDTH]
    zkv_ref[0] = z[:, POOL_WIDTH + 4 * HGRN_WIDTH + ATT_WIDTH:]
    yp_ref[0] = _pool(z[:, :POOL_WIDTH], pl.program_id(1), pw_ref, ps_ref, buf_ref, sa_ref, sb_ref).astype(yp_ref.dtype)


def _in_proj(x, sc, sh, nw, w_in_b, layer, pool_bd, pool_scale):
    b, s, d = x.shape
    n = w_in_b.shape[2]
    ts = min(IN_TILE, s)
    tok = lambda w: pl.BlockSpec((1, ts, w), lambda bi, i: (bi, i, 0))
    vec = pl.BlockSpec((1, 1, d), lambda bi, i: (bi, 0, 0))
    full = lambda a: pl.BlockSpec(a.shape, lambda bi, i: (0,) * a.ndim)
    pool_scale = pool_scale.reshape(1, POOL_WIDTH)
    outs = ((POOL_WIDTH, BF16), (4 * HGRN_WIDTH, F32), (ATT_WIDTH, F32), (2 * KV_WIDTH, F32))
    return pl.pallas_call(
        _in_kernel,
        grid=(b, s // ts),
        in_specs=[tok(d), vec, vec,
                  pl.BlockSpec((1, d), lambda bi, i: (0, 0)),
                  pl.BlockSpec((None, d, n), lambda bi, i: (layer, 0, 0)), full(pool_bd), full(pool_scale)],
        out_specs=[tok(w) for w, _ in outs],
        out_shape=[jax.ShapeDtypeStruct((b, s, w), dt) for w, dt in outs],
        scratch_shapes=[pltpu.VMEM((POOL_HALO + ts, POOL_WIDTH), F32)] * 3,
        compiler_params=_cparams(("parallel", "arbitrary")),
        name="in_proj_pool",
    )(x, sc.reshape(b, 1, d), sh.reshape(b, 1, d), nw.reshape(1, d), w_in_b, pool_bd, pool_scale)


def _pool(a, i, w_ref, scale_ref, buf_ref, sa_ref, sb_ref):
    ts = a.shape[0]
    halo = POOL_HALO
    end = halo + ts
    half = 2 * POOL_GROUP

    @pl.when(i == 0)
    def _():
        buf_ref[0:halo, :] = jnp.zeros((halo, POOL_WIDTH), F32)

    buf_ref[halo:end, :] = a
    s2 = buf_ref[8:end, :] + buf_ref[7:end - 1, :]
    sa_ref[8:end, :] = s2
    s4 = sa_ref[16:end, :] + sa_ref[14:end - 2, :]
    sb_ref[16:end, :] = s4
    s8 = sb_ref[24:end, half:] + sb_ref[20:end - 4, half:]
    sa_ref[24:end, half:] = s8
    s16 = sa_ref[halo:end, half:] + sa_ref[halo - 8:end - 8, half:]
    lane = lax.broadcasted_iota(jnp.int32, (1, POOL_WIDTH), 1)
    win = jnp.left_shift(2, lane // POOL_GROUP)
    low = lane[:, 0:half] % half < POOL_GROUP
    acc = jnp.concatenate([jnp.where(low, s2[halo - 8:, 0:half], s4[halo - 16:, 0:half]),
                           jnp.where(low, s8[halo - 24:], s16)], axis=1)
    pos = i * ts + lax.broadcasted_iota(jnp.int32, (ts, 1), 0)
    count = jnp.minimum(pos + 1, win).astype(F32)
    pooled = acc / count - a
    buf_ref[0:halo, :] = a[ts - halo:, :]
    return _dot(pooled.astype(BF16), w_ref[...]) * scale_ref[...]


def _block_diag(blocks):
    g, n, _ = blocks.shape
    eye = jnp.eye(g, dtype=blocks.dtype)
    return (eye[:, None, :, None] * blocks[:, :, None, :]).reshape(g * n, g * n)


def _head_ones(width, head):
    idx = np.arange(width) // head
    return jnp.asarray((idx[:, None] == idx[None, :]).astype(np.float32), dtype=BF16)


HGRN_LEVELS = (32, 16, 8, 4, 2, 1)
HGRN_MATMUL_LEVELS = (2, 1)
N_LEVEL_MASKS = len(HGRN_LEVELS) + 1
HGRN_SEQS = 8
HGRN_TILE = 256
ATT_GROUP = 8
ROUTER_SUBTILES = 4


def _hgrn_constants():
    c = HGRN_CHUNK
    m = np.zeros((len(HGRN_MATMUL_LEVELS) + 1, c, c), np.float32)
    masks = np.zeros((N_LEVEL_MASKS, c, c), np.float32)
    for lvl, n in enumerate(HGRN_LEVELS):
        for t in range(c):
            blk = t // (2 * n)
            mid = blk * 2 * n + n
            if t >= mid:
                masks[lvl, t, blk * 2 * n:mid] = 1.0
            if n in HGRN_MATMUL_LEVELS:
                row = m[HGRN_MATMUL_LEVELS.index(n), t]
                if t >= mid:
                    row[mid:t + 1] = 1.0
                else:
                    row[t + 1:mid] = 1.0
    for t in range(c):
        m[-1, t, :t + 1] = 1.0
        masks[-1, t, t] = 1.0
    m = m.reshape(-1, c)
    m3 = np.concatenate([m, m, m], axis=1)
    masks = np.tile(masks, (1, HGRN_HEADS, 1))
    return jnp.asarray(m3, dtype=BF16), jnp.asarray(masks, dtype=F32)


def _hgrn_chunks(units, bd, m3, lmask_ref):
    c = HGRN_CHUNK
    w = HGRN_WIDTH
    n_fine = len(HGRN_MATMUL_LEVELS)
    lane_head = lax.broadcasted_iota(jnp.int32, (1, w), 1) // HGRN_KDIM
    sums = [_dot(m3, u[2]) for u in units]
    bs = [s[n_fine * c:] for s in sums]

    def level_decay(u, n):
        if n in HGRN_MATMUL_LEVELS:
            k = HGRN_MATMUL_LEVELS.index(n)
            return jnp.exp(sums[u][k * c:(k + 1) * c])
        blocks = bs[u].reshape(c // (2 * n), 2 * n, w)
        ref = blocks[:, n - 1:n, :]
        right = lax.broadcasted_iota(jnp.int32, (1, 2 * n, 1), 1) >= n
        return jnp.exp(jnp.where(right, blocks - ref, ref - blocks).reshape(c, w))

    groups = c // 8
    parts = [[[None] * groups for _ in range(HGRN_HEADS)] for _ in units]

    def add_part(u, h, g, piece):
        parts[u][h][g] = piece if parts[u][h][g] is None else parts[u][h][g] + piece

    for lvl in range(N_LEVEL_MASKS):
        n = HGRN_LEVELS[lvl] if lvl < len(HGRN_LEVELS) else 0
        wanted = [g for g in range(groups) if (8 * g) % (2 * n) >= n] if n >= 8 else list(range(groups))
        for u, (qf, kk, _, _, _, _) in enumerate(units):
            if lvl < len(HGRN_LEVELS):
                e = level_decay(u, n)
                ql = qf * e
                kl = (kk * e).astype(BF16)
            else:
                ql = qf
                kl = kk.astype(BF16)
            ql = jnp.concatenate([ql[8 * g:8 * g + 8] for g in wanted], axis=0).astype(BF16)
            zero = jnp.zeros_like(ql)
            qs = jnp.concatenate([jnp.where(lane_head == h, ql, zero) for h in range(HGRN_HEADS)], axis=0)
            res = _dot_nt(qs, kl)
            for h in range(HGRN_HEADS):
                for k, g in enumerate(wanted):
                    r0 = (h * len(wanted) + k) * 8
                    add_part(u, h, g, res[r0:r0 + 8] * lmask_ref[lvl, h * c + 8 * g:h * c + 8 * g + 8, :])
    scores = [jnp.concatenate([parts[u][h][g] for h in range(HGRN_HEADS) for g in range(groups)], axis=0)
              for u in range(len(units))]
    rs = [_dot(scores[u].astype(BF16), unit[3]) for u, unit in enumerate(units)]
    inters = [_dot_nt((unit[0] * jnp.exp(bs[u])).astype(BF16), unit[5].astype(BF16)) for u, unit in enumerate(units)]
    upds = [_dot_tn(unit[3], (unit[1] * jnp.exp(bs[u][c - 1:c, :] - bs[u])).astype(BF16))
            for u, unit in enumerate(units)]
    outs = []
    for u in range(len(units)):
        o = inters[u]
        for h in range(HGRN_HEADS):
            o = o + jnp.where(lane_head == h, rs[u][h * c:(h + 1) * c], 0.0)
        outs.append(o)
    sss = [_head_sumsq(o, bd) for o in outs]
    res = []
    for u, unit in enumerate(units):
        st = unit[5] * jnp.exp(bs[u][c - 1:c, :]) + jnp.where(bd > 0, upds[u], 0.0)
        y = outs[u] * lax.rsqrt(sss[u] * (1.0 / HGRN_KDIM) + EPS) * unit[4]
        res.append((y, st))
    return res


def _hgrn_kernel(zh_ref, lb_ref, gw_ref, m3_ref, lmask_ref, bd_ref, o_ref, st_ref):
    i = pl.program_id(1)
    nb, ts = zh_ref.shape[0], zh_ref.shape[1]
    c = HGRN_CHUNK
    w = HGRN_WIDTH

    @pl.when(i == 0)
    def _():
        st_ref[...] = jnp.zeros(st_ref.shape, F32)

    one_minus_lb = 1.0 - lb_ref[...]
    gw = gw_ref[...]
    bd = bd_ref[...]
    m3 = m3_ref[...]

    def chunk(ci, carry):
        rows = pl.ds(pl.multiple_of(ci * c, c), c)
        units = []
        for s in range(nb):
            q, f, v, g = [zh_ref[s, rows, j * w:(j + 1) * w] for j in range(4)]
            kk = one_minus_lb * _sigmoid(-f)
            lf = jnp.log(1.0 - jnp.minimum(kk, MAX_ONE_MINUS_F))
            hi = lf.astype(BF16)
            r1 = lf - hi.astype(F32)
            mid = r1.astype(BF16)
            lo = (r1 - mid.astype(F32)).astype(BF16)
            units.append((_silu(q), kk, jnp.concatenate([hi, mid, lo], axis=0), v.astype(BF16), gw * _silu(g),
                          st_ref[s]))
        for s, (y, st) in enumerate(_hgrn_chunks(units, bd, m3, lmask_ref)):
            st_ref[s] = st
            o_ref[s, rows, :] = y.astype(o_ref.dtype)
        return carry

    lax.fori_loop(0, ts // c, chunk, 0)


def _hgrn_mixer(zh, lb, norm_w, consts):
    b, s, _ = zh.shape
    w = HGRN_WIDTH
    ts = min(HGRN_TILE, s)
    nb = HGRN_SEQS if b % HGRN_SEQS == 0 else 1
    m3, lmask, bd = consts
    full = lambda a: pl.BlockSpec(a.shape, lambda bi, i: (0,) * a.ndim)
    return pl.pallas_call(
        _hgrn_kernel,
        grid=(b // nb, s // ts),
        in_specs=[pl.BlockSpec((nb, ts, 4 * w), lambda bi, i: (bi, i, 0)),
                  pl.BlockSpec((1, w), lambda bi, i: (0, 0)),
                  pl.BlockSpec((1, w), lambda bi, i: (0, 0)),
                  full(m3), full(lmask), full(bd)],
        out_specs=pl.BlockSpec((nb, ts, w), lambda bi, i: (bi, i, 0)),
        out_shape=jax.ShapeDtypeStruct((b, s, w), BF16),
        scratch_shapes=[pltpu.VMEM((nb, w, w), F32)],
        compiler_params=_cparams(("parallel", "arbitrary")),
        name="hgrn_mixer",
    )(zh, lb.reshape(1, w), norm_w.reshape(1, w), m3, lmask, bd)


def _attn_bias():
    qi = np.arange(WINDOW)[:, None]
    kj = np.arange(2 * WINDOW)[None, :]
    dist = qi + WINDOW - kj
    valid = (dist >= 0) & (dist < WINDOW)
    slopes = np.exp2(-8.0 * np.arange(1, ATT_HEADS + 1) / ATT_HEADS)
    bias = np.where(valid[None], -slopes[:, None, None] * dist[None] * LOG2E, NEG)
    first = np.where(kj[None] < WINDOW, NEG, bias)
    return jnp.asarray(np.concatenate([bias, first]), dtype=F32)


def _attn_kernel(sink_ref, zq_ref, zkv_ref, qw_ref, kw_ref, bias_ref, bdq_ref, bdk_ref, o_ref,
                 qbuf, kbuf, vbuf):
    i = pl.program_id(1)
    ts = zq_ref.shape[1]
    hw = 2 * HEAD_DIM

    @pl.when(i == 0)
    def _():
        kbuf[:, 0:WINDOW, :] = jnp.zeros((4, WINDOW, hw), BF16)
        vbuf[:, 0:WINDOW, :] = jnp.zeros((4, WINDOW, hw), BF16)

    q = zq_ref[0]
    ssq = _head_sumsq(q, bdq_ref[...])
    qbuf[...] = (q * lax.rsqrt(ssq * (1.0 / HEAD_DIM) + EPS) * (qw_ref[...] * (HEAD_DIM ** -0.5 * LOG2E))).astype(BF16)
    kv = zkv_ref[0]
    k = kv[:, :KV_WIDTH]
    v = kv[:, KV_WIDTH:]
    ssk = _head_sumsq(k, bdk_ref[...])
    kn = k * lax.rsqrt(ssk * (1.0 / HEAD_DIM) + EPS) * kw_ref[...]
    kr = pltpu.roll(kn, HEAD_DIM, 1)
    vr = pltpu.roll(v, HEAD_DIM, 1)
    low = lax.broadcasted_iota(jnp.int32, (1, hw), 1) < HEAD_DIM
    for j in range(ATT_KV_HEADS):
        for half in range(2):
            keep = low if half == 0 else jnp.logical_not(low)
            ksrc = kn if j == half else kr
            vsrc = v if j == half else vr
            kbuf[2 * j + half, WINDOW:WINDOW + ts, :] = jnp.where(keep, ksrc, 0.0).astype(BF16)
            vbuf[2 * j + half, WINDOW:WINDOW + ts, :] = jnp.where(keep, vsrc, 0.0).astype(BF16)

    def block(n, carry):
        r0 = pl.multiple_of(n * WINDOW, WINDOW)
        table = jnp.where(jnp.logical_and(i == 0, n == 0), ATT_HEADS, 0)
        for g0 in range(0, ATT_HEADS, ATT_GROUP):
            heads = range(g0, g0 + ATT_GROUP)
            logits = []
            for h in heads:
                hp, half = h // 2, h % 2
                j = h // (ATT_HEADS // ATT_KV_HEADS)
                qp = qbuf[pl.ds(r0, WINDOW), hp * hw:(hp + 1) * hw]
                keys = kbuf[2 * j + half, pl.ds(r0, 2 * WINDOW), :]
                logits.append(_dot_nt(qp, keys) + bias_ref[table + h])
            ps, scales = [], []
            for h, lg in zip(heads, logits):
                sink = sink_ref[h] * LOG2E
                m = jnp.maximum(jnp.max(lg, axis=-1, keepdims=True), sink)
                p = jnp.exp2(lg - m)
                scales.append(1.0 / (jnp.sum(p, axis=-1, keepdims=True) + jnp.exp2(sink - m)))
                ps.append(p.astype(BF16))
            outs = []
            for h, p in zip(heads, ps):
                half = h % 2
                j = h // (ATT_HEADS // ATT_KV_HEADS)
                vals = vbuf[2 * j + half, pl.ds(r0, 2 * WINDOW), :]
                outs.append(_dot(p, vals))
            for k in range(0, ATT_GROUP, 2):
                hp = (g0 + k) // 2
                acc = outs[k] * scales[k] + outs[k + 1] * scales[k + 1]
                o_ref[0, pl.ds(r0, WINDOW), hp * hw:(hp + 1) * hw] = acc.astype(o_ref.dtype)
        return carry

    lax.fori_loop(0, ts // WINDOW, block, 0)
    kbuf[:, 0:WINDOW, :] = kbuf[:, ts:ts + WINDOW, :]
    vbuf[:, 0:WINDOW, :] = vbuf[:, ts:ts + WINDOW, :]


def _attn_mixer(zq, zkv, q_norm_w, k_norm_w, sinks, consts):
    b, s, _ = zq.shape
    ts = min(ATT_TILE, s)
    bias, bdq, bdk = consts
    hw = 2 * HEAD_DIM
    qw = jnp.tile(q_norm_w, ATT_HEADS).reshape(1, ATT_WIDTH)
    kw = jnp.tile(k_norm_w, ATT_KV_HEADS).reshape(1, KV_WIDTH)
    full = lambda a: pl.BlockSpec(a.shape, lambda bi, i, sk: (0,) * a.ndim)
    grid_spec = pltpu.PrefetchScalarGridSpec(
        num_scalar_prefetch=1,
        grid=(b, s // ts),
        in_specs=[pl.BlockSpec((1, ts, ATT_WIDTH), lambda bi, i, sk: (bi, i, 0)),
                  pl.BlockSpec((1, ts, 2 * KV_WIDTH), lambda bi, i, sk: (bi, i, 0)),
                  full(qw), full(kw), full(bias), full(bdq), full(bdk)],
        out_specs=pl.BlockSpec((1, ts, ATT_WIDTH), lambda bi, i, sk: (bi, i, 0)),
        scratch_shapes=[pltpu.VMEM((ts, ATT_WIDTH), BF16),
                        pltpu.VMEM((4, WINDOW + ts, hw), BF16),
                        pltpu.VMEM((4, WINDOW + ts, hw), BF16)],
    )
    return pl.pallas_call(
        _attn_kernel,
        grid_spec=grid_spec,
        out_shape=jax.ShapeDtypeStruct((b, s, ATT_WIDTH), BF16),
        compiler_params=_cparams(("parallel", "arbitrary")),
        name="swa_mixer",
    )(sinks, zq, zkv, qw, kw, bias, bdq, bdk)


def _router_constants(tm):
    t = np.arange(tm)
    before = (t[:, None] < t[None, :]).astype(np.float32)
    e = np.arange(N_EXPERTS)
    lower = (e[None, :] < e[:, None]).astype(np.float32)
    return (jnp.asarray(before, dtype=BF16), jnp.asarray(np.ones((tm, tm), np.float32), dtype=BF16),
            jnp.asarray(lower, dtype=BF16))


def _route(sel, scores):
    tm = sel.shape[1]
    group_scores = []
    for g in range(N_GROUPS):
        rows = [sel[g * EXPERTS_PER_GROUP + a:g * EXPERTS_PER_GROUP + a + 1] for a in range(EXPERTS_PER_GROUP)]
        best_pair = None
        for a in range(EXPERTS_PER_GROUP):
            for bb in range(a + 1, EXPERTS_PER_GROUP):
                pair = rows[a] + rows[bb]
                best_pair = pair if best_pair is None else jnp.maximum(best_pair, pair)
        group_scores.append(best_pair)
    top = functools.reduce(jnp.maximum, group_scores)
    best = jnp.full((1, tm), N_GROUPS - 1, jnp.int32)
    for g in reversed(range(N_GROUPS - 1)):
        best = jnp.where(group_scores[g] == top, g, best)
    row = lax.broadcasted_iota(jnp.int32, (N_EXPERTS, tm), 0)
    cand = jnp.where(row // EXPERTS_PER_GROUP == best, sel, NEG)
    m1 = jnp.max(cand, axis=0, keepdims=True)
    i1 = jnp.min(jnp.where(cand == m1, row, N_EXPERTS), axis=0, keepdims=True)
    oh1 = row == i1
    cand = jnp.where(oh1, NEG, cand)
    m2 = jnp.max(cand, axis=0, keepdims=True)
    i2 = jnp.min(jnp.where(cand == m2, row, N_EXPERTS), axis=0, keepdims=True)
    oh2 = row == i2
    s1 = jnp.sum(jnp.where(oh1, scores, 0.0), axis=0, keepdims=True)
    s2 = jnp.sum(jnp.where(oh2, scores, 0.0), axis=0, keepdims=True)
    return oh1, oh2, s1 / (s1 + s2), s2 / (s1 + s2)


def _out_kernel(yp_ref, yh_ref, ya_ref, x_ref, g1_ref, sc_ref, sh_ref, nw_ref, wo_ref, rwt_ref, rb_ref,
                before_ref, ones_ref, lower_ref, x1_ref, h2_ref, ls_ref, gate_ref, cnt_ref):
    tm = MOE_TOK_TILE
    subs = [pl.ds(u * tm, tm) for u in range(x_ref.shape[0] // tm)]
    p0, p1 = POOL_WIDTH, POOL_WIDTH + HGRN_WIDTH
    h2s = []
    for rows in subs:
        mix = (_dot(yp_ref[rows, :], wo_ref[0:p0, :]) + _dot(yh_ref[rows, :], wo_ref[p0:p1, :])
               + _dot(ya_ref[rows, :], wo_ref[p1:, :]))
        x1 = x_ref[rows, :] + g1_ref[0] * mix
        x1_ref[rows, :] = x1
        ms = jnp.mean(x1 * x1, axis=-1, keepdims=True)
        h2 = ((x1 * lax.rsqrt(ms + EPS) * nw_ref[...]) * (1.0 + sc_ref[0]) + sh_ref[0]).astype(BF16)
        h2_ref[rows, :] = h2
        h2s.append(h2)
    logits = [_dot_nt(rwt_ref[...], h2) for h2 in h2s]
    picks = []
    for lg in logits:
        ex = jnp.exp(lg - jnp.max(lg, axis=0, keepdims=True))
        scores = ex / jnp.sum(ex, axis=0, keepdims=True)
        picks.append(_route(scores + rb_ref[...], scores))
    chosen = [jnp.where(jnp.logical_or(oh1, oh2), 1.0, 0.0).astype(BF16) for oh1, oh2, _, _ in picks]
    ranks = [_dot(ch, before_ref[...]) for ch in chosen]
    counts = [_dot(ch, ones_ref[...]) for ch in chosen]
    aligned = [(jnp.floor((cn + (ROW_ALIGN - 1)) * (1.0 / ROW_ALIGN)) * ROW_ALIGN).astype(BF16) for cn in counts]
    slots = [_dot(lower_ref[...], al) + rk for al, rk in zip(aligned, ranks)]
    for u, rows in enumerate(subs):
        oh1, oh2, w1, w2 = picks[u]
        gate_ref[:, rows] = jnp.concatenate([w1, w2], axis=0)
        ls_ref[:, rows] = jnp.concatenate([jnp.sum(jnp.where(oh1, slots[u], 0.0), axis=0, keepdims=True),
                                           jnp.sum(jnp.where(oh2, slots[u], 0.0), axis=0, keepdims=True)], axis=0)
        cnt_ref[u] = counts[u][:, 0:128]


def _out_proj_router(yp, yh, ya, x, g1, sc, sh, nw, wo_b, layer, rwt_b, rb, consts):
    b, s, d = x.shape
    t = b * s
    tm = MOE_TOK_TILE
    n_sub = ROUTER_SUBTILES if s % (ROUTER_SUBTILES * tm) == 0 else 1
    ts = n_sub * tm
    per_batch = s // ts
    nt = t // tm
    before, ones, lower = consts
    tok = lambda w: pl.BlockSpec((ts, w), lambda i: (i, 0))
    vec = pl.BlockSpec((1, 1, d), lambda i: (i // per_batch, 0, 0))
    full = lambda a: pl.BlockSpec(a.shape, lambda i: (0,) * a.ndim)
    lanes = pl.BlockSpec((2, ts), lambda i: (0, i))
    return pl.pallas_call(
        _out_kernel,
        grid=(t // ts,),
        in_specs=[tok(POOL_WIDTH), tok(HGRN_WIDTH), tok(ATT_WIDTH), tok(d), vec, vec, vec,
                  pl.BlockSpec((1, d), lambda i: (0, 0)),
                  pl.BlockSpec((None,) + wo_b.shape[1:], lambda i: (layer, 0, 0)), full(rwt_b),
                  pl.BlockSpec((N_EXPERTS, 1), lambda i: (0, 0)), full(before), full(ones), full(lower)],
        out_specs=[tok(d), tok(d), lanes, lanes, pl.BlockSpec((n_sub, N_EXPERTS, 128), lambda i: (i, 0, 0))],
        out_shape=[jax.ShapeDtypeStruct((t, d), F32), jax.ShapeDtypeStruct((t, d), BF16),
                   jax.ShapeDtypeStruct((2, t), F32), jax.ShapeDtypeStruct((2, t), F32),
                   jax.ShapeDtypeStruct((nt, N_EXPERTS, 128), F32)],
        compiler_params=_cparams(("parallel",)),
        name="out_proj_router",
    )(yp.reshape(t, -1), yh.reshape(t, -1), ya.reshape(t, -1), x.reshape(t, d),
      g1.reshape(b, 1, d), sc.reshape(b, 1, d), sh.reshape(b, 1, d), nw.reshape(1, d), wo_b, rwt_b,
      rb.reshape(N_EXPERTS, 1), before, ones, lower)


def _moe_tables(cnt_out):
    cnt = jnp.round(cnt_out[:, :, 0]).astype(jnp.int32)
    cnt = (cnt + ROW_ALIGN - 1) // ROW_ALIGN * ROW_ALIGN
    total = jnp.sum(cnt, axis=0)
    padded = (total + EXPERT_TILE - 1) // EXPERT_TILE * EXPERT_TILE
    ends = jnp.cumsum(padded)
    first = ends - padded
    start = first[None, :] + jnp.cumsum(cnt, axis=0) - cnt
    loff = jnp.cumsum(cnt, axis=1) - cnt
    n_tiles = _moe_rows(cnt.shape[0] * MOE_TOK_TILE) // EXPERT_TILE
    n_used = ends[-1] // EXPERT_TILE
    tile_row = jnp.minimum(jnp.arange(n_tiles), n_used - 1) * EXPERT_TILE
    tile_expert = jnp.sum((ends[None, :] <= tile_row[:, None]).astype(jnp.int32), axis=1)
    pieces = jnp.sum(cnt, axis=1, keepdims=True) // ROW_ALIGN

    def copies(n_per_expert, offset_in_chunk, n_max):
        incl = jnp.cumsum(n_per_expert, axis=1)
        k = jnp.arange(n_max, dtype=jnp.int32)
        owner = jnp.sum((incl[:, None, :] <= k[None, :, None]).astype(jnp.int32), axis=2)
        owner = jnp.minimum(owner, N_EXPERTS - 1)
        is_owner = owner[:, :, None] == jnp.arange(N_EXPERTS, dtype=jnp.int32)[None, None, :]
        pick = lambda a: jnp.sum(jnp.where(is_owner, a[:, None, :], 0), axis=2)
        off = offset_in_chunk(k[None, :] - pick(incl - n_per_expert), pick(cnt))
        return pick(loff) + off, pick(start) + off, incl[:, -1:]

    big_src, big_dst, n_big = copies(cnt // (2 * ROW_ALIGN), lambda j, c: j * (2 * ROW_ALIGN), MAX_PIECES // 2)
    small_src, small_dst, n_small = copies(cnt % (2 * ROW_ALIGN) // ROW_ALIGN, lambda j, c: c - ROW_ALIGN, N_EXPERTS)
    moves = jnp.concatenate([big_src, big_dst, small_src, small_dst, n_big, n_small, pieces], axis=1)
    free_pieces = (n_tiles - n_used) * (EXPERT_TILE // ZERO_ROWS)
    tail = jnp.concatenate([first + total, padded - total, jnp.stack([ends[-1], free_pieces])])
    return (moves.reshape(-1).astype(jnp.int32), tail.astype(jnp.int32), tile_expert.astype(jnp.int32),
            n_used.reshape(1).astype(jnp.int32))


MOVE_BIG_SRC = 0
MOVE_BIG_DST = MOVE_BIG_SRC + MAX_PIECES // 2
MOVE_SMALL_SRC = MOVE_BIG_DST + MAX_PIECES // 2
MOVE_SMALL_DST = MOVE_SMALL_SRC + N_EXPERTS
MOVE_N_BIG = MOVE_SMALL_DST + N_EXPERTS
MOVE_N_SMALL = MOVE_N_BIG + 1
MOVE_PIECES = MOVE_N_SMALL + 1
MOVE_WIDTH = MOVE_PIECES + 1


def _tile_copies(moves_ref, tile, make_copy):
    base = tile * MOVE_WIDTH
    for src0, dst0, n_at, size in ((MOVE_BIG_SRC, MOVE_BIG_DST, MOVE_N_BIG, 2 * ROW_ALIGN),
                                   (MOVE_SMALL_SRC, MOVE_SMALL_DST, MOVE_N_SMALL, ROW_ALIGN)):
        def body(j, carry):
            make_copy(pl.multiple_of(moves_ref[base + src0 + j], ROW_ALIGN),
                      pl.multiple_of(moves_ref[base + dst0 + j], ROW_ALIGN), size).start()
            return carry

        lax.fori_loop(0, moves_ref[base + n_at], body, 0)


def _moe_rows(n_tokens):
    per_tile = 2 * MOE_TOK_TILE + N_EXPERTS * (ROW_ALIGN - 1)
    rows = (n_tokens // MOE_TOK_TILE) * per_tile + N_EXPERTS * (EXPERT_TILE - ROW_ALIGN)
    return (rows + EXPERT_TILE - 1) // EXPERT_TILE * EXPERT_TILE


def _chunk_copies(count, make_copy, wait=False):
    for size in CHUNK_SIZES:
        offset = jnp.bitwise_and(count, ~(2 * size - 1))

        @pl.when(jnp.bitwise_and(count, size) != 0)
        def _():
            copy = make_copy(pl.multiple_of(offset, ROW_ALIGN), size)
            copy.wait() if wait else copy.start()


def _wait_rows(n_pieces, make_copy):
    for p in WAIT_PIECES:
        @pl.when(jnp.bitwise_and(n_pieces, p) != 0)
        def _():
            make_copy(p * ROW_ALIGN).wait()


def _tiles_per_step(n_tiles, most):
    return next(k for k in (most, 2, 1) if n_tiles % k == 0)


def _slot_rows(tm):
    return lax.broadcasted_iota(jnp.int32, (MOE_SLOTS, tm), 0).astype(F32)


def _sort_kernel(moves_ref, tail_ref, h2_ref, ls_ref, xs_ref, buf_ref, sem_ref):
    i = pl.program_id(0)
    n = pl.num_programs(0)
    tm = MOE_TOK_TILE
    n_sub = h2_ref.shape[0] // tm
    n_buf = buf_ref.shape[0]

    def sent(tile):
        s = tile % n_buf
        _wait_rows(moves_ref[tile * MOVE_WIDTH + MOVE_PIECES], lambda rows: pltpu.make_async_copy(
            buf_ref.at[s, pl.ds(0, rows), :], xs_ref.at[pl.ds(0, rows), :], sem_ref.at[s]))

    for u in range(n_sub):
        tile = i * n_sub + u
        slot = tile % n_buf

        @pl.when(tile >= n_buf)
        def _():
            sent(tile - n_buf)

        ls = ls_ref[:, u * tm:(u + 1) * tm]
        srow = _slot_rows(tm)
        perm = jnp.where(jnp.logical_or(srow == ls[0:1], srow == ls[1:2]), 1.0, 0.0).astype(BF16)
        buf_ref[slot] = _dot(perm, h2_ref[u * tm:(u + 1) * tm, :]).astype(BF16)
        _tile_copies(moves_ref, tile, lambda src, dst, size: pltpu.make_async_copy(
            buf_ref.at[slot, pl.ds(src, size), :], xs_ref.at[pl.ds(dst, size), :], sem_ref.at[slot]))

    @pl.when(i == n - 1)
    def _():
        last = n * n_sub - 1
        for back in reversed(range(n_buf)):
            @pl.when(last >= back)
            def _():
                sent(last - back)

        buf_ref[0] = jnp.zeros(buf_ref.shape[1:], BF16)
        zeros_to = lambda row, size: pltpu.make_async_copy(
            buf_ref.at[0, pl.ds(0, size), :], xs_ref.at[pl.ds(row, size), :], sem_ref.at[0])

        for wait in (False, True):
            def expert_tail(e, carry):
                st = pl.multiple_of(tail_ref[e], ROW_ALIGN)
                _chunk_copies(tail_ref[N_EXPERTS + e], lambda off, size: zeros_to(st + off, size), wait)
                return carry

            def free_tile(j, carry):
                copy = zeros_to(pl.multiple_of(tail_ref[2 * N_EXPERTS] + j * ZERO_ROWS, ROW_ALIGN), ZERO_ROWS)
                copy.wait() if wait else copy.start()
                return carry

            lax.fori_loop(0, N_EXPERTS, expert_tail, 0)
            lax.fori_loop(0, tail_ref[2 * N_EXPERTS + 1], free_tile, 0)


def _sort_tokens(h2, ls, moves, tail, n_rows):
    t, d = h2.shape
    tm = MOE_TOK_TILE * _tiles_per_step(t // MOE_TOK_TILE, SORT_TILES_PER_STEP)
    grid_spec = pltpu.PrefetchScalarGridSpec(
        num_scalar_prefetch=2,
        grid=(t // tm,),
        in_specs=[pl.BlockSpec((tm, d), lambda i, *_: (i, 0)),
                  pl.BlockSpec((2, tm), lambda i, *_: (0, i))],
        out_specs=pl.BlockSpec(memory_space=pl.ANY),
        scratch_shapes=[pltpu.VMEM((SORT_BUFFERS, MOE_SLOTS, d), BF16), pltpu.SemaphoreType.DMA((SORT_BUFFERS,))],
    )
    return pl.pallas_call(
        _sort_kernel,
        grid_spec=grid_spec,
        out_shape=jax.ShapeDtypeStruct((n_rows, d), BF16),
        compiler_params=_cparams(("arbitrary",)),
        name="moe_sort",
    )(moves, tail, h2, ls)


def _gmm_kernel(te_ref, nu_ref, xs_ref, wg_ref, wu_ref, wd_ref, ys_ref, wgb, wub, wdb):
    i = pl.program_id(0)
    used = i < nu_ref[0]
    new_expert = jnp.logical_or(i == 0, te_ref[i] != te_ref[jnp.maximum(i - 1, 0)])

    @pl.when(jnp.logical_and(used, new_expert))
    def _():
        wgb[...] = wg_ref[...].astype(BF16)
        wub[...] = wu_ref[...].astype(BF16)
        wdb[...] = wd_ref[...].astype(BF16)

    @pl.when(used)
    def _():
        x = xs_ref[...]
        he = _silu(_dot(x, wgb[...])) * _dot(x, wub[...])
        ys_ref[...] = _dot(he.astype(BF16), wdb[...]).astype(BF16)

    @pl.when(jnp.logical_not(used))
    def _():
        ys_ref[...] = jnp.zeros(ys_ref.shape, BF16)


def _grouped_mlp(xs, tile_expert, n_used, wg, wu, wd, layer):
    n_rows, d = xs.shape
    te = EXPERT_TILE
    row_map = lambda i, tex, nu: (jnp.minimum(i, nu[0] - 1), 0)
    grid_spec = pltpu.PrefetchScalarGridSpec(
        num_scalar_prefetch=2,
        grid=(n_rows // te,),
        in_specs=[pl.BlockSpec((te, d), row_map),
                  pl.BlockSpec((None, None, d, D_EXPERT), lambda i, tex, nu: (layer, tex[i], 0, 0)),
                  pl.BlockSpec((None, None, d, D_EXPERT), lambda i, tex, nu: (layer, tex[i], 0, 0)),
                  pl.BlockSpec((None, None, D_EXPERT, d), lambda i, tex, nu: (layer, tex[i], 0, 0))],
        out_specs=pl.BlockSpec((te, d), lambda i, tex, nu: (i, 0)),
        scratch_shapes=[pltpu.VMEM((d, D_EXPERT), BF16), pltpu.VMEM((d, D_EXPERT), BF16),
                        pltpu.VMEM((D_EXPERT, d), BF16)],
    )
    return pl.pallas_call(
        _gmm_kernel,
        grid_spec=grid_spec,
        out_shape=jax.ShapeDtypeStruct((n_rows, d), BF16),
        compiler_params=_cparams(("arbitrary",)),
        name="moe_grouped_mlp",
    )(tile_expert, n_used, xs, wg, wu, wd)


def _combine_kernel(moves_ref, ls_ref, gate_ref, x1_ref, g2_ref, ys_ref, o_ref, buf_ref, sem_ref):
    i = pl.program_id(0)
    n = pl.num_programs(0)
    tm = MOE_TOK_TILE
    n_sub = x1_ref.shape[0] // tm
    n_tiles = n * n_sub
    n_buf = buf_ref.shape[0]

    def fetch(tile):
        s = tile % n_buf
        _tile_copies(moves_ref, tile, lambda dst, src, size: pltpu.make_async_copy(
            ys_ref.at[pl.ds(src, size), :], buf_ref.at[s, pl.ds(dst, size), :], sem_ref.at[s]))

    @pl.when(i == 0)
    def _():
        buf_ref[...] = jnp.zeros(buf_ref.shape, BF16)
        for first in range(COMBINE_AHEAD):
            @pl.when(first < n_tiles)
            def _():
                fetch(first)

    for u in range(n_sub):
        tile = i * n_sub + u
        slot = tile % n_buf

        @pl.when(tile + COMBINE_AHEAD < n_tiles)
        def _():
            fetch(tile + COMBINE_AHEAD)

        n_pieces = moves_ref[tile * MOVE_WIDTH + MOVE_PIECES]
        _wait_rows(n_pieces, lambda rows: pltpu.make_async_copy(
            ys_ref.at[pl.ds(0, rows), :], buf_ref.at[slot, pl.ds(0, rows), :], sem_ref.at[slot]))

        rows_u = pl.ds(u * tm, tm)
        srow = _slot_rows(tm)
        ls = ls_ref[:, u * tm:(u + 1) * tm]
        gate = gate_ref[:, u * tm:(u + 1) * tm]
        weights = (jnp.where(srow == ls[0:1], gate[0:1], 0.0)
                   + jnp.where(srow == ls[1:2], gate[1:2], 0.0)).astype(BF16)
        y = _dot_tn(weights, buf_ref[slot])
        o_ref[rows_u, :] = x1_ref[rows_u, :] + g2_ref[0] * y


def _combine(ys, ls, gate, x1, g2, moves, seq_len):
    t, d = x1.shape
    tm = MOE_TOK_TILE * _tiles_per_step(seq_len // MOE_TOK_TILE, COMBINE_TILES_PER_STEP)
    per_batch = seq_len // tm
    grid_spec = pltpu.PrefetchScalarGridSpec(
        num_scalar_prefetch=1,
        grid=(t // tm,),
        in_specs=[pl.BlockSpec((2, tm), lambda i, *_: (0, i)),
                  pl.BlockSpec((2, tm), lambda i, *_: (0, i)),
                  pl.BlockSpec((tm, d), lambda i, *_: (i, 0)),
                  pl.BlockSpec((1, 1, d), lambda i, *_: (i // per_batch, 0, 0)),
                  pl.BlockSpec(memory_space=pl.ANY)],
        out_specs=pl.BlockSpec((tm, d), lambda i, *_: (i, 0)),
        scratch_shapes=[pltpu.VMEM((COMBINE_AHEAD + 1, MOE_SLOTS, d), BF16),
                        pltpu.SemaphoreType.DMA((COMBINE_AHEAD + 1,))],
    )
    return pl.pallas_call(
        _combine_kernel,
        grid_spec=grid_spec,
        out_shape=jax.ShapeDtypeStruct((t, d), F32),
        compiler_params=_cparams(("arbitrary",)),
        name="moe_combine",
    )(moves, ls, gate, x1, g2.reshape(-1, 1, d), ys)


def kernel(x, c, ada_w, ada_b, norm1_w, norm2_w, w_in, pool_w, pool_scale, hgrn_lb_raw, hgrn_norm_w, q_norm_w,
           k_norm_w, attn_sinks, w_out, router_w, router_bias, expert_w_gate, expert_w_up, expert_w_down):
    b, s, d = x.shape
    depth = ada_w.shape[0]
    t = b * s
    n_rows = _moe_rows(t)

    p = jax.nn.softmax(hgrn_lb_raw.astype(F32), axis=0)
    lower_bounds = jnp.maximum(jnp.cumsum(p, axis=0) - p[0:1], 0.0)

    mod = _modulation(c, ada_w, ada_b)
    hgrn_consts = _hgrn_constants() + (_head_ones(HGRN_WIDTH, HGRN_KDIM),)
    attn_consts = (_attn_bias(), _head_ones(ATT_WIDTH, HEAD_DIM), _head_ones(KV_WIDTH, HEAD_DIM))
    router_consts = _router_constants(MOE_TOK_TILE)
    rwt_b = router_w.T.astype(BF16)
    w_in_b = w_in.astype(BF16)
    w_out_b = w_out.astype(BF16)

    for l in range(depth):
        sh1, sc1, g1, sh2, sc2, g2 = [mod[l, :, j * d:(j + 1) * d] for j in range(6)]
        yp, zh, zq, zkv = _in_proj(x, sc1, sh1, norm1_w[l], w_in_b, l, _block_diag(pool_w[l]).astype(BF16),
                                   pool_scale[l])
        yh = _hgrn_mixer(zh, lower_bounds[l], hgrn_norm_w[l], hgrn_consts)
        ya = _attn_mixer(zq, zkv, q_norm_w[l], k_norm_w[l], attn_sinks[l], attn_consts)
        x1, h2, ls, gate, cnt_out = _out_proj_router(
            yp, yh, ya, x, g1, sc2, sh2, norm2_w[l], w_out_b, l, rwt_b, router_bias, router_consts)
        moves, tail, tile_expert, n_used = _moe_tables(cnt_out)
        xs = _sort_tokens(h2, ls, moves, tail, n_rows)
        ys = _grouped_mlp(xs, tile_expert, n_used, expert_w_gate, expert_w_up, expert_w_down, l)
        x = _combine(ys, ls, gate, x1, g2, moves, s).reshape(b, s, d)
    return x
```

```python
import functools

import numpy as np
import jax
import jax.numpy as jnp
from jax import lax
from jax.experimental import pallas as pl
from jax.experimental.pallas import tpu as pltpu

F32 = jnp.float32
BF16 = jnp.bfloat16

D_MODEL = 1024
POOL_WINDOWS = (2, 4, 8, 16)
POOL_WIDTH = 256
POOL_GROUP = 64
POOL_HALO = 32
HGRN_HEADS = 4
HGRN_KDIM = 64
HGRN_WIDTH = 256
HGRN_CHUNK = 64
ATT_HEADS = 8
ATT_KV_HEADS = 2
HEAD_DIM = 64
ATT_WIDTH = 512
KV_WIDTH = 128
WINDOW = 128
N_EXPERTS = 16
N_GROUPS = 4
EXPERTS_PER_GROUP = 4
D_EXPERT = 512
EPS = 1e-6
MAX_ONE_MINUS_F = 1.0 - 1e-6
LOG2E = 1.4426950408889634
NEG = -1e30

VMEM_LIMIT = 48 * 1024 * 1024

IN_TILE = 1024
ATT_TILE = 1024
MOE_TOK_TILE = 256
ROW_ALIGN = 16
MOE_SLOTS = 768
EXPERT_TILE = 1024
ZERO_ROWS = 512
CHUNK_SIZES = (512, 256, 128, 64, 32, 16)
SORT_TILES_PER_STEP = 4
COMBINE_TILES_PER_STEP = 2
COMBINE_AHEAD = 3
SORT_BUFFERS = 4
MAX_PIECES = MOE_SLOTS // ROW_ALIGN
WAIT_PIECES = (32, 16, 8, 4, 2, 1)


def _sigmoid(x):
    return 1.0 / (1.0 + jnp.exp(-x))


def _silu(x):
    return x * _sigmoid(x)


def _cparams(sem, **kw):
    return pltpu.CompilerParams(dimension_semantics=sem, vmem_limit_bytes=VMEM_LIMIT, **kw)


def _dot(a, b):
    return jnp.dot(a, b, preferred_element_type=F32)


def _dot_nt(a, b):
    return lax.dot_general(a, b, (((1,), (1,)), ((), ())), preferred_element_type=F32)


def _dot_tn(a, b):
    return lax.dot_general(a, b, (((0,), (0,)), ((), ())), preferred_element_type=F32)


def _head_sumsq(x, bd):
    return _dot((x * x).astype(BF16), bd)


def _mod_kernel(c_ref, w_ref, b_ref, o_ref):
    cond = _silu(c_ref[...])
    o_ref[0] = _dot(cond.astype(BF16), w_ref[0].astype(BF16)) + b_ref[0]


def _modulation(c, ada_w, ada_b):
    depth, d, n = ada_w.shape
    b = c.shape[0]
    nb = n // d
    return pl.pallas_call(
        _mod_kernel,
        grid=(depth, nb),
        in_specs=[pl.BlockSpec((b, d), lambda l, j: (0, 0)),
                  pl.BlockSpec((1, d, d), lambda l, j: (l, 0, j)),
                  pl.BlockSpec((1, 1, d), lambda l, j: (l, 0, j))],
        out_specs=pl.BlockSpec((1, b, d), lambda l, j: (l, 0, j)),
        out_shape=jax.ShapeDtypeStruct((depth, b, n), F32),
        compiler_params=_cparams(("parallel", "parallel")),
        name="adaln_mod",
    )(c, ada_w, ada_b.reshape(depth, 1, n))


def _in_kernel(x_ref, sc_ref, sh_ref, nw_ref, w_ref, pw_ref, ps_ref, yp_ref, zh_ref, zq_ref, zkv_ref,
               buf_ref, sa_ref, sb_ref):
    x = x_ref[0]
    ms = jnp.mean(x * x, axis=-1, keepdims=True)
    h = (x * lax.rsqrt(ms + EPS) * nw_ref[...]) * (1.0 + sc_ref[0]) + sh_ref[0]
    z = _dot(h.astype(BF16), w_ref[...])
    zh_ref[0] = z[:, POOL_WIDTH:POOL_WIDTH + 4 * HGRN_WIDTH]
    zq_ref[0] = z[:, POOL_WIDTH + 4 * HGRN_WIDTH:POOL_WIDTH + 4 * HGRN_WIDTH + ATT_WIDTH]
    zkv_ref[0] = z[:, POOL_WIDTH + 4 * HGRN_WIDTH + ATT_WIDTH:]
    yp_ref[0] = _pool(z[:, :POOL_WIDTH], pl.program_id(1), pw_ref, ps_ref, buf_ref, sa_ref, sb_ref).astype(yp_ref.dtype)


def _in_proj(x, sc, sh, nw, w_in_b, layer, pool_bd, pool_scale):
    b, s, d = x.shape
    n = w_in_b.shape[2]
    ts = min(IN_TILE, s)
    tok = lambda w: pl.BlockSpec((1, ts, w), lambda bi, i: (bi, i, 0))
    vec = pl.BlockSpec((1, 1, d), lambda bi, i: (bi, 0, 0))
    full = lambda a: pl.BlockSpec(a.shape, lambda bi, i: (0,) * a.ndim)
    pool_scale = pool_scale.reshape(1, POOL_WIDTH)
    outs = ((POOL_WIDTH, BF16), (4 * HGRN_WIDTH, F32), (ATT_WIDTH, F32), (2 * KV_WIDTH, F32))
    return pl.pallas_call(
        _in_kernel,
        grid=(b, s // ts),
        in_specs=[tok(d), vec, vec,
                  pl.BlockSpec((1, d), lambda bi, i: (0, 0)),
                  pl.BlockSpec((None, d, n), lambda bi, i: (layer, 0, 0)), full(pool_bd), full(pool_scale)],
        out_specs=[tok(w) for w, _ in outs],
        out_shape=[jax.ShapeDtypeStruct((b, s, w), dt) for w, dt in outs],
        scratch_shapes=[pltpu.VMEM((POOL_HALO + ts, POOL_WIDTH), F32)] * 3,
        compiler_params=_cparams(("parallel", "arbitrary")),
        name="in_proj_pool",
    )(x, sc.reshape(b, 1, d), sh.reshape(b, 1, d), nw.reshape(1, d), w_in_b, pool_bd, pool_scale)


def _pool(a, i, w_ref, scale_ref, buf_ref, sa_ref, sb_ref):
    ts = a.shape[0]
    halo = POOL_HALO
    end = halo + ts
    half = 2 * POOL_GROUP

    @pl.when(i == 0)
    def _():
        buf_ref[0:halo, :] = jnp.zeros((halo, POOL_WIDTH), F32)

    buf_ref[halo:end, :] = a
    s2 = buf_ref[8:end, :] + buf_ref[7:end - 1, :]
    sa_ref[8:end, :] = s2
    s4 = sa_ref[16:end, :] + sa_ref[14:end - 2, :]
    sb_ref[16:end, :] = s4
    s8 = sb_ref[24:end, half:] + sb_ref[20:end - 4, half:]
    sa_ref[24:end, half:] = s8
    s16 = sa_ref[halo:end, half:] + sa_ref[halo - 8:end - 8, half:]
    lane = lax.broadcasted_iota(jnp.int32, (1, POOL_WIDTH), 1)
    win = jnp.left_shift(2, lane // POOL_GROUP)
    low = lane[:, 0:half] % half < POOL_GROUP
    acc = jnp.concatenate([jnp.where(low, s2[halo - 8:, 0:half], s4[halo - 16:, 0:half]),
                           jnp.where(low, s8[halo - 24:], s16)], axis=1)
    pos = i * ts + lax.broadcasted_iota(jnp.int32, (ts, 1), 0)
    count = jnp.minimum(pos + 1, win).astype(F32)
    pooled = acc / count - a
    buf_ref[0:halo, :] = a[ts - halo:, :]
    return _dot(pooled.astype(BF16), w_ref[...]) * scale_ref[...]


def _block_diag(blocks):
    g, n, _ = blocks.shape
    eye = jnp.eye(g, dtype=blocks.dtype)
    return (eye[:, None, :, None] * blocks[:, :, None, :]).reshape(g * n, g * n)


def _head_ones(width, head):
    idx = np.arange(width) // head
    return jnp.asarray((idx[:, None] == idx[None, :]).astype(np.float32), dtype=BF16)


HGRN_LEVELS = (32, 16, 8, 4, 2, 1)
HGRN_MATMUL_LEVELS = (2, 1)
N_LEVEL_MASKS = len(HGRN_LEVELS) + 1
HGRN_SEQS = 8
HGRN_TILE = 256
ATT_GROUP = 8
ROUTER_SUBTILES = 4


def _hgrn_constants():
    c = HGRN_CHUNK
    m = np.zeros((len(HGRN_MATMUL_LEVELS) + 1, c, c), np.float32)
    masks = np.zeros((N_LEVEL_MASKS, c, c), np.float32)
    for lvl, n in enumerate(HGRN_LEVELS):
        for t in range(c):
            blk = t // (2 * n)
            mid = blk * 2 * n + n
            if t >= mid:
                masks[lvl, t, blk * 2 * n:mid] = 1.0
            if n in HGRN_MATMUL_LEVELS:
                row = m[HGRN_MATMUL_LEVELS.index(n), t]
                if t >= mid:
                    row[mid:t + 1] = 1.0
                else:
                    row[t + 1:mid] = 1.0
    for t in range(c):
        m[-1, t, :t + 1] = 1.0
        masks[-1, t, t] = 1.0
    m = m.reshape(-1, c)
    m3 = np.concatenate([m, m, m], axis=1)
    masks = np.tile(masks, (1, HGRN_HEADS, 1))
    return jnp.asarray(m3, dtype=BF16), jnp.asarray(masks, dtype=F32)


def _hgrn_chunks(units, bd, m3, lmask_ref):
    c = HGRN_CHUNK
    w = HGRN_WIDTH
    n_fine = len(HGRN_MATMUL_LEVELS)
    lane_head = lax.broadcasted_iota(jnp.int32, (1, w), 1) // HGRN_KDIM
    sums = [_dot(m3, u[2]) for u in units]
    bs = [s[n_fine * c:] for s in sums]

    def level_decay(u, n):
        if n in HGRN_MATMUL_LEVELS:
            k = HGRN_MATMUL_LEVELS.index(n)
            return jnp.exp(sums[u][k * c:(k + 1) * c])
        blocks = bs[u].reshape(c // (2 * n), 2 * n, w)
        ref = blocks[:, n - 1:n, :]
        right = lax.broadcasted_iota(jnp.int32, (1, 2 * n, 1), 1) >= n
        return jnp.exp(jnp.where(right, blocks - ref, ref - blocks).reshape(c, w))

    groups = c // 8
    parts = [[[None] * groups for _ in range(HGRN_HEADS)] for _ in units]

    def add_part(u, h, g, piece):
        parts[u][h][g] = piece if parts[u][h][g] is None else parts[u][h][g] + piece

    for lvl in range(N_LEVEL_MASKS):
        n = HGRN_LEVELS[lvl] if lvl < len(HGRN_LEVELS) else 0
        wanted = [g for g in range(groups) if (8 * g) % (2 * n) >= n] if n >= 8 else list(range(groups))
        for u, (qf, kk, _, _, _, _) in enumerate(units):
            if lvl < len(HGRN_LEVELS):
                e = level_decay(u, n)
                ql = qf * e
                kl = (kk * e).astype(BF16)
            else:
                ql = qf
                kl = kk.astype(BF16)
            ql = jnp.concatenate([ql[8 * g:8 * g + 8] for g in wanted], axis=0).astype(BF16)
            zero = jnp.zeros_like(ql)
            qs = jnp.concatenate([jnp.where(lane_head == h, ql, zero) for h in range(HGRN_HEADS)], axis=0)
            res = _dot_nt(qs, kl)
            for h in range(HGRN_HEADS):
                for k, g in enumerate(wanted):
                    r0 = (h * len(wanted) + k) * 8
                    add_part(u, h, g, res[r0:r0 + 8] * lmask_ref[lvl, h * c + 8 * g:h * c + 8 * g + 8, :])
    scores = [jnp.concatenate([parts[u][h][g] for h in range(HGRN_HEADS) for g in range(groups)], axis=0)
              for u in range(len(units))]
    rs = [_dot(scores[u].astype(BF16), unit[3]) for u, unit in enumerate(units)]
    inters = [_dot_nt((unit[0] * jnp.exp(bs[u])).astype(BF16), unit[5].astype(BF16)) for u, unit in enumerate(units)]
    upds = [_dot_tn(unit[3], (unit[1] * jnp.exp(bs[u][c - 1:c, :] - bs[u])).astype(BF16))
            for u, unit in enumerate(units)]
    outs = []
    for u in range(len(units)):
        o = inters[u]
        for h in range(HGRN_HEADS):
            o = o + jnp.where(lane_head == h, rs[u][h * c:(h + 1) * c], 0.0)
        outs.append(o)
    sss = [_head_sumsq(o, bd) for o in outs]
    res = []
    for u, unit in enumerate(units):
        st = unit[5] * jnp.exp(bs[u][c - 1:c, :]) + jnp.where(bd > 0, upds[u], 0.0)
        y = outs[u] * lax.rsqrt(sss[u] * (1.0 / HGRN_KDIM) + EPS) * unit[4]
        res.append((y, st))
    return res


def _hgrn_kernel(zh_ref, lb_ref, gw_ref, m3_ref, lmask_ref, bd_ref, o_ref, st_ref):
    i = pl.program_id(1)
    nb, ts = zh_ref.shape[0], zh_ref.shape[1]
    c = HGRN_CHUNK
    w = HGRN_WIDTH

    @pl.when(i == 0)
    def _():
        st_ref[...] = jnp.zeros(st_ref.shape, F32)

    one_minus_lb = 1.0 - lb_ref[...]
    gw = gw_ref[...]
    bd = bd_ref[...]
    m3 = m3_ref[...]

    def chunk(ci, carry):
        rows = pl.ds(pl.multiple_of(ci * c, c), c)
        units = []
        for s in range(nb):
            q, f, v, g = [zh_ref[s, rows, j * w:(j + 1) * w] for j in range(4)]
            kk = one_minus_lb * _sigmoid(-f)
            lf = jnp.log(1.0 - jnp.minimum(kk, MAX_ONE_MINUS_F))
            hi = lf.astype(BF16)
            r1 = lf - hi.astype(F32)
            mid = r1.astype(BF16)
            lo = (r1 - mid.astype(F32)).astype(BF16)
            units.append((_silu(q), kk, jnp.concatenate([hi, mid, lo], axis=0), v.astype(BF16), gw * _silu(g),
                          st_ref[s]))
        for s, (y, st) in enumerate(_hgrn_chunks(units, bd, m3, lmask_ref)):
            st_ref[s] = st
            o_ref[s, rows, :] = y.astype(o_ref.dtype)
        return carry

    lax.fori_loop(0, ts // c, chunk, 0)


def _hgrn_mixer(zh, lb, norm_w, consts):
    b, s, _ = zh.shape
    w = HGRN_WIDTH
    ts = min(HGRN_TILE, s)
    nb = HGRN_SEQS if b % HGRN_SEQS == 0 else 1
    m3, lmask, bd = consts
    full = lambda a: pl.BlockSpec(a.shape, lambda bi, i: (0,) * a.ndim)
    return pl.pallas_call(
        _hgrn_kernel,
        grid=(b // nb, s // ts),
        in_specs=[pl.BlockSpec((nb, ts, 4 * w), lambda bi, i: (bi, i, 0)),
                  pl.BlockSpec((1, w), lambda bi, i: (0, 0)),
                  pl.BlockSpec((1, w), lambda bi, i: (0, 0)),
                  full(m3), full(lmask), full(bd)],
        out_specs=pl.BlockSpec((nb, ts, w), lambda bi, i: (bi, i, 0)),
        out_shape=jax.ShapeDtypeStruct((b, s, w), BF16),
        scratch_shapes=[pltpu.VMEM((nb, w, w), F32)],
        compiler_params=_cparams(("parallel", "arbitrary")),
        name="hgrn_mixer",
    )(zh, lb.reshape(1, w), norm_w.reshape(1, w), m3, lmask, bd)


def _attn_bias():
    qi = np.arange(WINDOW)[:, None]
    kj = np.arange(2 * WINDOW)[None, :]
    dist = qi + WINDOW - kj
    valid = (dist >= 0) & (dist < WINDOW)
    slopes = np.exp2(-8.0 * np.arange(1, ATT_HEADS + 1) / ATT_HEADS)
    bias = np.where(valid[None], -slopes[:, None, None] * dist[None] * LOG2E, NEG)
    first = np.where(kj[None] < WINDOW, NEG, bias)
    return jnp.asarray(np.concatenate([bias, first]), dtype=F32)


def _attn_kernel(sink_ref, zq_ref, zkv_ref, qw_ref, kw_ref, bias_ref, bdq_ref, bdk_ref, o_ref,
                 qbuf, kbuf, vbuf):
    i = pl.program_id(1)
    ts = zq_ref.shape[1]
    hw = 2 * HEAD_DIM

    @pl.when(i == 0)
    def _():
        kbuf[:, 0:WINDOW, :] = jnp.zeros((4, WINDOW, hw), BF16)
        vbuf[:, 0:WINDOW, :] = jnp.zeros((4, WINDOW, hw), BF16)

    q = zq_ref[0]
    ssq = _head_sumsq(q, bdq_ref[...])
    qbuf[...] = (q * lax.rsqrt(ssq * (1.0 / HEAD_DIM) + EPS) * (qw_ref[...] * (HEAD_DIM ** -0.5 * LOG2E))).astype(BF16)
    kv = zkv_ref[0]
    k = kv[:, :KV_WIDTH]
    v = kv[:, KV_WIDTH:]
    ssk = _head_sumsq(k, bdk_ref[...])
    kn = k * lax.rsqrt(ssk * (1.0 / HEAD_DIM) + EPS) * kw_ref[...]
    kr = pltpu.roll(kn, HEAD_DIM, 1)
    vr = pltpu.roll(v, HEAD_DIM, 1)
    low = lax.broadcasted_iota(jnp.int32, (1, hw), 1) < HEAD_DIM
    for j in range(ATT_KV_HEADS):
        for half in range(2):
            keep = low if half == 0 else jnp.logical_not(low)
            ksrc = kn if j == half else kr
            vsrc = v if j == half else vr
            kbuf[2 * j + half, WINDOW:WINDOW + ts, :] = jnp.where(keep, ksrc, 0.0).astype(BF16)
            vbuf[2 * j + half, WINDOW:WINDOW + ts, :] = jnp.where(keep, vsrc, 0.0).astype(BF16)

    def block(n, carry):
        r0 = pl.multiple_of(n * WINDOW, WINDOW)
        table = jnp.where(jnp.logical_and(i == 0, n == 0), ATT_HEADS, 0)
        for g0 in range(0, ATT_HEADS, ATT_GROUP):
            heads = range(g0, g0 + ATT_GROUP)
            logits = []
            for h in heads:
                hp, half = h // 2, h % 2
                j = h // (ATT_HEADS // ATT_KV_HEADS)
                qp = qbuf[pl.ds(r0, WINDOW), hp * hw:(hp + 1) * hw]
                keys = kbuf[2 * j + half, pl.ds(r0, 2 * WINDOW), :]
                logits.append(_dot_nt(qp, keys) + bias_ref[table + h])
            ps, scales = [], []
            for h, lg in zip(heads, logits):
                sink = sink_ref[h] * LOG2E
                m = jnp.maximum(jnp.max(lg, axis=-1, keepdims=True), sink)
                p = jnp.exp2(lg - m)
                scales.append(1.0 / (jnp.sum(p, axis=-1, keepdims=True) + jnp.exp2(sink - m)))
                ps.append(p.astype(BF16))
            outs = []
            for h, p in zip(heads, ps):
                half = h % 2
                j = h // (ATT_HEADS // ATT_KV_HEADS)
                vals = vbuf[2 * j + half, pl.ds(r0, 2 * WINDOW), :]
                outs.append(_dot(p, vals))
            for k in range(0, ATT_GROUP, 2):
                hp = (g0 + k) // 2
                acc = outs[k] * scales[k] + outs[k + 1] * scales[k + 1]
                o_ref[0, pl.ds(r0, WINDOW), hp * hw:(hp + 1) * hw] = acc.astype(o_ref.dtype)
        return carry

    lax.fori_loop(0, ts // WINDOW, block, 0)
    kbuf[:, 0:WINDOW, :] = kbuf[:, ts:ts + WINDOW, :]
    vbuf[:, 0:WINDOW, :] = vbuf[:, ts:ts + WINDOW, :]


def _attn_mixer(zq, zkv, q_norm_w, k_norm_w, sinks, consts):
    b, s, _ = zq.shape
    ts = min(ATT_TILE, s)
    bias, bdq, bdk = consts
    hw = 2 * HEAD_DIM
    qw = jnp.tile(q_norm_w, ATT_HEADS).reshape(1, ATT_WIDTH)
    kw = jnp.tile(k_norm_w, ATT_KV_HEADS).reshape(1, KV_WIDTH)
    full = lambda a: pl.BlockSpec(a.shape, lambda bi, i, sk: (0,) * a.ndim)
    grid_spec = pltpu.PrefetchScalarGridSpec(
        num_scalar_prefetch=1,
        grid=(b, s // ts),
        in_specs=[pl.BlockSpec((1, ts, ATT_WIDTH), lambda bi, i, sk: (bi, i, 0)),
                  pl.BlockSpec((1, ts, 2 * KV_WIDTH), lambda bi, i, sk: (bi, i, 0)),
                  full(qw), full(kw), full(bias), full(bdq), full(bdk)],
        out_specs=pl.BlockSpec((1, ts, ATT_WIDTH), lambda bi, i, sk: (bi, i, 0)),
        scratch_shapes=[pltpu.VMEM((ts, ATT_WIDTH), BF16),
                        pltpu.VMEM((4, WINDOW + ts, hw), BF16),
                        pltpu.VMEM((4, WINDOW + ts, hw), BF16)],
    )
    return pl.pallas_call(
        _attn_kernel,
        grid_spec=grid_spec,
        out_shape=jax.ShapeDtypeStruct((b, s, ATT_WIDTH), BF16),
        compiler_params=_cparams(("parallel", "arbitrary")),
        name="swa_mixer",
    )(sinks, zq, zkv, qw, kw, bias, bdq, bdk)


def _router_constants(tm):
    t = np.arange(tm)
    before = (t[:, None] < t[None, :]).astype(np.float32)
    e = np.arange(N_EXPERTS)
    lower = (e[None, :] < e[:, None]).astype(np.float32)
    return (jnp.asarray(before, dtype=BF16), jnp.asarray(np.ones((tm, tm), np.float32), dtype=BF16),
            jnp.asarray(lower, dtype=BF16))


def _route(sel, scores):
    tm = sel.shape[1]
    group_scores = []
    for g in range(N_GROUPS):
        rows = [sel[g * EXPERTS_PER_GROUP + a:g * EXPERTS_PER_GROUP + a + 1] for a in range(EXPERTS_PER_GROUP)]
        best_pair = None
        for a in range(EXPERTS_PER_GROUP):
            for bb in range(a + 1, EXPERTS_PER_GROUP):
                pair = rows[a] + rows[bb]
                best_pair = pair if best_pair is None else jnp.maximum(best_pair, pair)
        group_scores.append(best_pair)
    top = functools.reduce(jnp.maximum, group_scores)
    best = jnp.full((1, tm), N_GROUPS - 1, jnp.int32)
    for g in reversed(range(N_GROUPS - 1)):
        best = jnp.where(group_scores[g] == top, g, best)
    row = lax.broadcasted_iota(jnp.int32, (N_EXPERTS, tm), 0)
    cand = jnp.where(row // EXPERTS_PER_GROUP == best, sel, NEG)
    m1 = jnp.max(cand, axis=0, keepdims=True)
    i1 = jnp.min(jnp.where(cand == m1, row, N_EXPERTS), axis=0, keepdims=True)
    oh1 = row == i1
    cand = jnp.where(oh1, NEG, cand)
    m2 = jnp.max(cand, axis=0, keepdims=True)
    i2 = jnp.min(jnp.where(cand == m2, row, N_EXPERTS), axis=0, keepdims=True)
    oh2 = row == i2
    s1 = jnp.sum(jnp.where(oh1, scores, 0.0), axis=0, keepdims=True)
    s2 = jnp.sum(jnp.where(oh2, scores, 0.0), axis=0, keepdims=True)
    return oh1, oh2, s1 / (s1 + s2), s2 / (s1 + s2)


def _out_kernel(yp_ref, yh_ref, ya_ref, x_ref, g1_ref, sc_ref, sh_ref, nw_ref, wo_ref, rwt_ref, rb_ref,
                before_ref, ones_ref, lower_ref, x1_ref, h2_ref, ls_ref, gate_ref, cnt_ref):
    tm = MOE_TOK_TILE
    subs = [pl.ds(u * tm, tm) for u in range(x_ref.shape[0] // tm)]
    p0, p1 = POOL_WIDTH, POOL_WIDTH + HGRN_WIDTH
    h2s = []
    for rows in subs:
        mix = (_dot(yp_ref[rows, :], wo_ref[0:p0, :]) + _dot(yh_ref[rows, :], wo_ref[p0:p1, :])
               + _dot(ya_ref[rows, :], wo_ref[p1:, :]))
        x1 = x_ref[rows, :] + g1_ref[0] * mix
        x1_ref[rows, :] = x1
        ms = jnp.mean(x1 * x1, axis=-1, keepdims=True)
        h2 = ((x1 * lax.rsqrt(ms + EPS) * nw_ref[...]) * (1.0 + sc_ref[0]) + sh_ref[0]).astype(BF16)
        h2_ref[rows, :] = h2
        h2s.append(h2)
    logits = [_dot_nt(rwt_ref[...], h2) for h2 in h2s]
    picks = []
    for lg in logits:
        ex = jnp.exp(lg - jnp.max(lg, axis=0, keepdims=True))
        scores = ex / jnp.sum(ex, axis=0, keepdims=True)
        picks.append(_route(scores + rb_ref[...], scores))
    chosen = [jnp.where(jnp.logical_or(oh1, oh2), 1.0, 0.0).astype(BF16) for oh1, oh2, _, _ in picks]
    ranks = [_dot(ch, before_ref[...]) for ch in chosen]
    counts = [_dot(ch, ones_ref[...]) for ch in chosen]
    aligned = [(jnp.floor((cn + (ROW_ALIGN - 1)) * (1.0 / ROW_ALIGN)) * ROW_ALIGN).astype(BF16) for cn in counts]
    slots = [_dot(lower_ref[...], al) + rk for al, rk in zip(aligned, ranks)]
    for u, rows in enumerate(subs):
        oh1, oh2, w1, w2 = picks[u]
        gate_ref[:, rows] = jnp.concatenate([w1, w2], axis=0)
        ls_ref[:, rows] = jnp.concatenate([jnp.sum(jnp.where(oh1, slots[u], 0.0), axis=0, keepdims=True),
                                           jnp.sum(jnp.where(oh2, slots[u], 0.0), axis=0, keepdims=True)], axis=0)
        cnt_ref[u] = counts[u][:, 0:128]


def _out_proj_router(yp, yh, ya, x, g1, sc, sh, nw, wo_b, layer, rwt_b, rb, consts):
    b, s, d = x.shape
    t = b * s
    tm = MOE_TOK_TILE
    n_sub = ROUTER_SUBTILES if s % (ROUTER_SUBTILES * tm) == 0 else 1
    ts = n_sub * tm
    per_batch = s // ts
    nt = t // tm
    before, ones, lower = consts
    tok = lambda w: pl.BlockSpec((ts, w), lambda i: (i, 0))
    vec = pl.BlockSpec((1, 1, d), lambda i: (i // per_batch, 0, 0))
    full = lambda a: pl.BlockSpec(a.shape, lambda i: (0,) * a.ndim)
    lanes = pl.BlockSpec((2, ts), lambda i: (0, i))
    return pl.pallas_call(
        _out_kernel,
        grid=(t // ts,),
        in_specs=[tok(POOL_WIDTH), tok(HGRN_WIDTH), tok(ATT_WIDTH), tok(d), vec, vec, vec,
                  pl.BlockSpec((1, d), lambda i: (0, 0)),
                  pl.BlockSpec((None,) + wo_b.shape[1:], lambda i: (layer, 0, 0)), full(rwt_b),
                  pl.BlockSpec((N_EXPERTS, 1), lambda i: (0, 0)), full(before), full(ones), full(lower)],
        out_specs=[tok(d), tok(d), lanes, lanes, pl.BlockSpec((n_sub, N_EXPERTS, 128), lambda i: (i, 0, 0))],
        out_shape=[jax.ShapeDtypeStruct((t, d), F32), jax.ShapeDtypeStruct((t, d), BF16),
                   jax.ShapeDtypeStruct((2, t), F32), jax.ShapeDtypeStruct((2, t), F32),
                   jax.ShapeDtypeStruct((nt, N_EXPERTS, 128), F32)],
        compiler_params=_cparams(("parallel",)),
        name="out_proj_router",
    )(yp.reshape(t, -1), yh.reshape(t, -1), ya.reshape(t, -1), x.reshape(t, d),
      g1.reshape(b, 1, d), sc.reshape(b, 1, d), sh.reshape(b, 1, d), nw.reshape(1, d), wo_b, rwt_b,
      rb.reshape(N_EXPERTS, 1), before, ones, lower)


def _moe_tables(cnt_out):
    cnt = jnp.round(cnt_out[:, :, 0]).astype(jnp.int32)
    cnt = (cnt + ROW_ALIGN - 1) // ROW_ALIGN * ROW_ALIGN
    total = jnp.sum(cnt, axis=0)
    padded = (total + EXPERT_TILE - 1) // EXPERT_TILE * EXPERT_TILE
    ends = jnp.cumsum(padded)
    first = ends - padded
    start = first[None, :] + jnp.cumsum(cnt, axis=0) - cnt
    loff = jnp.cumsum(cnt, axis=1) - cnt
    n_tiles = _moe_rows(cnt.shape[0] * MOE_TOK_TILE) // EXPERT_TILE
    n_used = ends[-1] // EXPERT_TILE
    tile_row = jnp.minimum(jnp.arange(n_tiles), n_used - 1) * EXPERT_TILE
    tile_expert = jnp.sum((ends[None, :] <= tile_row[:, None]).astype(jnp.int32), axis=1)
    pieces = jnp.sum(cnt, axis=1, keepdims=True) // ROW_ALIGN

    def copies(n_per_expert, offset_in_chunk, n_max):
        incl = jnp.cumsum(n_per_expert, axis=1)
        k = jnp.arange(n_max, dtype=jnp.int32)
        owner = jnp.sum((incl[:, None, :] <= k[None, :, None]).astype(jnp.int32), axis=2)
        owner = jnp.minimum(owner, N_EXPERTS - 1)
        is_owner = owner[:, :, None] == jnp.arange(N_EXPERTS, dtype=jnp.int32)[None, None, :]
        pick = lambda a: jnp.sum(jnp.where(is_owner, a[:, None, :], 0), axis=2)
        off = offset_in_chunk(k[None, :] - pick(incl - n_per_expert), pick(cnt))
        return pick(loff) + off, pick(start) + off, incl[:, -1:]

    big_src, big_dst, n_big = copies(cnt // (2 * ROW_ALIGN), lambda j, c: j * (2 * ROW_ALIGN), MAX_PIECES // 2)
    small_src, small_dst, n_small = copies(cnt % (2 * ROW_ALIGN) // ROW_ALIGN, lambda j, c: c - ROW_ALIGN, N_EXPERTS)
    moves = jnp.concatenate([big_src, big_dst, small_src, small_dst, n_big, n_small, pieces], axis=1)
    free_pieces = (n_tiles - n_used) * (EXPERT_TILE // ZERO_ROWS)
    tail = jnp.concatenate([first + total, padded - total, jnp.stack([ends[-1], free_pieces])])
    return (moves.reshape(-1).astype(jnp.int32), tail.astype(jnp.int32), tile_expert.astype(jnp.int32),
            n_used.reshape(1).astype(jnp.int32))


MOVE_BIG_SRC = 0
MOVE_BIG_DST = MOVE_BIG_SRC + MAX_PIECES // 2
MOVE_SMALL_SRC = MOVE_BIG_DST + MAX_PIECES // 2
MOVE_SMALL_DST = MOVE_SMALL_SRC + N_EXPERTS
MOVE_N_BIG = MOVE_SMALL_DST + N_EXPERTS
MOVE_N_SMALL = MOVE_N_BIG + 1
MOVE_PIECES = MOVE_N_SMALL + 1
MOVE_WIDTH = MOVE_PIECES + 1


def _tile_copies(moves_ref, tile, make_copy):
    base = tile * MOVE_WIDTH
    for src0, dst0, n_at, size in ((MOVE_BIG_SRC, MOVE_BIG_DST, MOVE_N_BIG, 2 * ROW_ALIGN),
                                   (MOVE_SMALL_SRC, MOVE_SMALL_DST, MOVE_N_SMALL, ROW_ALIGN)):
        count = moves_ref[base + n_at]

        def start(j, priority):
            make_copy(pl.multiple_of(moves_ref[base + src0 + j], ROW_ALIGN),
                      pl.multiple_of(moves_ref[base + dst0 + j], ROW_ALIGN), size).start(priority=priority)

        def body(p, carry):
            start(2 * p, 0)

            @pl.when(2 * p + 1 < count)
            def _():
                start(2 * p + 1, 1)
            return carry

        lax.fori_loop(0, lax.shift_right_logical(count + 1, 1), body, 0)


def _moe_rows(n_tokens):
    per_tile = 2 * MOE_TOK_TILE + N_EXPERTS * (ROW_ALIGN - 1)
    rows = (n_tokens // MOE_TOK_TILE) * per_tile + N_EXPERTS * (EXPERT_TILE - ROW_ALIGN)
    return (rows + EXPERT_TILE - 1) // EXPERT_TILE * EXPERT_TILE


def _chunk_copies(count, make_copy, wait=False):
    for size in CHUNK_SIZES:
        offset = jnp.bitwise_and(count, ~(2 * size - 1))

        @pl.when(jnp.bitwise_and(count, size) != 0)
        def _():
            copy = make_copy(pl.multiple_of(offset, ROW_ALIGN), size)
            copy.wait() if wait else copy.start()


def _wait_rows(n_pieces, make_copy):
    for p in WAIT_PIECES:
        @pl.when(jnp.bitwise_and(n_pieces, p) != 0)
        def _():
            make_copy(p * ROW_ALIGN).wait()


def _tiles_per_step(n_tiles, most):
    return next(k for k in (most, 2, 1) if n_tiles % k == 0)


def _slot_rows(tm):
    return lax.broadcasted_iota(jnp.int32, (MOE_SLOTS, tm), 0).astype(F32)


def _sort_kernel(moves_ref, tail_ref, h2_ref, ls_ref, xs_ref, buf_ref, sem_ref):
    i = pl.program_id(0)
    n = pl.num_programs(0)
    tm = MOE_TOK_TILE
    n_sub = h2_ref.shape[0] // tm
    n_buf = buf_ref.shape[0]

    def sent(tile):
        s = tile % n_buf
        _wait_rows(moves_ref[tile * MOVE_WIDTH + MOVE_PIECES], lambda rows: pltpu.make_async_copy(
            buf_ref.at[s, pl.ds(0, rows), :], xs_ref.at[pl.ds(0, rows), :], sem_ref.at[s]))

    for u in range(n_sub):
        tile = i * n_sub + u
        slot = tile % n_buf

        @pl.when(tile >= n_buf)
        def _():
            sent(tile - n_buf)

        ls = ls_ref[:, u * tm:(u + 1) * tm]
        srow = _slot_rows(tm)
        perm = jnp.where(jnp.logical_or(srow == ls[0:1], srow == ls[1:2]), 1.0, 0.0).astype(BF16)
        buf_ref[slot] = _dot(perm, h2_ref[u * tm:(u + 1) * tm, :]).astype(BF16)
        _tile_copies(moves_ref, tile, lambda src, dst, size: pltpu.make_async_copy(
            buf_ref.at[slot, pl.ds(src, size), :], xs_ref.at[pl.ds(dst, size), :], sem_ref.at[slot]))

    @pl.when(i == n - 1)
    def _():
        last = n * n_sub - 1
        for back in reversed(range(n_buf)):
            @pl.when(last >= back)
            def _():
                sent(last - back)

        buf_ref[0] = jnp.zeros(buf_ref.shape[1:], BF16)
        zeros_to = lambda row, size: pltpu.make_async_copy(
            buf_ref.at[0, pl.ds(0, size), :], xs_ref.at[pl.ds(row, size), :], sem_ref.at[0])

        for wait in (False, True):
            def expert_tail(e, carry):
                st = pl.multiple_of(tail_ref[e], ROW_ALIGN)
                _chunk_copies(tail_ref[N_EXPERTS + e], lambda off, size: zeros_to(st + off, size), wait)
                return carry

            def free_tile(j, carry):
                copy = zeros_to(pl.multiple_of(tail_ref[2 * N_EXPERTS] + j * ZERO_ROWS, ROW_ALIGN), ZERO_ROWS)
                copy.wait() if wait else copy.start()
                return carry

            lax.fori_loop(0, N_EXPERTS, expert_tail, 0)
            lax.fori_loop(0, tail_ref[2 * N_EXPERTS + 1], free_tile, 0)


def _sort_tokens(h2, ls, moves, tail, n_rows):
    t, d = h2.shape
    tm = MOE_TOK_TILE * _tiles_per_step(t // MOE_TOK_TILE, SORT_TILES_PER_STEP)
    grid_spec = pltpu.PrefetchScalarGridSpec(
        num_scalar_prefetch=2,
        grid=(t // tm,),
        in_specs=[pl.BlockSpec((tm, d), lambda i, *_: (i, 0)),
                  pl.BlockSpec((2, tm), lambda i, *_: (0, i))],
        out_specs=pl.BlockSpec(memory_space=pl.ANY),
        scratch_shapes=[pltpu.VMEM((SORT_BUFFERS, MOE_SLOTS, d), BF16), pltpu.SemaphoreType.DMA((SORT_BUFFERS,))],
    )
    return pl.pallas_call(
        _sort_kernel,
        grid_spec=grid_spec,
        out_shape=jax.ShapeDtypeStruct((n_rows, d), BF16),
        compiler_params=_cparams(("arbitrary",)),
        name="moe_sort",
    )(moves, tail, h2, ls)


def _gmm_kernel(te_ref, nu_ref, xs_ref, wg_ref, wu_ref, wd_ref, ys_ref, wgb, wub, wdb):
    i = pl.program_id(0)
    used = i < nu_ref[0]
    new_expert = jnp.logical_or(i == 0, te_ref[i] != te_ref[jnp.maximum(i - 1, 0)])

    @pl.when(jnp.logical_and(used, new_expert))
    def _():
        wgb[...] = wg_ref[...].astype(BF16)
        wub[...] = wu_ref[...].astype(BF16)
        wdb[...] = wd_ref[...].astype(BF16)

    @pl.when(used)
    def _():
        x = xs_ref[...]
        he = _silu(_dot(x, wgb[...])) * _dot(x, wub[...])
        ys_ref[...] = _dot(he.astype(BF16), wdb[...]).astype(BF16)

    @pl.when(jnp.logical_not(used))
    def _():
        ys_ref[...] = jnp.zeros(ys_ref.shape, BF16)


def _grouped_mlp(xs, tile_expert, n_used, wg, wu, wd, layer):
    n_rows, d = xs.shape
    te = EXPERT_TILE
    row_map = lambda i, tex, nu: (jnp.minimum(i, nu[0] - 1), 0)
    grid_spec = pltpu.PrefetchScalarGridSpec(
        num_scalar_prefetch=2,
        grid=(n_rows // te,),
        in_specs=[pl.BlockSpec((te, d), row_map),
                  pl.BlockSpec((None, None, d, D_EXPERT), lambda i, tex, nu: (layer, tex[i], 0, 0)),
                  pl.BlockSpec((None, None, d, D_EXPERT), lambda i, tex, nu: (layer, tex[i], 0, 0)),
                  pl.BlockSpec((None, None, D_EXPERT, d), lambda i, tex, nu: (layer, tex[i], 0, 0))],
        out_specs=pl.BlockSpec((te, d), lambda i, tex, nu: (i, 0)),
        scratch_shapes=[pltpu.VMEM((d, D_EXPERT), BF16), pltpu.VMEM((d, D_EXPERT), BF16),
                        pltpu.VMEM((D_EXPERT, d), BF16)],
    )
    return pl.pallas_call(
        _gmm_kernel,
        grid_spec=grid_spec,
        out_shape=jax.ShapeDtypeStruct((n_rows, d), BF16),
        compiler_params=_cparams(("arbitrary",)),
        name="moe_grouped_mlp",
    )(tile_expert, n_used, xs, wg, wu, wd)


def _combine_kernel(moves_ref, ls_ref, gate_ref, x1_ref, g2_ref, ys_ref, o_ref, buf_ref, sem_ref):
    i = pl.program_id(0)
    n = pl.num_programs(0)
    tm = MOE_TOK_TILE
    n_sub = x1_ref.shape[0] // tm
    n_tiles = n * n_sub
    n_buf = buf_ref.shape[0]

    def fetch(tile):
        s = tile % n_buf
        _tile_copies(moves_ref, tile, lambda dst, src, size: pltpu.make_async_copy(
            ys_ref.at[pl.ds(src, size), :], buf_ref.at[s, pl.ds(dst, size), :], sem_ref.at[s]))

    @pl.when(i == 0)
    def _():
        buf_ref[...] = jnp.zeros(buf_ref.shape, BF16)
        for first in range(COMBINE_AHEAD):
            @pl.when(first < n_tiles)
            def _():
                fetch(first)

    for u in range(n_sub):
        tile = i * n_sub + u
        slot = tile % n_buf

        @pl.when(tile + COMBINE_AHEAD < n_tiles)
        def _():
            fetch(tile + COMBINE_AHEAD)

        n_pieces = moves_ref[tile * MOVE_WIDTH + MOVE_PIECES]
        _wait_rows(n_pieces, lambda rows: pltpu.make_async_copy(
            ys_ref.at[pl.ds(0, rows), :], buf_ref.at[slot, pl.ds(0, rows), :], sem_ref.at[slot]))

        rows_u = pl.ds(u * tm, tm)
        srow = _slot_rows(tm)
        ls = ls_ref[:, u * tm:(u + 1) * tm]
        gate = gate_ref[:, u * tm:(u + 1) * tm]
        weights = (jnp.where(srow == ls[0:1], gate[0:1], 0.0)
                   + jnp.where(srow == ls[1:2], gate[1:2], 0.0)).astype(BF16)
        y = _dot_tn(weights, buf_ref[slot])
        o_ref[rows_u, :] = x1_ref[rows_u, :] + g2_ref[0] * y


def _combine(ys, ls, gate, x1, g2, moves, seq_len):
    t, d = x1.shape
    tm = MOE_TOK_TILE * _tiles_per_step(seq_len // MOE_TOK_TILE, COMBINE_TILES_PER_STEP)
    per_batch = seq_len // tm
    grid_spec = pltpu.PrefetchScalarGridSpec(
        num_scalar_prefetch=1,
        grid=(t // tm,),
        in_specs=[pl.BlockSpec((2, tm), lambda i, *_: (0, i)),
                  pl.BlockSpec((2, tm), lambda i, *_: (0, i)),
                  pl.BlockSpec((tm, d), lambda i, *_: (i, 0)),
                  pl.BlockSpec((1, 1, d), lambda i, *_: (i // per_batch, 0, 0)),
                  pl.BlockSpec(memory_space=pl.ANY)],
        out_specs=pl.BlockSpec((tm, d), lambda i, *_: (i, 0)),
        scratch_shapes=[pltpu.VMEM((COMBINE_AHEAD + 1, MOE_SLOTS, d), BF16),
                        pltpu.SemaphoreType.DMA((COMBINE_AHEAD + 1,))],
    )
    return pl.pallas_call(
        _combine_kernel,
        grid_spec=grid_spec,
        out_shape=jax.ShapeDtypeStruct((t, d), F32),
        compiler_params=_cparams(("arbitrary",)),
        name="moe_combine",
    )(moves, ls, gate, x1, g2.reshape(-1, 1, d), ys)


def kernel(x, c, ada_w, ada_b, norm1_w, norm2_w, w_in, pool_w, pool_scale, hgrn_lb_raw, hgrn_norm_w, q_norm_w,
           k_norm_w, attn_sinks, w_out, router_w, router_bias, expert_w_gate, expert_w_up, expert_w_down):
    b, s, d = x.shape
    depth = ada_w.shape[0]
    t = b * s
    n_rows = _moe_rows(t)

    p = jax.nn.softmax(hgrn_lb_raw.astype(F32), axis=0)
    lower_bounds = jnp.maximum(jnp.cumsum(p, axis=0) - p[0:1], 0.0)

    mod = _modulation(c, ada_w, ada_b)
    hgrn_consts = _hgrn_constants() + (_head_ones(HGRN_WIDTH, HGRN_KDIM),)
    attn_consts = (_attn_bias(), _head_ones(ATT_WIDTH, HEAD_DIM), _head_ones(KV_WIDTH, HEAD_DIM))
    router_consts = _router_constants(MOE_TOK_TILE)
    rwt_b = router_w.T.astype(BF16)
    w_in_b = w_in.astype(BF16)
    w_out_b = w_out.astype(BF16)

    for l in range(depth):
        sh1, sc1, g1, sh2, sc2, g2 = [mod[l, :, j * d:(j + 1) * d] for j in range(6)]
        yp, zh, zq, zkv = _in_proj(x, sc1, sh1, norm1_w[l], w_in_b, l, _block_diag(pool_w[l]).astype(BF16),
                                   pool_scale[l])
        yh = _hgrn_mixer(zh, lower_bounds[l], hgrn_norm_w[l], hgrn_consts)
        ya = _attn_mixer(zq, zkv, q_norm_w[l], k_norm_w[l], attn_sinks[l], attn_consts)
        x1, h2, ls, gate, cnt_out = _out_proj_router(
            yp, yh, ya, x, g1, sc2, sh2, norm2_w[l], w_out_b, l, rwt_b, router_bias, router_consts)
        moves, tail, tile_expert, n_used = _moe_tables(cnt_out)
        xs = _sort_tokens(h2, ls, moves, tail, n_rows)
        ys = _grouped_mlp(xs, tile_expert, n_used, expert_w_gate, expert_w_up, expert_w_down, l)
        x = _combine(ys, ls, gate, x1, g2, moves, s).reshape(b, s, d)
    return x
```

```python
import functools

import numpy as np
import jax
import jax.numpy as jnp
from jax import lax
from jax.experimental import pallas as pl
from jax.experimental.pallas import tpu as pltpu

F32 = jnp.float32
BF16 = jnp.bfloat16

D_MODEL = 1024
POOL_WINDOWS = (2, 4, 8, 16)
POOL_WIDTH = 256
POOL_GROUP = 64
POOL_HALO = 32
HGRN_HEADS = 4
HGRN_KDIM = 64
HGRN_WIDTH = 256
HGRN_CHUNK = 64
ATT_HEADS = 8
ATT_KV_HEADS = 2
HEAD_DIM = 64
ATT_WIDTH = 512
KV_WIDTH = 128
WINDOW = 128
N_EXPERTS = 16
N_GROUPS = 4
EXPERTS_PER_GROUP = 4
D_EXPERT = 512
EPS = 1e-6
MAX_ONE_MINUS_F = 1.0 - 1e-6
LOG2E = 1.4426950408889634
NEG = -1e30

VMEM_LIMIT = 48 * 1024 * 1024

IN_TILE = 1024
ATT_TILE = 1024
MOE_TOK_TILE = 256
ROW_ALIGN = 16
MOE_SLOTS = 768
EXPERT_TILE = 1024
ZERO_ROWS = 512
CHUNK_SIZES = (512, 256, 128, 64, 32, 16)
SORT_TILES_PER_STEP = 4
COMBINE_TILES_PER_STEP = 4
COMBINE_AHEAD = 6
SORT_BUFFERS = 4
MAX_PIECES = MOE_SLOTS // ROW_ALIGN
WAIT_PIECES = (32, 16, 8, 4, 2, 1)


def _sigmoid(x):
    return 1.0 / (1.0 + jnp.exp(-x))


def _silu(x):
    return x * _sigmoid(x)


def _cparams(sem, **kw):
    return pltpu.CompilerParams(dimension_semantics=sem, vmem_limit_bytes=VMEM_LIMIT, **kw)


def _dot(a, b):
    return jnp.dot(a, b, preferred_element_type=F32)


def _dot_nt(a, b):
    return lax.dot_general(a, b, (((1,), (1,)), ((), ())), preferred_element_type=F32)


def _dot_tn(a, b):
    return lax.dot_general(a, b, (((0,), (0,)), ((), ())), preferred_element_type=F32)


def _head_sumsq(x, bd):
    return _dot((x * x).astype(BF16), bd)


def _mod_kernel(c_ref, w_ref, b_ref, o_ref):
    cond = _silu(c_ref[...])
    o_ref[0] = _dot(cond.astype(BF16), w_ref[0].astype(BF16)) + b_ref[0]


def _modulation(c, ada_w, ada_b):
    depth, d, n = ada_w.shape
    b = c.shape[0]
    nb = n // d
    return pl.pallas_call(
        _mod_kernel,
        grid=(depth, nb),
        in_specs=[pl.BlockSpec((b, d), lambda l, j: (0, 0)),
                  pl.BlockSpec((1, d, d), lambda l, j: (l, 0, j)),
                  pl.BlockSpec((1, 1, d), lambda l, j: (l, 0, j))],
        out_specs=pl.BlockSpec((1, b, d), lambda l, j: (l, 0, j)),
        out_shape=jax.ShapeDtypeStruct((depth, b, n), F32),
        compiler_params=_cparams(("parallel", "parallel")),
        name="adaln_mod",
    )(c, ada_w, ada_b.reshape(depth, 1, n))


def _in_kernel(x_ref, sc_ref, sh_ref, nw_ref, w_ref, pw_ref, ps_ref, yp_ref, zh_ref, zq_ref, zkv_ref,
               buf_ref, sa_ref, sb_ref):
    x = x_ref[0]
    ms = jnp.mean(x * x, axis=-1, keepdims=True)
    h = (x * lax.rsqrt(ms + EPS) * nw_ref[...]) * (1.0 + sc_ref[0]) + sh_ref[0]
    z = _dot(h.astype(BF16), w_ref[...])
    zh_ref[0] = z[:, POOL_WIDTH:POOL_WIDTH + 4 * HGRN_WIDTH]
    zq_ref[0] = z[:, POOL_WIDTH + 4 * HGRN_WIDTH:POOL_WIDTH + 4 * HGRN_WIDTH + ATT_WIDTH]
    zkv_ref[0] = z[:, POOL_WIDTH + 4 * HGRN_WIDTH + ATT_WIDTH:]
    yp_ref[0] = _pool(z[:, :POOL_WIDTH], pl.program_id(1), pw_ref, ps_ref, buf_ref, sa_ref, sb_ref).astype(yp_ref.dtype)


def _in_proj(x, sc, sh, nw, w_in_b, layer, pool_bd, pool_scale):
    b, s, d = x.shape
    n = w_in_b.shape[2]
    ts = min(IN_TILE, s)
    tok = lambda w: pl.BlockSpec((1, ts, w), lambda bi, i: (bi, i, 0))
    vec = pl.BlockSpec((1, 1, d), lambda bi, i: (bi, 0, 0))
    full = lambda a: pl.BlockSpec(a.shape, lambda bi, i: (0,) * a.ndim)
    pool_scale = pool_scale.reshape(1, POOL_WIDTH)
    outs = ((POOL_WIDTH, BF16), (4 * HGRN_WIDTH, F32), (ATT_WIDTH, F32), (2 * KV_WIDTH, F32))
    return pl.pallas_call(
        _in_kernel,
        grid=(b, s // ts),
        in_specs=[tok(d), vec, vec,
                  pl.BlockSpec((1, d), lambda bi, i: (0, 0)),
                  pl.BlockSpec((None, d, n), lambda bi, i: (layer, 0, 0)), full(pool_bd), full(pool_scale)],
        out_specs=[tok(w) for w, _ in outs],
        out_shape=[jax.ShapeDtypeStruct((b, s, w), dt) for w, dt in outs],
        scratch_shapes=[pltpu.VMEM((POOL_HALO + ts, POOL_WIDTH), F32)] * 3,
        compiler_params=_cparams(("parallel", "arbitrary")),
        name="in_proj_pool",
    )(x, sc.reshape(b, 1, d), sh.reshape(b, 1, d), nw.reshape(1, d), w_in_b, pool_bd, pool_scale)


def _pool(a, i, w_ref, scale_ref, buf_ref, sa_ref, sb_ref):
    ts = a.shape[0]
    halo = POOL_HALO
    end = halo + ts
    half = 2 * POOL_GROUP

    @pl.when(i == 0)
    def _():
        buf_ref[0:halo, :] = jnp.zeros((halo, POOL_WIDTH), F32)

    buf_ref[halo:end, :] = a
    s2 = buf_ref[8:end, :] + buf_ref[7:end - 1, :]
    sa_ref[8:end, :] = s2
    s4 = sa_ref[16:end, :] + sa_ref[14:end - 2, :]
    sb_ref[16:end, :] = s4
    s8 = sb_ref[24:end, half:] + sb_ref[20:end - 4, half:]
    sa_ref[24:end, half:] = s8
    s16 = sa_ref[halo:end, half:] + sa_ref[halo - 8:end - 8, half:]
    lane = lax.broadcasted_iota(jnp.int32, (1, POOL_WIDTH), 1)
    win = jnp.left_shift(2, lane // POOL_GROUP)
    low = lane[:, 0:half] % half < POOL_GROUP
    acc = jnp.concatenate([jnp.where(low, s2[halo - 8:, 0:half], s4[halo - 16:, 0:half]),
                           jnp.where(low, s8[halo - 24:], s16)], axis=1)
    pos = i * ts + lax.broadcasted_iota(jnp.int32, (ts, 1), 0)
    count = jnp.minimum(pos + 1, win).astype(F32)
    pooled = acc / count - a
    buf_ref[0:halo, :] = a[ts - halo:, :]
    return _dot(pooled.astype(BF16), w_ref[...]) * scale_ref[...]


def _block_diag(blocks):
    g, n, _ = blocks.shape
    eye = jnp.eye(g, dtype=blocks.dtype)
    return (eye[:, None, :, None] * blocks[:, :, None, :]).reshape(g * n, g * n)


def _head_ones(width, head):
    idx = np.arange(width) // head
    return jnp.asarray((idx[:, None] == idx[None, :]).astype(np.float32), dtype=BF16)


HGRN_LEVELS = (32, 16, 8, 4, 2, 1)
HGRN_MATMUL_LEVELS = (2, 1)
N_LEVEL_MASKS = len(HGRN_LEVELS) + 1
HGRN_SEQS = 8
HGRN_TILE = 256
ATT_GROUP = 8
ROUTER_SUBTILES = 4


def _hgrn_constants():
    c = HGRN_CHUNK
    m = np.zeros((len(HGRN_MATMUL_LEVELS) + 1, c, c), np.float32)
    masks = np.zeros((N_LEVEL_MASKS, c, c), np.float32)
    for lvl, n in enumerate(HGRN_LEVELS):
        for t in range(c):
            blk = t // (2 * n)
            mid = blk * 2 * n + n
            if t >= mid:
                masks[lvl, t, blk * 2 * n:mid] = 1.0
            if n in HGRN_MATMUL_LEVELS:
                row = m[HGRN_MATMUL_LEVELS.index(n), t]
                if t >= mid:
                    row[mid:t + 1] = 1.0
                else:
                    row[t + 1:mid] = 1.0
    for t in range(c):
        m[-1, t, :t + 1] = 1.0
        masks[-1, t, t] = 1.0
    m = m.reshape(-1, c)
    m3 = np.concatenate([m, m, m], axis=1)
    masks = np.tile(masks, (1, HGRN_HEADS, 1))
    return jnp.asarray(m3, dtype=BF16), jnp.asarray(masks, dtype=F32)


def _hgrn_chunks(units, bd, m3, lmask_ref):
    c = HGRN_CHUNK
    w = HGRN_WIDTH
    n_fine = len(HGRN_MATMUL_LEVELS)
    lane_head = lax.broadcasted_iota(jnp.int32, (1, w), 1) // HGRN_KDIM
    sums = [_dot(m3, u[2]) for u in units]
    bs = [s[n_fine * c:] for s in sums]

    def level_decay(u, n):
        if n in HGRN_MATMUL_LEVELS:
            k = HGRN_MATMUL_LEVELS.index(n)
            return jnp.exp(sums[u][k * c:(k + 1) * c])
        blocks = bs[u].reshape(c // (2 * n), 2 * n, w)
        ref = blocks[:, n - 1:n, :]
        right = lax.broadcasted_iota(jnp.int32, (1, 2 * n, 1), 1) >= n
        return jnp.exp(jnp.where(right, blocks - ref, ref - blocks).reshape(c, w))

    groups = c // 8
    parts = [[[None] * groups for _ in range(HGRN_HEADS)] for _ in units]

    def add_part(u, h, g, piece):
        parts[u][h][g] = piece if parts[u][h][g] is None else parts[u][h][g] + piece

    for lvl in range(N_LEVEL_MASKS):
        n = HGRN_LEVELS[lvl] if lvl < len(HGRN_LEVELS) else 0
        wanted = [g for g in range(groups) if (8 * g) % (2 * n) >= n] if n >= 8 else list(range(groups))
        for u, (qf, kk, _, _, _, _) in enumerate(units):
            if lvl < len(HGRN_LEVELS):
                e = level_decay(u, n)
                ql = qf * e
                kl = (kk * e).astype(BF16)
            else:
                ql = qf
                kl = kk.astype(BF16)
            ql = jnp.concatenate([ql[8 * g:8 * g + 8] for g in wanted], axis=0).astype(BF16)
            zero = jnp.zeros_like(ql)
            qs = jnp.concatenate([jnp.where(lane_head == h, ql, zero) for h in range(HGRN_HEADS)], axis=0)
            res = _dot_nt(qs, kl)
            for h in range(HGRN_HEADS):
                for k, g in enumerate(wanted):
                    r0 = (h * len(wanted) + k) * 8
                    add_part(u, h, g, res[r0:r0 + 8] * lmask_ref[lvl, h * c + 8 * g:h * c + 8 * g + 8, :])
    scores = [jnp.concatenate([parts[u][h][g] for h in range(HGRN_HEADS) for g in range(groups)], axis=0)
              for u in range(len(units))]
    rs = [_dot(scores[u].astype(BF16), unit[3]) for u, unit in enumerate(units)]
    inters = [_dot_nt((unit[0] * jnp.exp(bs[u])).astype(BF16), unit[5].astype(BF16)) for u, unit in enumerate(units)]
    upds = [_dot_tn(unit[3], (unit[1] * jnp.exp(bs[u][c - 1:c, :] - bs[u])).astype(BF16))
            for u, unit in enumerate(units)]
    outs = []
    for u in range(len(units)):
        o = inters[u]
        for h in range(HGRN_HEADS):
            o = o + jnp.where(lane_head == h, rs[u][h * c:(h + 1) * c], 0.0)
        outs.append(o)
    sss = [_head_sumsq(o, bd) for o in outs]
    res = []
    for u, unit in enumerate(units):
        st = unit[5] * jnp.exp(bs[u][c - 1:c, :]) + jnp.where(bd > 0, upds[u], 0.0)
        y = outs[u] * lax.rsqrt(sss[u] * (1.0 / HGRN_KDIM) + EPS) * unit[4]
        res.append((y, st))
    return res


def _hgrn_kernel(zh_ref, lb_ref, gw_ref, m3_ref, lmask_ref, bd_ref, o_ref, st_ref):
    i = pl.program_id(1)
    nb, ts = zh_ref.shape[0], zh_ref.shape[1]
    c = HGRN_CHUNK
    w = HGRN_WIDTH

    @pl.when(i == 0)
    def _():
        st_ref[...] = jnp.zeros(st_ref.shape, F32)

    one_minus_lb = 1.0 - lb_ref[...]
    gw = gw_ref[...]
    bd = bd_ref[...]
    m3 = m3_ref[...]

    def chunk(ci, carry):
        rows = pl.ds(pl.multiple_of(ci * c, c), c)
        units = []
        for s in range(nb):
            q, f, v, g = [zh_ref[s, rows, j * w:(j + 1) * w] for j in range(4)]
            kk = one_minus_lb * _sigmoid(-f)
            lf = jnp.log(1.0 - jnp.minimum(kk, MAX_ONE_MINUS_F))
            hi = lf.astype(BF16)
            r1 = lf - hi.astype(F32)
            mid = r1.astype(BF16)
            lo = (r1 - mid.astype(F32)).astype(BF16)
            units.append((_silu(q), kk, jnp.concatenate([hi, mid, lo], axis=0), v.astype(BF16), gw * _silu(g),
                          st_ref[s]))
        for s, (y, st) in enumerate(_hgrn_chunks(units, bd, m3, lmask_ref)):
            st_ref[s] = st
            o_ref[s, rows, :] = y.astype(o_ref.dtype)
        return carry

    lax.fori_loop(0, ts // c, chunk, 0)


def _hgrn_mixer(zh, lb, norm_w, consts):
    b, s, _ = zh.shape
    w = HGRN_WIDTH
    ts = min(HGRN_TILE, s)
    nb = HGRN_SEQS if b % HGRN_SEQS == 0 else 1
    m3, lmask, bd = consts
    full = lambda a: pl.BlockSpec(a.shape, lambda bi, i: (0,) * a.ndim)
    return pl.pallas_call(
        _hgrn_kernel,
        grid=(b // nb, s // ts),
        in_specs=[pl.BlockSpec((nb, ts, 4 * w), lambda bi, i: (bi, i, 0)),
                  pl.BlockSpec((1, w), lambda bi, i: (0, 0)),
                  pl.BlockSpec((1, w), lambda bi, i: (0, 0)),
                  full(m3), full(lmask), full(bd)],
        out_specs=pl.BlockSpec((nb, ts, w), lambda bi, i: (bi, i, 0)),
        out_shape=jax.ShapeDtypeStruct((b, s, w), BF16),
        scratch_shapes=[pltpu.VMEM((nb, w, w), F32)],
        compiler_params=_cparams(("parallel", "arbitrary")),
        name="hgrn_mixer",
    )(zh, lb.reshape(1, w), norm_w.reshape(1, w), m3, lmask, bd)


def _attn_bias():
    qi = np.arange(WINDOW)[:, None]
    kj = np.arange(2 * WINDOW)[None, :]
    dist = qi + WINDOW - kj
    valid = (dist >= 0) & (dist < WINDOW)
    slopes = np.exp2(-8.0 * np.arange(1, ATT_HEADS + 1) / ATT_HEADS)
    bias = np.where(valid[None], -slopes[:, None, None] * dist[None] * LOG2E, NEG)
    first = np.where(kj[None] < WINDOW, NEG, bias)
    return jnp.asarray(np.concatenate([bias, first]), dtype=F32)


def _attn_kernel(sink_ref, zq_ref, zkv_ref, qw_ref, kw_ref, bias_ref, bdq_ref, bdk_ref, o_ref,
                 qbuf, kbuf, vbuf):
    i = pl.program_id(1)
    ts = zq_ref.shape[1]
    hw = 2 * HEAD_DIM

    @pl.when(i == 0)
    def _():
        kbuf[:, 0:WINDOW, :] = jnp.zeros((4, WINDOW, hw), BF16)
        vbuf[:, 0:WINDOW, :] = jnp.zeros((4, WINDOW, hw), BF16)

    q = zq_ref[0]
    ssq = _head_sumsq(q, bdq_ref[...])
    qbuf[...] = (q * lax.rsqrt(ssq * (1.0 / HEAD_DIM) + EPS) * (qw_ref[...] * (HEAD_DIM ** -0.5 * LOG2E))).astype(BF16)
    kv = zkv_ref[0]
    k = kv[:, :KV_WIDTH]
    v = kv[:, KV_WIDTH:]
    ssk = _head_sumsq(k, bdk_ref[...])
    kn = k * lax.rsqrt(ssk * (1.0 / HEAD_DIM) + EPS) * kw_ref[...]
    kr = pltpu.roll(kn, HEAD_DIM, 1)
    vr = pltpu.roll(v, HEAD_DIM, 1)
    low = lax.broadcasted_iota(jnp.int32, (1, hw), 1) < HEAD_DIM
    for j in range(ATT_KV_HEADS):
        for half in range(2):
            keep = low if half == 0 else jnp.logical_not(low)
            ksrc = kn if j == half else kr
            vsrc = v if j == half else vr
            kbuf[2 * j + half, WINDOW:WINDOW + ts, :] = jnp.where(keep, ksrc, 0.0).astype(BF16)
            vbuf[2 * j + half, WINDOW:WINDOW + ts, :] = jnp.where(keep, vsrc, 0.0).astype(BF16)

    def block(n, carry):
        r0 = pl.multiple_of(n * WINDOW, WINDOW)
        table = jnp.where(jnp.logical_and(i == 0, n == 0), ATT_HEADS, 0)
        for g0 in range(0, ATT_HEADS, ATT_GROUP):
            heads = range(g0, g0 + ATT_GROUP)
            logits = []
            for h in heads:
                hp, half = h // 2, h % 2
                j = h // (ATT_HEADS // ATT_KV_HEADS)
                qp = qbuf[pl.ds(r0, WINDOW), hp * hw:(hp + 1) * hw]
                keys = kbuf[2 * j + half, pl.ds(r0, 2 * WINDOW), :]
                logits.append(_dot_nt(qp, keys) + bias_ref[table + h])
            ps, scales = [], []
            for h, lg in zip(heads, logits):
                sink = sink_ref[h] * LOG2E
                m = jnp.maximum(jnp.max(lg, axis=-1, keepdims=True), sink)
                p = jnp.exp2(lg - m)
                scales.append(1.0 / (jnp.sum(p, axis=-1, keepdims=True) + jnp.exp2(sink - m)))
                ps.append(p.astype(BF16))
            outs = []
            for h, p in zip(heads, ps):
                half = h % 2
                j = h // (ATT_HEADS // ATT_KV_HEADS)
                vals = vbuf[2 * j + half, pl.ds(r0, 2 * WINDOW), :]
                outs.append(_dot(p, vals))
            for k in range(0, ATT_GROUP, 2):
                hp = (g0 + k) // 2
                acc = outs[k] * scales[k] + outs[k + 1] * scales[k + 1]
                o_ref[0, pl.ds(r0, WINDOW), hp * hw:(hp + 1) * hw] = acc.astype(o_ref.dtype)
        return carry

    lax.fori_loop(0, ts // WINDOW, block, 0)
    kbuf[:, 0:WINDOW, :] = kbuf[:, ts:ts + WINDOW, :]
    vbuf[:, 0:WINDOW, :] = vbuf[:, ts:ts + WINDOW, :]


def _attn_mixer(zq, zkv, q_norm_w, k_norm_w, sinks, consts):
    b, s, _ = zq.shape
    ts = min(ATT_TILE, s)
    bias, bdq, bdk = consts
    hw = 2 * HEAD_DIM
    qw = jnp.tile(q_norm_w, ATT_HEADS).reshape(1, ATT_WIDTH)
    kw = jnp.tile(k_norm_w, ATT_KV_HEADS).reshape(1, KV_WIDTH)
    full = lambda a: pl.BlockSpec(a.shape, lambda bi, i, sk: (0,) * a.ndim)
    grid_spec = pltpu.PrefetchScalarGridSpec(
        num_scalar_prefetch=1,
        grid=(b, s // ts),
        in_specs=[pl.BlockSpec((1, ts, ATT_WIDTH), lambda bi, i, sk: (bi, i, 0)),
                  pl.BlockSpec((1, ts, 2 * KV_WIDTH), lambda bi, i, sk: (bi, i, 0)),
                  full(qw), full(kw), full(bias), full(bdq), full(bdk)],
        out_specs=pl.BlockSpec((1, ts, ATT_WIDTH), lambda bi, i, sk: (bi, i, 0)),
        scratch_shapes=[pltpu.VMEM((ts, ATT_WIDTH), BF16),
                        pltpu.VMEM((4, WINDOW + ts, hw), BF16),
                        pltpu.VMEM((4, WINDOW + ts, hw), BF16)],
    )
    return pl.pallas_call(
        _attn_kernel,
        grid_spec=grid_spec,
        out_shape=jax.ShapeDtypeStruct((b, s, ATT_WIDTH), BF16),
        compiler_params=_cparams(("parallel", "arbitrary")),
        name="swa_mixer",
    )(sinks, zq, zkv, qw, kw, bias, bdq, bdk)


def _router_constants(tm):
    t = np.arange(tm)
    before = (t[:, None] < t[None, :]).astype(np.float32)
    e = np.arange(N_EXPERTS)
    lower = (e[None, :] < e[:, None]).astype(np.float32)
    return (jnp.asarray(before, dtype=BF16), jnp.asarray(np.ones((tm, tm), np.float32), dtype=BF16),
            jnp.asarray(lower, dtype=BF16))


def _route(sel, scores):
    tm = sel.shape[1]
    group_scores = []
    for g in range(N_GROUPS):
        rows = [sel[g * EXPERTS_PER_GROUP + a:g * EXPERTS_PER_GROUP + a + 1] for a in range(EXPERTS_PER_GROUP)]
        best_pair = None
        for a in range(EXPERTS_PER_GROUP):
            for bb in range(a + 1, EXPERTS_PER_GROUP):
                pair = rows[a] + rows[bb]
                best_pair = pair if best_pair is None else jnp.maximum(best_pair, pair)
        group_scores.append(best_pair)
    top = functools.reduce(jnp.maximum, group_scores)
    best = jnp.full((1, tm), N_GROUPS - 1, jnp.int32)
    for g in reversed(range(N_GROUPS - 1)):
        best = jnp.where(group_scores[g] == top, g, best)
    row = lax.broadcasted_iota(jnp.int32, (N_EXPERTS, tm), 0)
    cand = jnp.where(row // EXPERTS_PER_GROUP == best, sel, NEG)
    m1 = jnp.max(cand, axis=0, keepdims=True)
    i1 = jnp.min(jnp.where(cand == m1, row, N_EXPERTS), axis=0, keepdims=True)
    oh1 = row == i1
    cand = jnp.where(oh1, NEG, cand)
    m2 = jnp.max(cand, axis=0, keepdims=True)
    i2 = jnp.min(jnp.where(cand == m2, row, N_EXPERTS), axis=0, keepdims=True)
    oh2 = row == i2
    s1 = jnp.sum(jnp.where(oh1, scores, 0.0), axis=0, keepdims=True)
    s2 = jnp.sum(jnp.where(oh2, scores, 0.0), axis=0, keepdims=True)
    return oh1, oh2, s1 / (s1 + s2), s2 / (s1 + s2)


def _out_kernel(yp_ref, yh_ref, ya_ref, x_ref, g1_ref, sc_ref, sh_ref, nw_ref, wo_ref, rwt_ref, rb_ref,
                before_ref, ones_ref, lower_ref, x1_ref, h2_ref, ls_ref, gate_ref, cnt_ref):
    tm = MOE_TOK_TILE
    subs = [pl.ds(u * tm, tm) for u in range(x_ref.shape[0] // tm)]
    p0, p1 = POOL_WIDTH, POOL_WIDTH + HGRN_WIDTH
    h2s = []
    for rows in subs:
        mix = (_dot(yp_ref[rows, :], wo_ref[0:p0, :]) + _dot(yh_ref[rows, :], wo_ref[p0:p1, :])
               + _dot(ya_ref[rows, :], wo_ref[p1:, :]))
        x1 = x_ref[rows, :] + g1_ref[0] * mix
        x1_ref[rows, :] = x1
        ms = jnp.mean(x1 * x1, axis=-1, keepdims=True)
        h2 = ((x1 * lax.rsqrt(ms + EPS) * nw_ref[...]) * (1.0 + sc_ref[0]) + sh_ref[0]).astype(BF16)
        h2_ref[rows, :] = h2
        h2s.append(h2)
    logits = [_dot_nt(rwt_ref[...], h2) for h2 in h2s]
    picks = []
    for lg in logits:
        ex = jnp.exp(lg - jnp.max(lg, axis=0, keepdims=True))
        scores = ex / jnp.sum(ex, axis=0, keepdims=True)
        picks.append(_route(scores + rb_ref[...], scores))
    chosen = [jnp.where(jnp.logical_or(oh1, oh2), 1.0, 0.0).astype(BF16) for oh1, oh2, _, _ in picks]
    ranks = [_dot(ch, before_ref[...]) for ch in chosen]
    counts = [_dot(ch, ones_ref[...]) for ch in chosen]
    aligned = [(jnp.floor((cn + (ROW_ALIGN - 1)) * (1.0 / ROW_ALIGN)) * ROW_ALIGN).astype(BF16) for cn in counts]
    slots = [_dot(lower_ref[...], al) + rk for al, rk in zip(aligned, ranks)]
    for u, rows in enumerate(subs):
        oh1, oh2, w1, w2 = picks[u]
        gate_ref[:, rows] = jnp.concatenate([w1, w2], axis=0)
        ls_ref[:, rows] = jnp.concatenate([jnp.sum(jnp.where(oh1, slots[u], 0.0), axis=0, keepdims=True),
                                           jnp.sum(jnp.where(oh2, slots[u], 0.0), axis=0, keepdims=True)], axis=0)
        cnt_ref[u] = counts[u][:, 0:128]


def _out_proj_router(yp, yh, ya, x, g1, sc, sh, nw, wo_b, layer, rwt_b, rb, consts):
    b, s, d = x.shape
    t = b * s
    tm = MOE_TOK_TILE
    n_sub = ROUTER_SUBTILES if s % (ROUTER_SUBTILES * tm) == 0 else 1
    ts = n_sub * tm
    per_batch = s // ts
    nt = t // tm
    before, ones, lower = consts
    tok = lambda w: pl.BlockSpec((ts, w), lambda i: (i, 0))
    vec = pl.BlockSpec((1, 1, d), lambda i: (i // per_batch, 0, 0))
    full = lambda a: pl.BlockSpec(a.shape, lambda i: (0,) * a.ndim)
    lanes = pl.BlockSpec((2, ts), lambda i: (0, i))
    return pl.pallas_call(
        _out_kernel,
        grid=(t // ts,),
        in_specs=[tok(POOL_WIDTH), tok(HGRN_WIDTH), tok(ATT_WIDTH), tok(d), vec, vec, vec,
                  pl.BlockSpec((1, d), lambda i: (0, 0)),
                  pl.BlockSpec((None,) + wo_b.shape[1:], lambda i: (layer, 0, 0)), full(rwt_b),
                  pl.BlockSpec((N_EXPERTS, 1), lambda i: (0, 0)), full(before), full(ones), full(lower)],
        out_specs=[tok(d), tok(d), lanes, lanes, pl.BlockSpec((n_sub, N_EXPERTS, 128), lambda i: (i, 0, 0))],
        out_shape=[jax.ShapeDtypeStruct((t, d), F32), jax.ShapeDtypeStruct((t, d), BF16),
                   jax.ShapeDtypeStruct((2, t), F32), jax.ShapeDtypeStruct((2, t), F32),
                   jax.ShapeDtypeStruct((nt, N_EXPERTS, 128), F32)],
        compiler_params=_cparams(("parallel",)),
        name="out_proj_router",
    )(yp.reshape(t, -1), yh.reshape(t, -1), ya.reshape(t, -1), x.reshape(t, d),
      g1.reshape(b, 1, d), sc.reshape(b, 1, d), sh.reshape(b, 1, d), nw.reshape(1, d), wo_b, rwt_b,
      rb.reshape(N_EXPERTS, 1), before, ones, lower)


def _moe_tables(cnt_out):
    cnt = jnp.round(cnt_out[:, :, 0]).astype(jnp.int32)
    cnt = (cnt + ROW_ALIGN - 1) // ROW_ALIGN * ROW_ALIGN
    total = jnp.sum(cnt, axis=0)
    padded = (total + EXPERT_TILE - 1) // EXPERT_TILE * EXPERT_TILE
    ends = jnp.cumsum(padded)
    first = ends - padded
    start = first[None, :] + jnp.cumsum(cnt, axis=0) - cnt
    loff = jnp.cumsum(cnt, axis=1) - cnt
    n_tiles = _moe_rows(cnt.shape[0] * MOE_TOK_TILE) // EXPERT_TILE
    n_used = ends[-1] // EXPERT_TILE
    tile_row = jnp.minimum(jnp.arange(n_tiles), n_used - 1) * EXPERT_TILE
    tile_expert = jnp.sum((ends[None, :] <= tile_row[:, None]).astype(jnp.int32), axis=1)
    pieces = jnp.sum(cnt, axis=1, keepdims=True) // ROW_ALIGN

    def copies(n_per_expert, offset_in_chunk, n_max):
        incl = jnp.cumsum(n_per_expert, axis=1)
        k = jnp.arange(n_max, dtype=jnp.int32)
        owner = jnp.sum((incl[:, None, :] <= k[None, :, None]).astype(jnp.int32), axis=2)
        owner = jnp.minimum(owner, N_EXPERTS - 1)
        is_owner = owner[:, :, None] == jnp.arange(N_EXPERTS, dtype=jnp.int32)[None, None, :]
        pick = lambda a: jnp.sum(jnp.where(is_owner, a[:, None, :], 0), axis=2)
        off = offset_in_chunk(k[None, :] - pick(incl - n_per_expert), pick(cnt))
        return pick(loff) + off, pick(start) + off, incl[:, -1:]

    big_src, big_dst, n_big = copies(cnt // (2 * ROW_ALIGN), lambda j, c: j * (2 * ROW_ALIGN), MAX_PIECES // 2)
    small_src, small_dst, n_small = copies(cnt % (2 * ROW_ALIGN) // ROW_ALIGN, lambda j, c: c - ROW_ALIGN, N_EXPERTS)
    moves = jnp.concatenate([big_src, big_dst, small_src, small_dst, n_big, n_small, pieces], axis=1)
    free_pieces = (n_tiles - n_used) * (EXPERT_TILE // ZERO_ROWS)
    tail = jnp.concatenate([first + total, padded - total, jnp.stack([ends[-1], free_pieces])])
    return (moves.reshape(-1).astype(jnp.int32), tail.astype(jnp.int32), tile_expert.astype(jnp.int32),
            n_used.reshape(1).astype(jnp.int32))


MOVE_BIG_SRC = 0
MOVE_BIG_DST = MOVE_BIG_SRC + MAX_PIECES // 2
MOVE_SMALL_SRC = MOVE_BIG_DST + MAX_PIECES // 2
MOVE_SMALL_DST = MOVE_SMALL_SRC + N_EXPERTS
MOVE_N_BIG = MOVE_SMALL_DST + N_EXPERTS
MOVE_N_SMALL = MOVE_N_BIG + 1
MOVE_PIECES = MOVE_N_SMALL + 1
MOVE_WIDTH = MOVE_PIECES + 1


def _tile_copies(moves_ref, tile, make_copy):
    base = tile * MOVE_WIDTH
    for src0, dst0, n_at, size in ((MOVE_BIG_SRC, MOVE_BIG_DST, MOVE_N_BIG, 2 * ROW_ALIGN),
                                   (MOVE_SMALL_SRC, MOVE_SMALL_DST, MOVE_N_SMALL, ROW_ALIGN)):
        def body(j, carry):
            make_copy(pl.multiple_of(moves_ref[base + src0 + j], ROW_ALIGN),
                      pl.multiple_of(moves_ref[base + dst0 + j], ROW_ALIGN), size).start()
            return carry

        lax.fori_loop(0, moves_ref[base + n_at], body, 0)


def _moe_rows(n_tokens):
    per_tile = 2 * MOE_TOK_TILE + N_EXPERTS * (ROW_ALIGN - 1)
    rows = (n_tokens // MOE_TOK_TILE) * per_tile + N_EXPERTS * (EXPERT_TILE - ROW_ALIGN)
    return (rows + EXPERT_TILE - 1) // EXPERT_TILE * EXPERT_TILE


def _chunk_copies(count, make_copy, wait=False):
    for size in CHUNK_SIZES:
        offset = jnp.bitwise_and(count, ~(2 * size - 1))

        @pl.when(jnp.bitwise_and(count, size) != 0)
        def _():
            copy = make_copy(pl.multiple_of(offset, ROW_ALIGN), size)
            copy.wait() if wait else copy.start()


def _wait_rows(n_pieces, make_copy):
    for p in WAIT_PIECES:
        @pl.when(jnp.bitwise_and(n_pieces, p) != 0)
        def _():
            make_copy(p * ROW_ALIGN).wait()


def _tiles_per_step(n_tiles, most):
    return next(k for k in (most, 2, 1) if n_tiles % k == 0)


def _slot_rows(tm):
    return lax.broadcasted_iota(jnp.int32, (MOE_SLOTS, tm), 0).astype(F32)


def _sort_kernel(moves_ref, tail_ref, h2_ref, ls_ref, xs_ref, buf_ref, sem_ref):
    i = pl.program_id(0)
    n = pl.num_programs(0)
    tm = MOE_TOK_TILE
    n_sub = h2_ref.shape[0] // tm
    n_buf = buf_ref.shape[0]

    def sent(tile):
        s = tile % n_buf
        _wait_rows(moves_ref[tile * MOVE_WIDTH + MOVE_PIECES], lambda rows: pltpu.make_async_copy(
            buf_ref.at[s, pl.ds(0, rows), :], xs_ref.at[pl.ds(0, rows), :], sem_ref.at[s]))

    for u in range(n_sub):
        tile = i * n_sub + u
        slot = tile % n_buf

        @pl.when(tile >= n_buf)
        def _():
            sent(tile - n_buf)

        ls = ls_ref[:, u * tm:(u + 1) * tm]
        srow = _slot_rows(tm)
        perm = jnp.where(jnp.logical_or(srow == ls[0:1], srow == ls[1:2]), 1.0, 0.0).astype(BF16)
        buf_ref[slot] = _dot(perm, h2_ref[u * tm:(u + 1) * tm, :]).astype(BF16)
        _tile_copies(moves_ref, tile, lambda src, dst, size: pltpu.make_async_copy(
            buf_ref.at[slot, pl.ds(src, size), :], xs_ref.at[pl.ds(dst, size), :], sem_ref.at[slot]))

    @pl.when(i == n - 1)
    def _():
        last = n * n_sub - 1
        for back in reversed(range(n_buf)):
            @pl.when(last >= back)
            def _():
                sent(last - back)

        buf_ref[0] = jnp.zeros(buf_ref.shape[1:], BF16)
        zeros_to = lambda row, size: pltpu.make_async_copy(
            buf_ref.at[0, pl.ds(0, size), :], xs_ref.at[pl.ds(row, size), :], sem_ref.at[0])

        for wait in (False, True):
            def expert_tail(e, carry):
                st = pl.multiple_of(tail_ref[e], ROW_ALIGN)
                _chunk_copies(tail_ref[N_EXPERTS + e], lambda off, size: zeros_to(st + off, size), wait)
                return carry

            def free_tile(j, carry):
                copy = zeros_to(pl.multiple_of(tail_ref[2 * N_EXPERTS] + j * ZERO_ROWS, ROW_ALIGN), ZERO_ROWS)
                copy.wait() if wait else copy.start()
                return carry

            lax.fori_loop(0, N_EXPERTS, expert_tail, 0)
            lax.fori_loop(0, tail_ref[2 * N_EXPERTS + 1], free_tile, 0)


def _sort_tokens(h2, ls, moves, tail, n_rows):
    t, d = h2.shape
    tm = MOE_TOK_TILE * _tiles_per_step(t // MOE_TOK_TILE, SORT_TILES_PER_STEP)
    grid_spec = pltpu.PrefetchScalarGridSpec(
        num_scalar_prefetch=2,
        grid=(t // tm,),
        in_specs=[pl.BlockSpec((tm, d), lambda i, *_: (i, 0)),
                  pl.BlockSpec((2, tm), lambda i, *_: (0, i))],
        out_specs=pl.BlockSpec(memory_space=pl.ANY),
        scratch_shapes=[pltpu.VMEM((SORT_BUFFERS, MOE_SLOTS, d), BF16), pltpu.SemaphoreType.DMA((SORT_BUFFERS,))],
    )
    return pl.pallas_call(
        _sort_kernel,
        grid_spec=grid_spec,
        out_shape=jax.ShapeDtypeStruct((n_rows, d), BF16),
        compiler_params=_cparams(("arbitrary",)),
        name="moe_sort",
    )(moves, tail, h2, ls)


def _gmm_kernel(te_ref, nu_ref, xs_ref, wg_ref, wu_ref, wd_ref, ys_ref, wgb, wub, wdb):
    i = pl.program_id(0)
    used = i < nu_ref[0]
    new_expert = jnp.logical_or(i == 0, te_ref[i] != te_ref[jnp.maximum(i - 1, 0)])

    @pl.when(jnp.logical_and(used, new_expert))
    def _():
        wgb[...] = wg_ref[...].astype(BF16)
        wub[...] = wu_ref[...].astype(BF16)
        wdb[...] = wd_ref[...].astype(BF16)

    @pl.when(used)
    def _():
        x = xs_ref[...]
        he = _silu(_dot(x, wgb[...])) * _dot(x, wub[...])
        ys_ref[...] = _dot(he.astype(BF16), wdb[...]).astype(BF16)

    @pl.when(jnp.logical_not(used))
    def _():
        ys_ref[...] = jnp.zeros(ys_ref.shape, BF16)


def _grouped_mlp(xs, tile_expert, n_used, wg, wu, wd, layer):
    n_rows, d = xs.shape
    te = EXPERT_TILE
    row_map = lambda i, tex, nu: (jnp.minimum(i, nu[0] - 1), 0)
    grid_spec = pltpu.PrefetchScalarGridSpec(
        num_scalar_prefetch=2,
        grid=(n_rows // te,),
        in_specs=[pl.BlockSpec((te, d), row_map),
                  pl.BlockSpec((None, None, d, D_EXPERT), lambda i, tex, nu: (layer, tex[i], 0, 0)),
                  pl.BlockSpec((None, None, d, D_EXPERT), lambda i, tex, nu: (layer, tex[i], 0, 0)),
                  pl.BlockSpec((None, None, D_EXPERT, d), lambda i, tex, nu: (layer, tex[i], 0, 0))],
        out_specs=pl.BlockSpec((te, d), lambda i, tex, nu: (i, 0)),
        scratch_shapes=[pltpu.VMEM((d, D_EXPERT), BF16), pltpu.VMEM((d, D_EXPERT), BF16),
                        pltpu.VMEM((D_EXPERT, d), BF16)],
    )
    return pl.pallas_call(
        _gmm_kernel,
        grid_spec=grid_spec,
        out_shape=jax.ShapeDtypeStruct((n_rows, d), BF16),
        compiler_params=_cparams(("arbitrary",)),
        name="moe_grouped_mlp",
    )(tile_expert, n_used, xs, wg, wu, wd)


def _combine_kernel(moves_ref, ls_ref, gate_ref, x1_ref, g2_ref, ys_ref, o_ref, buf_ref, sem_ref):
    i = pl.program_id(0)
    n = pl.num_programs(0)
    tm = MOE_TOK_TILE
    n_sub = x1_ref.shape[0] // tm
    n_tiles = n * n_sub
    n_buf = buf_ref.shape[0]

    def fetch(tile):
        s = tile % n_buf
        _tile_copies(moves_ref, tile, lambda dst, src, size: pltpu.make_async_copy(
            ys_ref.at[pl.ds(src, size), :], buf_ref.at[s, pl.ds(dst, size), :], sem_ref.at[s]))

    @pl.when(i == 0)
    def _():
        buf_ref[...] = jnp.zeros(buf_ref.shape, BF16)
        for first in range(COMBINE_AHEAD):
            @pl.when(first < n_tiles)
            def _():
                fetch(first)

    for u in range(n_sub):
        tile = i * n_sub + u
        slot = tile % n_buf

        @pl.when(tile + COMBINE_AHEAD < n_tiles)
        def _():
            fetch(tile + COMBINE_AHEAD)

        n_pieces = moves_ref[tile * MOVE_WIDTH + MOVE_PIECES]
        _wait_rows(n_pieces, lambda rows: pltpu.make_async_copy(
            ys_ref.at[pl.ds(0, rows), :], buf_ref.at[slot, pl.ds(0, rows), :], sem_ref.at[slot]))

        rows_u = pl.ds(u * tm, tm)
        srow = _slot_rows(tm)
        ls = ls_ref[:, u * tm:(u + 1) * tm]
        gate = gate_ref[:, u * tm:(u + 1) * tm]
        weights = (jnp.where(srow == ls[0:1], gate[0:1], 0.0)
                   + jnp.where(srow == ls[1:2], gate[1:2], 0.0)).astype(BF16)
        y = _dot_tn(weights, buf_ref[slot])
        o_ref[rows_u, :] = x1_ref[rows_u, :] + g2_ref[0] * y


def _combine(ys, ls, gate, x1, g2, moves, seq_len):
    t, d = x1.shape
    tm = MOE_TOK_TILE * _tiles_per_step(seq_len // MOE_TOK_TILE, COMBINE_TILES_PER_STEP)
    per_batch = seq_len // tm
    grid_spec = pltpu.PrefetchScalarGridSpec(
        num_scalar_prefetch=1,
        grid=(t // tm,),
        in_specs=[pl.BlockSpec((2, tm), lambda i, *_: (0, i)),
                  pl.BlockSpec((2, tm), lambda i, *_: (0, i)),
                  pl.BlockSpec((tm, d), lambda i, *_: (i, 0)),
                  pl.BlockSpec((1, 1, d), lambda i, *_: (i // per_batch, 0, 0)),
                  pl.BlockSpec(memory_space=pl.ANY)],
        out_specs=pl.BlockSpec((tm, d), lambda i, *_: (i, 0)),
        scratch_shapes=[pltpu.VMEM((COMBINE_AHEAD + 1, MOE_SLOTS, d), BF16),
                        pltpu.SemaphoreType.DMA((COMBINE_AHEAD + 1,))],
    )
    return pl.pallas_call(
        _combine_kernel,
        grid_spec=grid_spec,
        out_shape=jax.ShapeDtypeStruct((t, d), F32),
        compiler_params=_cparams(("arbitrary",)),
        name="moe_combine",
    )(moves, ls, gate, x1, g2.reshape(-1, 1, d), ys)


def kernel(x, c, ada_w, ada_b, norm1_w, norm2_w, w_in, pool_w, pool_scale, hgrn_lb_raw, hgrn_norm_w, q_norm_w,
           k_norm_w, attn_sinks, w_out, router_w, router_bias, expert_w_gate, expert_w_up, expert_w_down):
    b, s, d = x.shape
    depth = ada_w.shape[0]
    t = b * s
    n_rows = _moe_rows(t)

    p = jax.nn.softmax(hgrn_lb_raw.astype(F32), axis=0)
    lower_bounds = jnp.maximum(jnp.cumsum(p, axis=0) - p[0:1], 0.0)

    mod = _modulation(c, ada_w, ada_b)
    hgrn_consts = _hgrn_constants() + (_head_ones(HGRN_WIDTH, HGRN_KDIM),)
    attn_consts = (_attn_bias(), _head_ones(ATT_WIDTH, HEAD_DIM), _head_ones(KV_WIDTH, HEAD_DIM))
    router_consts = _router_constants(MOE_TOK_TILE)
    rwt_b = router_w.T.astype(BF16)
    w_in_b = w_in.astype(BF16)
    w_out_b = w_out.astype(BF16)

    for l in range(depth):
        sh1, sc1, g1, sh2, sc2, g2 = [mod[l, :, j * d:(j + 1) * d] for j in range(6)]
        yp, zh, zq, zkv = _in_proj(x, sc1, sh1, norm1_w[l], w_in_b, l, _block_diag(pool_w[l]).astype(BF16),
                                   pool_scale[l])
        yh = _hgrn_mixer(zh, lower_bounds[l], hgrn_norm_w[l], hgrn_consts)
        ya = _attn_mixer(zq, zkv, q_norm_w[l], k_norm_w[l], attn_sinks[l], attn_consts)
        x1, h2, ls, gate, cnt_out = _out_proj_router(
            yp, yh, ya, x, g1, sc2, sh2, norm2_w[l], w_out_b, l, rwt_b, router_bias, router_consts)
        moves, tail, tile_expert, n_used = _moe_tables(cnt_out)
        xs = _sort_tokens(h2, ls, moves, tail, n_rows)
        ys = _grouped_mlp(xs, tile_expert, n_used, expert_w_gate, expert_w_up, expert_w_down, l)
        x = _combine(ys, ls, gate, x1, g2, moves, s).reshape(b, s, d)
    return x
```

```python
import functools

import numpy as np
import jax
import jax.numpy as jnp
from jax import lax
from jax.experimental import pallas as pl
from jax.experimental.pallas import tpu as pltpu

F32 = jnp.float32
BF16 = jnp.bfloat16

D_MODEL = 1024
POOL_WINDOWS = (2, 4, 8, 16)
POOL_WIDTH = 256
POOL_GROUP = 64
POOL_HALO = 32
HGRN_HEADS = 4
HGRN_KDIM = 64
HGRN_WIDTH = 256
HGRN_CHUNK = 64
ATT_HEADS = 8
ATT_KV_HEADS = 2
HEAD_DIM = 64
ATT_WIDTH = 512
KV_WIDTH = 128
WINDOW = 128
N_EXPERTS = 16
N_GROUPS = 4
EXPERTS_PER_GROUP = 4
D_EXPERT = 512
EPS = 1e-6
MAX_ONE_MINUS_F = 1.0 - 1e-6
LOG2E = 1.4426950408889634
NEG = -1e30

VMEM_LIMIT = 48 * 1024 * 1024

IN_TILE = 1024
ATT_TILE = 1024
MOE_TOK_TILE = 256
ROW_ALIGN = 16
MOE_SLOTS = 768
EXPERT_TILE = 1024
ZERO_ROWS = 512
CHUNK_SIZES = (512, 256, 128, 64, 32, 16)
SORT_TILES_PER_STEP = 4
COMBINE_TILES_PER_STEP = 2
COMBINE_AHEAD = 3
SORT_BUFFERS = 4
MAX_PIECES = MOE_SLOTS // ROW_ALIGN
WAIT_PIECES = (32, 16, 8, 4, 2, 1)


def _sigmoid(x):
    return 1.0 / (1.0 + jnp.exp(-x))


def _silu(x):
    return x * _sigmoid(x)


def _cparams(sem, **kw):
    return pltpu.CompilerParams(dimension_semantics=sem, vmem_limit_bytes=VMEM_LIMIT, **kw)


def _dot(a, b):
    return jnp.dot(a, b, preferred_element_type=F32)


def _dot_nt(a, b):
    return lax.dot_general(a, b, (((1,), (1,)), ((), ())), preferred_element_type=F32)


def _dot_tn(a, b):
    return lax.dot_general(a, b, (((0,), (0,)), ((), ())), preferred_element_type=F32)


def _head_sumsq(x, bd):
    return _dot((x * x).astype(BF16), bd)


def _mod_kernel(c_ref, w_ref, b_ref, o_ref):
    cond = _silu(c_ref[...])
    o_ref[0] = _dot(cond.astype(BF16), w_ref[0].astype(BF16)) + b_ref[0]


def _modulation(c, ada_w, ada_b):
    depth, d, n = ada_w.shape
    b = c.shape[0]
    nb = n // d
    return pl.pallas_call(
        _mod_kernel,
        grid=(depth, nb),
        in_specs=[pl.BlockSpec((b, d), lambda l, j: (0, 0)),
                  pl.BlockSpec((1, d, d), lambda l, j: (l, 0, j)),
                  pl.BlockSpec((1, 1, d), lambda l, j: (l, 0, j))],
        out_specs=pl.BlockSpec((1, b, d), lambda l, j: (l, 0, j)),
        out_shape=jax.ShapeDtypeStruct((depth, b, n), F32),
        compiler_params=_cparams(("parallel", "parallel")),
        name="adaln_mod",
    )(c, ada_w, ada_b.reshape(depth, 1, n))


def _in_kernel(x_ref, sc_ref, sh_ref, nw_ref, w_ref, pw_ref, ps_ref, yp_ref, zh_ref, zq_ref, zkv_ref,
               buf_ref, sa_ref, sb_ref):
    x = x_ref[0]
    ms = jnp.mean(x * x, axis=-1, keepdims=True)
    h = (x * lax.rsqrt(ms + EPS) * nw_ref[...]) * (1.0 + sc_ref[0]) + sh_ref[0]
    z = _dot(h.astype(BF16), w_ref[...])
    zh_ref[0] = z[:, POOL_WIDTH:POOL_WIDTH + 4 * HGRN_WIDTH]
    zq_ref[0] = z[:, POOL_WIDTH + 4 * HGRN_WIDTH:POOL_WIDTH + 4 * HGRN_WIDTH + ATT_WIDTH]
    zkv_ref[0] = z[:, POOL_WIDTH + 4 * HGRN_WIDTH + ATT_WIDTH:]
    yp_ref[0] = _pool(z[:, :POOL_WIDTH], pl.program_id(1), pw_ref, ps_ref, buf_ref, sa_ref, sb_ref).astype(yp_ref.dtype)


def _in_proj(x, sc, sh, nw, w_in_b, layer, pool_bd, pool_scale):
    b, s, d = x.shape
    n = w_in_b.shape[2]
    ts = min(IN_TILE, s)
    tok = lambda w: pl.BlockSpec((1, ts, w), lambda bi, i: (bi, i, 0))
    vec = pl.BlockSpec((1, 1, d), lambda bi, i: (bi, 0, 0))
    full = lambda a: pl.BlockSpec(a.shape, lambda bi, i: (0,) * a.ndim)
    pool_scale = pool_scale.reshape(1, POOL_WIDTH)
    outs = ((POOL_WIDTH, BF16), (4 * HGRN_WIDTH, F32), (ATT_WIDTH, F32), (2 * KV_WIDTH, F32))
    return pl.pallas_call(
        _in_kernel,
        grid=(b, s // ts),
        in_specs=[tok(d), vec, vec,
                  pl.BlockSpec((1, d), lambda bi, i: (0, 0)),
                  pl.BlockSpec((None, d, n), lambda bi, i: (layer, 0, 0)), full(pool_bd), full(pool_scale)],
        out_specs=[tok(w) for w, _ in outs],
        out_shape=[jax.ShapeDtypeStruct((b, s, w), dt) for w, dt in outs],
        scratch_shapes=[pltpu.VMEM((POOL_HALO + ts, POOL_WIDTH), F32)] * 3,
        compiler_params=_cparams(("parallel", "arbitrary")),
        name="in_proj_pool",
    )(x, sc.reshape(b, 1, d), sh.reshape(b, 1, d), nw.reshape(1, d), w_in_b, pool_bd, pool_scale)


def _pool(a, i, w_ref, scale_ref, buf_ref, sa_ref, sb_ref):
    ts = a.shape[0]
    halo = POOL_HALO
    end = halo + ts
    half = 2 * POOL_GROUP

    @pl.when(i == 0)
    def _():
        buf_ref[0:halo, :] = jnp.zeros((halo, POOL_WIDTH), F32)

    buf_ref[halo:end, :] = a
    s2 = buf_ref[8:end, :] + buf_ref[7:end - 1, :]
    sa_ref[8:end, :] = s2
    s4 = sa_ref[16:end, :] + sa_ref[14:end - 2, :]
    sb_ref[16:end, :] = s4
    s8 = sb_ref[24:end, half:] + sb_ref[20:end - 4, half:]
    sa_ref[24:end, half:] = s8
    s16 = sa_ref[halo:end, half:] + sa_ref[halo - 8:end - 8, half:]
    lane = lax.broadcasted_iota(jnp.int32, (1, POOL_WIDTH), 1)
    win = jnp.left_shift(2, lane // POOL_GROUP)
    low = lane[:, 0:half] % half < POOL_GROUP
    acc = jnp.concatenate([jnp.where(low, s2[halo - 8:, 0:half], s4[halo - 16:, 0:half]),
                           jnp.where(low, s8[halo - 24:], s16)], axis=1)
    pos = i * ts + lax.broadcasted_iota(jnp.int32, (ts, 1), 0)
    count = jnp.minimum(pos + 1, win).astype(F32)
    pooled = acc / count - a
    buf_ref[0:halo, :] = a[ts - halo:, :]
    return _dot(pooled.astype(BF16), w_ref[...]) * scale_ref[...]


def _block_diag(blocks):
    g, n, _ = blocks.shape
    eye = jnp.eye(g, dtype=blocks.dtype)
    return (eye[:, None, :, None] * blocks[:, :, None, :]).reshape(g * n, g * n)


def _head_ones(width, head):
    idx = np.arange(width) // head
    return jnp.asarray((idx[:, None] == idx[None, :]).astype(np.float32), dtype=BF16)


HGRN_LEVELS = (32, 16, 8, 4, 2, 1)
HGRN_MATMUL_LEVELS = (2, 1)
N_LEVEL_MASKS = len(HGRN_LEVELS) + 1
HGRN_SEQS = 8
HGRN_TILE = 256
ATT_GROUP = 8
ROUTER_SUBTILES = 4


def _hgrn_constants():
    c = HGRN_CHUNK
    m = np.zeros((len(HGRN_MATMUL_LEVELS) + 1, c, c), np.float32)
    masks = np.zeros((N_LEVEL_MASKS, c, c), np.float32)
    for lvl, n in enumerate(HGRN_LEVELS):
        for t in range(c):
            blk = t // (2 * n)
            mid = blk * 2 * n + n
            if t >= mid:
                masks[lvl, t, blk * 2 * n:mid] = 1.0
            if n in HGRN_MATMUL_LEVELS:
                row = m[HGRN_MATMUL_LEVELS.index(n), t]
                if t >= mid:
                    row[mid:t + 1] = 1.0
                else:
                    row[t + 1:mid] = 1.0
    for t in range(c):
        m[-1, t, :t + 1] = 1.0
        masks[-1, t, t] = 1.0
    m = m.reshape(-1, c)
    m3 = np.concatenate([m, m, m], axis=1)
    masks = np.tile(masks, (1, HGRN_HEADS, 1))
    return jnp.asarray(m3, dtype=BF16), jnp.asarray(masks, dtype=F32)


def _hgrn_chunks(units, bd, m3, lmask_ref):
    c = HGRN_CHUNK
    w = HGRN_WIDTH
    n_fine = len(HGRN_MATMUL_LEVELS)
    lane_head = lax.broadcasted_iota(jnp.int32, (1, w), 1) // HGRN_KDIM
    sums = [_dot(m3, u[2]) for u in units]
    bs = [s[n_fine * c:] for s in sums]

    def level_decay(u, n):
        if n in HGRN_MATMUL_LEVELS:
            k = HGRN_MATMUL_LEVELS.index(n)
            return jnp.exp(sums[u][k * c:(k + 1) * c])
        blocks = bs[u].reshape(c // (2 * n), 2 * n, w)
        ref = blocks[:, n - 1:n, :]
        right = lax.broadcasted_iota(jnp.int32, (1, 2 * n, 1), 1) >= n
        return jnp.exp(jnp.where(right, blocks - ref, ref - blocks).reshape(c, w))

    groups = c // 8
    parts = [[[None] * groups for _ in range(HGRN_HEADS)] for _ in units]

    def add_part(u, h, g, piece):
        parts[u][h][g] = piece if parts[u][h][g] is None else parts[u][h][g] + piece

    for lvl in range(N_LEVEL_MASKS):
        n = HGRN_LEVELS[lvl] if lvl < len(HGRN_LEVELS) else 0
        wanted = [g for g in range(groups) if (8 * g) % (2 * n) >= n] if n >= 8 else list(range(groups))
        for u, (qf, kk, _, _, _, _) in enumerate(units):
            if lvl < len(HGRN_LEVELS):
                e = level_decay(u, n)
                ql = qf * e
                kl = (kk * e).astype(BF16)
            else:
                ql = qf
                kl = kk.astype(BF16)
            ql = jnp.concatenate([ql[8 * g:8 * g + 8] for g in wanted], axis=0).astype(BF16)
            zero = jnp.zeros_like(ql)
            qs = jnp.concatenate([jnp.where(lane_head == h, ql, zero) for h in range(HGRN_HEADS)], axis=0)
            res = _dot_nt(qs, kl)
            for h in range(HGRN_HEADS):
                for k, g in enumerate(wanted):
                    r0 = (h * len(wanted) + k) * 8
                    add_part(u, h, g, res[r0:r0 + 8] * lmask_ref[lvl, h * c + 8 * g:h * c + 8 * g + 8, :])
    scores = [jnp.concatenate([parts[u][h][g] for h in range(HGRN_HEADS) for g in range(groups)], axis=0)
              for u in range(len(units))]
    rs = []
    for u, unit in enumerate(units):
        wide = jnp.concatenate([scores[u][h * c:(h + 1) * c] for h in range(HGRN_HEADS)], axis=1).astype(BF16)
        zero = jnp.zeros_like(unit[3])
        vbd = jnp.concatenate([jnp.where(lane_head == h, unit[3], zero) for h in range(HGRN_HEADS)], axis=0)
        rs.append(_dot(wide, vbd))
    inters = [_dot_nt((unit[0] * jnp.exp(bs[u])).astype(BF16), unit[5].astype(BF16)) for u, unit in enumerate(units)]
    upds = [_dot_tn(unit[3], (unit[1] * jnp.exp(bs[u][c - 1:c, :] - bs[u])).astype(BF16))
            for u, unit in enumerate(units)]
    outs = []
    for u in range(len(units)):
        outs.append(inters[u] + rs[u])
    sss = [_head_sumsq(o, bd) for o in outs]
    res = []
    for u, unit in enumerate(units):
        st = unit[5] * jnp.exp(bs[u][c - 1:c, :]) + jnp.where(bd > 0, upds[u], 0.0)
        y = outs[u] * lax.rsqrt(sss[u] * (1.0 / HGRN_KDIM) + EPS) * unit[4]
        res.append((y, st))
    return res


def _hgrn_kernel(zh_ref, lb_ref, gw_ref, m3_ref, lmask_ref, bd_ref, o_ref, st_ref):
    i = pl.program_id(1)
    nb, ts = zh_ref.shape[0], zh_ref.shape[1]
    c = HGRN_CHUNK
    w = HGRN_WIDTH

    @pl.when(i == 0)
    def _():
        st_ref[...] = jnp.zeros(st_ref.shape, F32)

    one_minus_lb = 1.0 - lb_ref[...]
    gw = gw_ref[...]
    bd = bd_ref[...]
    m3 = m3_ref[...]

    def chunk(ci, carry):
        rows = pl.ds(pl.multiple_of(ci * c, c), c)
        units = []
        for s in range(nb):
            q, f, v, g = [zh_ref[s, rows, j * w:(j + 1) * w] for j in range(4)]
            kk = one_minus_lb * _sigmoid(-f)
            lf = jnp.log(1.0 - jnp.minimum(kk, MAX_ONE_MINUS_F))
            hi = lf.astype(BF16)
            r1 = lf - hi.astype(F32)
            mid = r1.astype(BF16)
            lo = (r1 - mid.astype(F32)).astype(BF16)
            units.append((_silu(q), kk, jnp.concatenate([hi, mid, lo], axis=0), v.astype(BF16), gw * _silu(g),
                          st_ref[s]))
        for s, (y, st) in enumerate(_hgrn_chunks(units, bd, m3, lmask_ref)):
            st_ref[s] = st
            o_ref[s, rows, :] = y.astype(o_ref.dtype)
        return carry

    lax.fori_loop(0, ts // c, chunk, 0)


def _hgrn_mixer(zh, lb, norm_w, consts):
    b, s, _ = zh.shape
    w = HGRN_WIDTH
    ts = min(HGRN_TILE, s)
    nb = HGRN_SEQS if b % HGRN_SEQS == 0 else 1
    m3, lmask, bd = consts
    full = lambda a: pl.BlockSpec(a.shape, lambda bi, i: (0,) * a.ndim)
    return pl.pallas_call(
        _hgrn_kernel,
        grid=(b // nb, s // ts),
        in_specs=[pl.BlockSpec((nb, ts, 4 * w), lambda bi, i: (bi, i, 0)),
                  pl.BlockSpec((1, w), lambda bi, i: (0, 0)),
                  pl.BlockSpec((1, w), lambda bi, i: (0, 0)),
                  full(m3), full(lmask), full(bd)],
        out_specs=pl.BlockSpec((nb, ts, w), lambda bi, i: (bi, i, 0)),
        out_shape=jax.ShapeDtypeStruct((b, s, w), BF16),
        scratch_shapes=[pltpu.VMEM((nb, w, w), F32)],
        compiler_params=_cparams(("parallel", "arbitrary")),
        name="hgrn_mixer",
    )(zh, lb.reshape(1, w), norm_w.reshape(1, w), m3, lmask, bd)


def _attn_bias():
    qi = np.arange(WINDOW)[:, None]
    kj = np.arange(2 * WINDOW)[None, :]
    dist = qi + WINDOW - kj
    valid = (dist >= 0) & (dist < WINDOW)
    slopes = np.exp2(-8.0 * np.arange(1, ATT_HEADS + 1) / ATT_HEADS)
    bias = np.where(valid[None], -slopes[:, None, None] * dist[None] * LOG2E, NEG)
    first = np.where(kj[None] < WINDOW, NEG, bias)
    return jnp.asarray(np.concatenate([bias, first]), dtype=F32)


def _attn_kernel(sink_ref, zq_ref, zkv_ref, qw_ref, kw_ref, bias_ref, bdq_ref, bdk_ref, o_ref,
                 qbuf, kbuf, vbuf):
    i = pl.program_id(1)
    ts = zq_ref.shape[1]
    hw = 2 * HEAD_DIM

    @pl.when(i == 0)
    def _():
        kbuf[:, 0:WINDOW, :] = jnp.zeros((4, WINDOW, hw), BF16)
        vbuf[:, 0:WINDOW, :] = jnp.zeros((4, WINDOW, hw), BF16)

    q = zq_ref[0]
    ssq = _head_sumsq(q, bdq_ref[...])
    qbuf[...] = (q * lax.rsqrt(ssq * (1.0 / HEAD_DIM) + EPS) * (qw_ref[...] * (HEAD_DIM ** -0.5 * LOG2E))).astype(BF16)
    kv = zkv_ref[0]
    k = kv[:, :KV_WIDTH]
    v = kv[:, KV_WIDTH:]
    ssk = _head_sumsq(k, bdk_ref[...])
    kn = k * lax.rsqrt(ssk * (1.0 / HEAD_DIM) + EPS) * kw_ref[...]
    kr = pltpu.roll(kn, HEAD_DIM, 1)
    vr = pltpu.roll(v, HEAD_DIM, 1)
    low = lax.broadcasted_iota(jnp.int32, (1, hw), 1) < HEAD_DIM
    for j in range(ATT_KV_HEADS):
        for half in range(2):
            keep = low if half == 0 else jnp.logical_not(low)
            ksrc = kn if j == half else kr
            vsrc = v if j == half else vr
            kbuf[2 * j + half, WINDOW:WINDOW + ts, :] = jnp.where(keep, ksrc, 0.0).astype(BF16)
            vbuf[2 * j + half, WINDOW:WINDOW + ts, :] = jnp.where(keep, vsrc, 0.0).astype(BF16)

    def block(n, carry):
        r0 = pl.multiple_of(n * WINDOW, WINDOW)
        table = jnp.where(jnp.logical_and(i == 0, n == 0), ATT_HEADS, 0)
        for g0 in range(0, ATT_HEADS, ATT_GROUP):
            heads = range(g0, g0 + ATT_GROUP)
            logits = []
            for h in heads:
                hp, half = h // 2, h % 2
                j = h // (ATT_HEADS // ATT_KV_HEADS)
                qp = qbuf[pl.ds(r0, WINDOW), hp * hw:(hp + 1) * hw]
                keys = kbuf[2 * j + half, pl.ds(r0, 2 * WINDOW), :]
                logits.append(_dot_nt(qp, keys) + bias_ref[table + h])
            ps, scales = [], []
            for h, lg in zip(heads, logits):
                sink = sink_ref[h] * LOG2E
                m = jnp.maximum(jnp.max(lg, axis=-1, keepdims=True), sink)
                p = jnp.exp2(lg - m)
                scales.append(1.0 / (jnp.sum(p, axis=-1, keepdims=True) + jnp.exp2(sink - m)))
                ps.append(p.astype(BF16))
            outs = []
            for h, p in zip(heads, ps):
                half = h % 2
                j = h // (ATT_HEADS // ATT_KV_HEADS)
                vals = vbuf[2 * j + half, pl.ds(r0, 2 * WINDOW), :]
                outs.append(_dot(p, vals))
            for k in range(0, ATT_GROUP, 2):
                hp = (g0 + k) // 2
                acc = outs[k] * scales[k] + outs[k + 1] * scales[k + 1]
                o_ref[0, pl.ds(r0, WINDOW), hp * hw:(hp + 1) * hw] = acc.astype(o_ref.dtype)
        return carry

    lax.fori_loop(0, ts // WINDOW, block, 0)
    kbuf[:, 0:WINDOW, :] = kbuf[:, ts:ts + WINDOW, :]
    vbuf[:, 0:WINDOW, :] = vbuf[:, ts:ts + WINDOW, :]


def _attn_mixer(zq, zkv, q_norm_w, k_norm_w, sinks, consts):
    b, s, _ = zq.shape
    ts = min(ATT_TILE, s)
    bias, bdq, bdk = consts
    hw = 2 * HEAD_DIM
    qw = jnp.tile(q_norm_w, ATT_HEADS).reshape(1, ATT_WIDTH)
    kw = jnp.tile(k_norm_w, ATT_KV_HEADS).reshape(1, KV_WIDTH)
    full = lambda a: pl.BlockSpec(a.shape, lambda bi, i, sk: (0,) * a.ndim)
    grid_spec = pltpu.PrefetchScalarGridSpec(
        num_scalar_prefetch=1,
        grid=(b, s // ts),
        in_specs=[pl.BlockSpec((1, ts, ATT_WIDTH), lambda bi, i, sk: (bi, i, 0)),
                  pl.BlockSpec((1, ts, 2 * KV_WIDTH), lambda bi, i, sk: (bi, i, 0)),
                  full(qw), full(kw), full(bias), full(bdq), full(bdk)],
        out_specs=pl.BlockSpec((1, ts, ATT_WIDTH), lambda bi, i, sk: (bi, i, 0)),
        scratch_shapes=[pltpu.VMEM((ts, ATT_WIDTH), BF16),
                        pltpu.VMEM((4, WINDOW + ts, hw), BF16),
                        pltpu.VMEM((4, WINDOW + ts, hw), BF16)],
    )
    return pl.pallas_call(
        _attn_kernel,
        grid_spec=grid_spec,
        out_shape=jax.ShapeDtypeStruct((b, s, ATT_WIDTH), BF16),
        compiler_params=_cparams(("parallel", "arbitrary")),
        name="swa_mixer",
    )(sinks, zq, zkv, qw, kw, bias, bdq, bdk)


def _router_constants(tm):
    t = np.arange(tm)
    before = (t[:, None] < t[None, :]).astype(np.float32)
    e = np.arange(N_EXPERTS)
    lower = (e[None, :] < e[:, None]).astype(np.float32)
    return (jnp.asarray(before, dtype=BF16), jnp.asarray(np.ones((tm, tm), np.float32), dtype=BF16),
            jnp.asarray(lower, dtype=BF16))


def _route(sel, scores):
    tm = sel.shape[1]
    group_scores = []
    for g in range(N_GROUPS):
        rows = [sel[g * EXPERTS_PER_GROUP + a:g * EXPERTS_PER_GROUP + a + 1] for a in range(EXPERTS_PER_GROUP)]
        best_pair = None
        for a in range(EXPERTS_PER_GROUP):
            for bb in range(a + 1, EXPERTS_PER_GROUP):
                pair = rows[a] + rows[bb]
                best_pair = pair if best_pair is None else jnp.maximum(best_pair, pair)
        group_scores.append(best_pair)
    top = functools.reduce(jnp.maximum, group_scores)
    best = jnp.full((1, tm), N_GROUPS - 1, jnp.int32)
    for g in reversed(range(N_GROUPS - 1)):
        best = jnp.where(group_scores[g] == top, g, best)
    row = lax.broadcasted_iota(jnp.int32, (N_EXPERTS, tm), 0)
    cand = jnp.where(row // EXPERTS_PER_GROUP == best, sel, NEG)
    m1 = jnp.max(cand, axis=0, keepdims=True)
    i1 = jnp.min(jnp.where(cand == m1, row, N_EXPERTS), axis=0, keepdims=True)
    oh1 = row == i1
    cand = jnp.where(oh1, NEG, cand)
    m2 = jnp.max(cand, axis=0, keepdims=True)
    i2 = jnp.min(jnp.where(cand == m2, row, N_EXPERTS), axis=0, keepdims=True)
    oh2 = row == i2
    s1 = jnp.sum(jnp.where(oh1, scores, 0.0), axis=0, keepdims=True)
    s2 = jnp.sum(jnp.where(oh2, scores, 0.0), axis=0, keepdims=True)
    return oh1, oh2, s1 / (s1 + s2), s2 / (s1 + s2)


def _out_kernel(yp_ref, yh_ref, ya_ref, x_ref, g1_ref, sc_ref, sh_ref, nw_ref, wo_ref, rwt_ref, rb_ref,
                before_ref, ones_ref, lower_ref, x1_ref, h2_ref, ls_ref, gate_ref, cnt_ref):
    tm = MOE_TOK_TILE
    subs = [pl.ds(u * tm, tm) for u in range(x_ref.shape[0] // tm)]
    p0, p1 = POOL_WIDTH, POOL_WIDTH + HGRN_WIDTH
    h2s = []
    for rows in subs:
        mix = (_dot(yp_ref[rows, :], wo_ref[0:p0, :]) + _dot(yh_ref[rows, :], wo_ref[p0:p1, :])
               + _dot(ya_ref[rows, :], wo_ref[p1:, :]))
        x1 = x_ref[rows, :] + g1_ref[0] * mix
        x1_ref[rows, :] = x1
        ms = jnp.mean(x1 * x1, axis=-1, keepdims=True)
        h2 = ((x1 * lax.rsqrt(ms + EPS) * nw_ref[...]) * (1.0 + sc_ref[0]) + sh_ref[0]).astype(BF16)
        h2_ref[rows, :] = h2
        h2s.append(h2)
    logits = [_dot_nt(rwt_ref[...], h2) for h2 in h2s]
    picks = []
    for lg in logits:
        ex = jnp.exp(lg - jnp.max(lg, axis=0, keepdims=True))
        scores = ex / jnp.sum(ex, axis=0, keepdims=True)
        picks.append(_route(scores + rb_ref[...], scores))
    chosen = [jnp.where(jnp.logical_or(oh1, oh2), 1.0, 0.0).astype(BF16) for oh1, oh2, _, _ in picks]
    ranks = [_dot(ch, before_ref[...]) for ch in chosen]
    counts = [_dot(ch, ones_ref[...]) for ch in chosen]
    aligned = [(jnp.floor((cn + (ROW_ALIGN - 1)) * (1.0 / ROW_ALIGN)) * ROW_ALIGN).astype(BF16) for cn in counts]
    slots = [_dot(lower_ref[...], al) + rk for al, rk in zip(aligned, ranks)]
    for u, rows in enumerate(subs):
        oh1, oh2, w1, w2 = picks[u]
        gate_ref[:, rows] = jnp.concatenate([w1, w2], axis=0)
        ls_ref[:, rows] = jnp.concatenate([jnp.sum(jnp.where(oh1, slots[u], 0.0), axis=0, keepdims=True),
                                           jnp.sum(jnp.where(oh2, slots[u], 0.0), axis=0, keepdims=True)], axis=0)
        cnt_ref[u] = counts[u][:, 0:128]


def _out_proj_router(yp, yh, ya, x, g1, sc, sh, nw, wo_b, layer, rwt_b, rb, consts):
    b, s, d = x.shape
    t = b * s
    tm = MOE_TOK_TILE
    n_sub = ROUTER_SUBTILES if s % (ROUTER_SUBTILES * tm) == 0 else 1
    ts = n_sub * tm
    per_batch = s // ts
    nt = t // tm
    before, ones, lower = consts
    tok = lambda w: pl.BlockSpec((ts, w), lambda i: (i, 0))
    vec = pl.BlockSpec((1, 1, d), lambda i: (i // per_batch, 0, 0))
    full = lambda a: pl.BlockSpec(a.shape, lambda i: (0,) * a.ndim)
    lanes = pl.BlockSpec((2, ts), lambda i: (0, i))
    return pl.pallas_call(
        _out_kernel,
        grid=(t // ts,),
        in_specs=[tok(POOL_WIDTH), tok(HGRN_WIDTH), tok(ATT_WIDTH), tok(d), vec, vec, vec,
                  pl.BlockSpec((1, d), lambda i: (0, 0)),
                  pl.BlockSpec((None,) + wo_b.shape[1:], lambda i: (layer, 0, 0)), full(rwt_b),
                  pl.BlockSpec((N_EXPERTS, 1), lambda i: (0, 0)), full(before), full(ones), full(lower)],
        out_specs=[tok(d), tok(d), lanes, lanes, pl.BlockSpec((n_sub, N_EXPERTS, 128), lambda i: (i, 0, 0))],
        out_shape=[jax.ShapeDtypeStruct((t, d), F32), jax.ShapeDtypeStruct((t, d), BF16),
                   jax.ShapeDtypeStruct((2, t), F32), jax.ShapeDtypeStruct((2, t), F32),
                   jax.ShapeDtypeStruct((nt, N_EXPERTS, 128), F32)],
        compiler_params=_cparams(("parallel",)),
        name="out_proj_router",
    )(yp.reshape(t, -1), yh.reshape(t, -1), ya.reshape(t, -1), x.reshape(t, d),
      g1.reshape(b, 1, d), sc.reshape(b, 1, d), sh.reshape(b, 1, d), nw.reshape(1, d), wo_b, rwt_b,
      rb.reshape(N_EXPERTS, 1), before, ones, lower)


def _moe_tables(cnt_out):
    cnt = jnp.round(cnt_out[:, :, 0]).astype(jnp.int32)
    cnt = (cnt + ROW_ALIGN - 1) // ROW_ALIGN * ROW_ALIGN
    total = jnp.sum(cnt, axis=0)
    padded = (total + EXPERT_TILE - 1) // EXPERT_TILE * EXPERT_TILE
    ends = jnp.cumsum(padded)
    first = ends - padded
    start = first[None, :] + jnp.cumsum(cnt, axis=0) - cnt
    loff = jnp.cumsum(cnt, axis=1) - cnt
    n_tiles = _moe_rows(cnt.shape[0] * MOE_TOK_TILE) // EXPERT_TILE
    n_used = ends[-1] // EXPERT_TILE
    tile_row = jnp.minimum(jnp.arange(n_tiles), n_used - 1) * EXPERT_TILE
    tile_expert = jnp.sum((ends[None, :] <= tile_row[:, None]).astype(jnp.int32), axis=1)
    pieces = jnp.sum(cnt, axis=1, keepdims=True) // ROW_ALIGN

    def copies(n_per_expert, offset_in_chunk, n_max):
        incl = jnp.cumsum(n_per_expert, axis=1)
        k = jnp.arange(n_max, dtype=jnp.int32)
        owner = jnp.sum((incl[:, None, :] <= k[None, :, None]).astype(jnp.int32), axis=2)
        owner = jnp.minimum(owner, N_EXPERTS - 1)
        is_owner = owner[:, :, None] == jnp.arange(N_EXPERTS, dtype=jnp.int32)[None, None, :]
        pick = lambda a: jnp.sum(jnp.where(is_owner, a[:, None, :], 0), axis=2)
        off = offset_in_chunk(k[None, :] - pick(incl - n_per_expert), pick(cnt))
        return pick(loff) + off, pick(start) + off, incl[:, -1:]

    big_src, big_dst, n_big = copies(cnt // (2 * ROW_ALIGN), lambda j, c: j * (2 * ROW_ALIGN), MAX_PIECES // 2)
    small_src, small_dst, n_small = copies(cnt % (2 * ROW_ALIGN) // ROW_ALIGN, lambda j, c: c - ROW_ALIGN, N_EXPERTS)
    moves = jnp.concatenate([big_src, big_dst, small_src, small_dst, n_big, n_small, pieces], axis=1)
    free_pieces = (n_tiles - n_used) * (EXPERT_TILE // ZERO_ROWS)
    tail = jnp.concatenate([first + total, padded - total, jnp.stack([ends[-1], free_pieces])])
    return (moves.reshape(-1).astype(jnp.int32), tail.astype(jnp.int32), tile_expert.astype(jnp.int32),
            n_used.reshape(1).astype(jnp.int32))


MOVE_BIG_SRC = 0
MOVE_BIG_DST = MOVE_BIG_SRC + MAX_PIECES // 2
MOVE_SMALL_SRC = MOVE_BIG_DST + MAX_PIECES // 2
MOVE_SMALL_DST = MOVE_SMALL_SRC + N_EXPERTS
MOVE_N_BIG = MOVE_SMALL_DST + N_EXPERTS
MOVE_N_SMALL = MOVE_N_BIG + 1
MOVE_PIECES = MOVE_N_SMALL + 1
MOVE_WIDTH = MOVE_PIECES + 1


def _tile_copies(moves_ref, tile, make_copy):
    base = tile * MOVE_WIDTH
    for src0, dst0, n_at, size in ((MOVE_BIG_SRC, MOVE_BIG_DST, MOVE_N_BIG, 2 * ROW_ALIGN),
                                   (MOVE_SMALL_SRC, MOVE_SMALL_DST, MOVE_N_SMALL, ROW_ALIGN)):
        def body(j, carry):
            make_copy(pl.multiple_of(moves_ref[base + src0 + j], ROW_ALIGN),
                      pl.multiple_of(moves_ref[base + dst0 + j], ROW_ALIGN), size).start()
            return carry

        lax.fori_loop(0, moves_ref[base + n_at], body, 0)


def _moe_rows(n_tokens):
    per_tile = 2 * MOE_TOK_TILE + N_EXPERTS * (ROW_ALIGN - 1)
    rows = (n_tokens // MOE_TOK_TILE) * per_tile + N_EXPERTS * (EXPERT_TILE - ROW_ALIGN)
    return (rows + EXPERT_TILE - 1) // EXPERT_TILE * EXPERT_TILE


def _chunk_copies(count, make_copy, wait=False):
    for size in CHUNK_SIZES:
        offset = jnp.bitwise_and(count, ~(2 * size - 1))

        @pl.when(jnp.bitwise_and(count, size) != 0)
        def _():
            copy = make_copy(pl.multiple_of(offset, ROW_ALIGN), size)
            copy.wait() if wait else copy.start()


def _wait_rows(n_pieces, make_copy):
    for p in WAIT_PIECES:
        @pl.when(jnp.bitwise_and(n_pieces, p) != 0)
        def _():
            make_copy(p * ROW_ALIGN).wait()


def _tiles_per_step(n_tiles, most):
    return next(k for k in (most, 2, 1) if n_tiles % k == 0)


def _slot_rows(tm):
    return lax.broadcasted_iota(jnp.int32, (MOE_SLOTS, tm), 0).astype(F32)


def _sort_kernel(moves_ref, tail_ref, h2_ref, ls_ref, xs_ref, buf_ref, sem_ref):
    i = pl.program_id(0)
    n = pl.num_programs(0)
    tm = MOE_TOK_TILE
    n_sub = h2_ref.shape[0] // tm
    n_buf = buf_ref.shape[0]

    def sent(tile):
        s = tile % n_buf
        _wait_rows(moves_ref[tile * MOVE_WIDTH + MOVE_PIECES], lambda rows: pltpu.make_async_copy(
            buf_ref.at[s, pl.ds(0, rows), :], xs_ref.at[pl.ds(0, rows), :], sem_ref.at[s]))

    for u in range(n_sub):
        tile = i * n_sub + u
        slot = tile % n_buf

        @pl.when(tile >= n_buf)
        def _():
            sent(tile - n_buf)

        ls = ls_ref[:, u * tm:(u + 1) * tm]
        srow = _slot_rows(tm)
        perm = jnp.where(jnp.logical_or(srow == ls[0:1], srow == ls[1:2]), 1.0, 0.0).astype(BF16)
        buf_ref[slot] = _dot(perm, h2_ref[u * tm:(u + 1) * tm, :]).astype(BF16)
        _tile_copies(moves_ref, tile, lambda src, dst, size: pltpu.make_async_copy(
            buf_ref.at[slot, pl.ds(src, size), :], xs_ref.at[pl.ds(dst, size), :], sem_ref.at[slot]))

    @pl.when(i == n - 1)
    def _():
        last = n * n_sub - 1
        for back in reversed(range(n_buf)):
            @pl.when(last >= back)
            def _():
                sent(last - back)

        buf_ref[0] = jnp.zeros(buf_ref.shape[1:], BF16)
        zeros_to = lambda row, size: pltpu.make_async_copy(
            buf_ref.at[0, pl.ds(0, size), :], xs_ref.at[pl.ds(row, size), :], sem_ref.at[0])

        for wait in (False, True):
            def expert_tail(e, carry):
                st = pl.multiple_of(tail_ref[e], ROW_ALIGN)
                _chunk_copies(tail_ref[N_EXPERTS + e], lambda off, size: zeros_to(st + off, size), wait)
                return carry

            def free_tile(j, carry):
                copy = zeros_to(pl.multiple_of(tail_ref[2 * N_EXPERTS] + j * ZERO_ROWS, ROW_ALIGN), ZERO_ROWS)
                copy.wait() if wait else copy.start()
                return carry

            lax.fori_loop(0, N_EXPERTS, expert_tail, 0)
            lax.fori_loop(0, tail_ref[2 * N_EXPERTS + 1], free_tile, 0)


def _sort_tokens(h2, ls, moves, tail, n_rows):
    t, d = h2.shape
    tm = MOE_TOK_TILE * _tiles_per_step(t // MOE_TOK_TILE, SORT_TILES_PER_STEP)
    grid_spec = pltpu.PrefetchScalarGridSpec(
        num_scalar_prefetch=2,
        grid=(t // tm,),
        in_specs=[pl.BlockSpec((tm, d), lambda i, *_: (i, 0)),
                  pl.BlockSpec((2, tm), lambda i, *_: (0, i))],
        out_specs=pl.BlockSpec(memory_space=pl.ANY),
        scratch_shapes=[pltpu.VMEM((SORT_BUFFERS, MOE_SLOTS, d), BF16), pltpu.SemaphoreType.DMA((SORT_BUFFERS,))],
    )
    return pl.pallas_call(
        _sort_kernel,
        grid_spec=grid_spec,
        out_shape=jax.ShapeDtypeStruct((n_rows, d), BF16),
        compiler_params=_cparams(("arbitrary",)),
        name="moe_sort",
    )(moves, tail, h2, ls)


def _gmm_kernel(te_ref, nu_ref, xs_ref, wg_ref, wu_ref, wd_ref, ys_ref, wgb, wub, wdb):
    i = pl.program_id(0)
    used = i < nu_ref[0]
    new_expert = jnp.logical_or(i == 0, te_ref[i] != te_ref[jnp.maximum(i - 1, 0)])

    @pl.when(jnp.logical_and(used, new_expert))
    def _():
        wgb[...] = wg_ref[...].astype(BF16)
        wub[...] = wu_ref[...].astype(BF16)
        wdb[...] = wd_ref[...].astype(BF16)

    @pl.when(used)
    def _():
        x = xs_ref[...]
        he = _silu(_dot(x, wgb[...])) * _dot(x, wub[...])
        ys_ref[...] = _dot(he.astype(BF16), wdb[...]).astype(BF16)

    @pl.when(jnp.logical_not(used))
    def _():
        ys_ref[...] = jnp.zeros(ys_ref.shape, BF16)


def _grouped_mlp(xs, tile_expert, n_used, wg, wu, wd, layer):
    n_rows, d = xs.shape
    te = EXPERT_TILE
    row_map = lambda i, tex, nu: (jnp.minimum(i, nu[0] - 1), 0)
    grid_spec = pltpu.PrefetchScalarGridSpec(
        num_scalar_prefetch=2,
        grid=(n_rows // te,),
        in_specs=[pl.BlockSpec((te, d), row_map),
                  pl.BlockSpec((None, None, d, D_EXPERT), lambda i, tex, nu: (layer, tex[i], 0, 0)),
                  pl.BlockSpec((None, None, d, D_EXPERT), lambda i, tex, nu: (layer, tex[i], 0, 0)),
                  pl.BlockSpec((None, None, D_EXPERT, d), lambda i, tex, nu: (layer, tex[i], 0, 0))],
        out_specs=pl.BlockSpec((te, d), lambda i, tex, nu: (i, 0)),
        scratch_shapes=[pltpu.VMEM((d, D_EXPERT), BF16), pltpu.VMEM((d, D_EXPERT), BF16),
                        pltpu.VMEM((D_EXPERT, d), BF16)],
    )
    return pl.pallas_call(
        _gmm_kernel,
        grid_spec=grid_spec,
        out_shape=jax.ShapeDtypeStruct((n_rows, d), BF16),
        compiler_params=_cparams(("arbitrary",)),
        name="moe_grouped_mlp",
    )(tile_expert, n_used, xs, wg, wu, wd)


def _combine_kernel(moves_ref, ls_ref, gate_ref, x1_ref, g2_ref, ys_ref, o_ref, buf_ref, sem_ref):
    i = pl.program_id(0)
    n = pl.num_programs(0)
    tm = MOE_TOK_TILE
    n_sub = x1_ref.shape[0] // tm
    n_tiles = n * n_sub
    n_buf = buf_ref.shape[0]

    def fetch(tile):
        s = tile % n_buf
        _tile_copies(moves_ref, tile, lambda dst, src, size: pltpu.make_async_copy(
            ys_ref.at[pl.ds(src, size), :], buf_ref.at[s, pl.ds(dst, size), :], sem_ref.at[s]))

    @pl.when(i == 0)
    def _():
        buf_ref[...] = jnp.zeros(buf_ref.shape, BF16)
        for first in range(COMBINE_AHEAD):
            @pl.when(first < n_tiles)
            def _():
                fetch(first)

    for u in range(n_sub):
        tile = i * n_sub + u
        slot = tile % n_buf

        @pl.when(tile + COMBINE_AHEAD < n_tiles)
        def _():
            fetch(tile + COMBINE_AHEAD)

        n_pieces = moves_ref[tile * MOVE_WIDTH + MOVE_PIECES]
        _wait_rows(n_pieces, lambda rows: pltpu.make_async_copy(
            ys_ref.at[pl.ds(0, rows), :], buf_ref.at[slot, pl.ds(0, rows), :], sem_ref.at[slot]))

        rows_u = pl.ds(u * tm, tm)
        srow = _slot_rows(tm)
        ls = ls_ref[:, u * tm:(u + 1) * tm]
        gate = gate_ref[:, u * tm:(u + 1) * tm]
        weights = (jnp.where(srow == ls[0:1], gate[0:1], 0.0)
                   + jnp.where(srow == ls[1:2], gate[1:2], 0.0)).astype(BF16)
        y = _dot_tn(weights, buf_ref[slot])
        o_ref[rows_u, :] = x1_ref[rows_u, :] + g2_ref[0] * y


def _combine(ys, ls, gate, x1, g2, moves, seq_len):
    t, d = x1.shape
    tm = MOE_TOK_TILE * _tiles_per_step(seq_len // MOE_TOK_TILE, COMBINE_TILES_PER_STEP)
    per_batch = seq_len // tm
    grid_spec = pltpu.PrefetchScalarGridSpec(
        num_scalar_prefetch=1,
        grid=(t // tm,),
        in_specs=[pl.BlockSpec((2, tm), lambda i, *_: (0, i)),
                  pl.BlockSpec((2, tm), lambda i, *_: (0, i)),
                  pl.BlockSpec((tm, d), lambda i, *_: (i, 0)),
                  pl.BlockSpec((1, 1, d), lambda i, *_: (i // per_batch, 0, 0)),
                  pl.BlockSpec(memory_space=pl.ANY)],
        out_specs=pl.BlockSpec((tm, d), lambda i, *_: (i, 0)),
        scratch_shapes=[pltpu.VMEM((COMBINE_AHEAD + 1, MOE_SLOTS, d), BF16),
                        pltpu.SemaphoreType.DMA((COMBINE_AHEAD + 1,))],
    )
    return pl.pallas_call(
        _combine_kernel,
        grid_spec=grid_spec,
        out_shape=jax.ShapeDtypeStruct((t, d), F32),
        compiler_params=_cparams(("arbitrary",)),
        name="moe_combine",
    )(moves, ls, gate, x1, g2.reshape(-1, 1, d), ys)


def kernel(x, c, ada_w, ada_b, norm1_w, norm2_w, w_in, pool_w, pool_scale, hgrn_lb_raw, hgrn_norm_w, q_norm_w,
           k_norm_w, attn_sinks, w_out, router_w, router_bias, expert_w_gate, expert_w_up, expert_w_down):
    b, s, d = x.shape
    depth = ada_w.shape[0]
    t = b * s
    n_rows = _moe_rows(t)

    p = jax.nn.softmax(hgrn_lb_raw.astype(F32), axis=0)
    lower_bounds = jnp.maximum(jnp.cumsum(p, axis=0) - p[0:1], 0.0)

    mod = _modulation(c, ada_w, ada_b)
    hgrn_consts = _hgrn_constants() + (_head_ones(HGRN_WIDTH, HGRN_KDIM),)
    attn_consts = (_attn_bias(), _head_ones(ATT_WIDTH, HEAD_DIM), _head_ones(KV_WIDTH, HEAD_DIM))
    router_consts = _router_constants(MOE_TOK_TILE)
    rwt_b = router_w.T.astype(BF16)
    w_in_b = w_in.astype(BF16)
    w_out_b = w_out.astype(BF16)

    for l in range(depth):
        sh1, sc1, g1, sh2, sc2, g2 = [mod[l, :, j * d:(j + 1) * d] for j in range(6)]
        yp, zh, zq, zkv = _in_proj(x, sc1, sh1, norm1_w[l], w_in_b, l, _block_diag(pool_w[l]).astype(BF16),
                                   pool_scale[l])
        yh = _hgrn_mixer(zh, lower_bounds[l], hgrn_norm_w[l], hgrn_consts)
        ya = _attn_mixer(zq, zkv, q_norm_w[l], k_norm_w[l], attn_sinks[l], attn_consts)
        x1, h2, ls, gate, cnt_out = _out_proj_router(
            yp, yh, ya, x, g1, sc2, sh2, norm2_w[l], w_out_b, l, rwt_b, router_bias, router_consts)
        moves, tail, tile_expert, n_used = _moe_tables(cnt_out)
        xs = _sort_tokens(h2, ls, moves, tail, n_rows)
        ys = _grouped_mlp(xs, tile_expert, n_used, expert_w_gate, expert_w_up, expert_w_down, l)
        x = _combine(ys, ls, gate, x1, g2, moves, s).reshape(b, s, d)
    return x
```
